```python
import jax, jax.numpy as jnp
from jax import lax
import numpy as np

D_MODEL = 1024
BATCH = 4
SEQ = 8192
DEPTH = 4
DEC_BATCH = 8
DEC_SEQ = 4096
PAST_LEN = 128

GRID_W = 64
NA_WIDTH = D_MODEL // 2
NA_HEADS = 8
NA_HEAD_DIM = NA_WIDTH // NA_HEADS
WIN_ROWS = 8
WIN_COLS = 16
SG_WIDTH = D_MODEL // 2
SG_HEADS = 8
SG_HEAD_DIM = SG_WIDTH // SG_HEADS
SG_CHUNK = 128
MIX_WIDTH = NA_WIDTH + SG_WIDTH
IN_WIDTH = 3 * NA_WIDTH + 2 * SG_WIDTH
D_FF_DENSE = 11 * D_MODEL // 4
N_EXPERTS = 8
TOP_K = 2
D_FF_EXPERT = 7 * D_MODEL // 2
EXPERT_BLOCK = 128
N_DENSE = (DEPTH + 1) // 2
N_MOE = DEPTH // 2
RMS_EPS = 1e-6

kernel_name = "hymba_style_natten_gmlp_moe_encoder"


def rmsnorm(x, g):
    xf = x.astype(jnp.float32)
    r = lax.rsqrt(jnp.mean(xf * xf, axis=-1, keepdims=True) + RMS_EPS)
    return (xf * r * g.astype(jnp.float32)).astype(x.dtype)


def swiglu(h, w_gate, w_up, w_down):
    return (jax.nn.silu(h @ w_gate) * (h @ w_up)) @ w_down


def neighborhood_attention(q, k, v, rpb):
    B, S, H, dh = q.shape
    R = S // GRID_W
    KH = min(WIN_ROWS, R)
    W = GRID_W
    qg = q.reshape(B, R, W, H, dh)
    row_ids = jnp.arange(R)
    row_start = jnp.clip(row_ids - KH // 2, 0, R - KH)
    rows = row_start[:, None] + jnp.arange(KH)[None, :]
    kg = k.reshape(B, R, W, H, dh)[:, rows]
    vg = v.reshape(B, R, W, H, dh)[:, rows]
    scores = jnp.einsum('brqhd,brjkhd->brqhjk', qg, kg,
                        preferred_element_type=jnp.float32) * (dh ** -0.5)
    col_ids = jnp.arange(W)
    col_start = jnp.clip(col_ids - WIN_COLS // 2, 0, W - WIN_COLS)
    col_in = (col_ids[None, :] >= col_start[:, None]) & (col_ids[None, :] < col_start[:, None] + WIN_COLS)
    dr = rows - row_ids[:, None] + (WIN_ROWS - 1)
    dc = jnp.clip(col_ids[None, :] - col_ids[:, None], -(WIN_COLS - 1), WIN_COLS - 1) + (WIN_COLS - 1)
    bias = rpb[:, dr[:, None, :, None], dc[None, :, None, :]]
    bias = jnp.transpose(bias, (1, 2, 0, 3, 4)).astype(jnp.float32)
    scores = jnp.where(col_in[None, None, :, None, None, :], scores + bias[None], -jnp.inf)
    p = jax.nn.softmax(scores.reshape(B, R, W, H, KH * W), axis=-1).reshape(B, R, W, H, KH, W)
    out = jnp.einsum('brqhjk,brjkhd->brqhd', p.astype(v.dtype), vg)
    return out.reshape(B, S, H, dh)


def spatial_gating(u, v, g_norm, ws, bs):
    B, S, Hg, dg = u.shape
    v = rmsnorm(v, g_norm)
    vc = v.reshape(B, S // SG_CHUNK, SG_CHUNK, Hg, dg)
    mixed = jnp.einsum('hij,bnjhd->bnihd', ws, vc) + jnp.transpose(bs)[None, None, :, :, None]
    return u * mixed.reshape(B, S, Hg, dg)


def moe_swiglu(h, router_w, w_gate, w_up, w_down):
    B, S, D = h.shape
    T = B * S
    xt = h.reshape(T, D)
    logits = jnp.matmul(xt, router_w, preferred_element_type=jnp.float32)
    top_val, top_idx = lax.top_k(logits, TOP_K)
    gates = jax.nn.softmax(top_val, axis=-1)
    A = T * TOP_K
    e_flat = top_idx.reshape(A).astype(jnp.int32)
    g_flat = gates.reshape(A)
    tok = jnp.arange(A, dtype=jnp.int32) // TOP_K
    order = jnp.argsort(e_flat)
    e_s = e_flat[order]
    tok_s = tok[order]
    counts = jnp.zeros((N_EXPERTS,), jnp.int32).at[e_flat].add(1)
    padded = (counts + EXPERT_BLOCK - 1) // EXPERT_BLOCK * EXPERT_BLOCK
    pad_end = jnp.cumsum(padded)
    pad_start = pad_end - padded
    start = jnp.cumsum(counts) - counts
    dest = pad_start[e_s] + jnp.arange(A, dtype=jnp.int32) - start[e_s]
    P = A + N_EXPERTS * EXPERT_BLOCK
    n_blk = P // EXPERT_BLOCK
    buf = jnp.zeros((P, D), h.dtype).at[dest].set(xt[tok_s])
    blk_expert = jnp.minimum(
        jnp.searchsorted(pad_end, jnp.arange(n_blk, dtype=jnp.int32) * EXPERT_BLOCK, side='right'),
        N_EXPERTS - 1)

    def expert_block(args):
        xb, e = args
        return swiglu(xb, w_gate[e], w_up[e], w_down[e])

    out = lax.map(expert_block, (buf.reshape(n_blk, EXPERT_BLOCK, D), blk_expert)).reshape(P, D)
    y_s = out[dest] * g_flat[order][:, None].astype(h.dtype)
    y = jnp.zeros((T, D), h.dtype).at[tok_s].add(y_s)
    return y.reshape(B, S, D)


def trunk(x, norm_mix_g, w_in, na_rpb, sg_norm_g, sg_w, sg_b, out_norm_g, w_out,
          norm_ffn_g, dense_w_gate, dense_w_up, dense_w_down,
          router_w, moe_w_gate, moe_w_up, moe_w_down, final_norm_g):
    B, S, _ = x.shape
    for l in range(DEPTH):
        h = rmsnorm(x, norm_mix_g[l])
        proj = h @ w_in[l]
        q, k, v, u, sv = jnp.split(
            proj, [NA_WIDTH, 2 * NA_WIDTH, 3 * NA_WIDTH, 3 * NA_WIDTH + SG_WIDTH], axis=-1)
        q = q.reshape(B, S, NA_HEADS, NA_HEAD_DIM)
        k = k.reshape(B, S, NA_HEADS, NA_HEAD_DIM)
        v = v.reshape(B, S, NA_HEADS, NA_HEAD_DIM)
        attn = neighborhood_attention(q, k, v, na_rpb[l]).reshape(B, S, NA_WIDTH)
        u = jax.nn.gelu(u).reshape(B, S, SG_HEADS, SG_HEAD_DIM)
        sv = jax.nn.gelu(sv).reshape(B, S, SG_HEADS, SG_HEAD_DIM)
        sg = spatial_gating(u, sv, sg_norm_g[l], sg_w[l], sg_b[l]).reshape(B, S, SG_WIDTH)
        mix = jnp.concatenate([rmsnorm(attn, out_norm_g[l, :NA_WIDTH]),
                               rmsnorm(sg, out_norm_g[l, NA_WIDTH:])], axis=-1)
        x = x + mix @ w_out[l]
        h2 = rmsnorm(x, norm_ffn_g[l])
        if l % 2 == 0:
            i = l // 2
            x = x + swiglu(h2, dense_w_gate[i], dense_w_up[i], dense_w_down[i])
        else:
            i = l // 2
            x = x + moe_swiglu(h2, router_w[i], moe_w_gate[i], moe_w_up[i], moe_w_down[i])
    return rmsnorm(x, final_norm_g)


def setup_inputs(seed: int = 0) -> dict:
    key = jax.random.key(seed)
    ks = jax.random.split(key, 20)
    f32 = jnp.float32
    nrm = lambda k, shape, scale: jax.random.normal(k, shape, f32) * scale
    return {
        "x_prompt": nrm(ks[0], (BATCH, SEQ, D_MODEL), 1.0),
        "x_sample": nrm(ks[1], (DEC_BATCH, DEC_SEQ, D_MODEL), 1.0),
        "norm_mix_g": 1.0 + nrm(ks[2], (DEPTH, D_MODEL), 0.05),
        "w_in": nrm(ks[3], (DEPTH, D_MODEL, IN_WIDTH), D_MODEL ** -0.5),
        "na_rpb": nrm(ks[4], (DEPTH, NA_HEADS, 2 * WIN_ROWS - 1, 2 * WIN_COLS - 1), 0.2),
        "sg_norm_g": 1.0 + nrm(ks[5], (DEPTH, SG_HEADS, SG_HEAD_DIM), 0.05),
        "sg_w": nrm(ks[6], (DEPTH, SG_HEADS, SG_CHUNK, SG_CHUNK), SG_CHUNK ** -0.5),
        "sg_b": 1.0 + nrm(ks[7], (DEPTH, SG_HEADS, SG_CHUNK), 0.1),
        "out_norm_g": 1.0 + nrm(ks[8], (DEPTH, MIX_WIDTH), 0.05),
        "w_out": nrm(ks[9], (DEPTH, MIX_WIDTH, D_MODEL), MIX_WIDTH ** -0.5),
        "norm_ffn_g": 1.0 + nrm(ks[10], (DEPTH, D_MODEL), 0.05),
        "dense_w_gate": nrm(ks[11], (N_DENSE, D_MODEL, D_FF_DENSE), D_MODEL ** -0.5),
        "dense_w_up": nrm(ks[12], (N_DENSE, D_MODEL, D_FF_DENSE), D_MODEL ** -0.5),
        "dense_w_down": nrm(ks[13], (N_DENSE, D_FF_DENSE, D_MODEL), D_FF_DENSE ** -0.5),
        "router_w": nrm(ks[14], (N_MOE, D_MODEL, N_EXPERTS), D_MODEL ** -0.5),
        "moe_w_gate": nrm(ks[15], (N_MOE, N_EXPERTS, D_MODEL, D_FF_EXPERT), D_MODEL ** -0.5),
        "moe_w_up": nrm(ks[16], (N_MOE, N_EXPERTS, D_MODEL, D_FF_EXPERT), D_MODEL ** -0.5),
        "moe_w_down": nrm(ks[17], (N_MOE, N_EXPERTS, D_FF_EXPERT, D_MODEL), D_FF_EXPERT ** -0.5),
        "final_norm_g": 1.0 + nrm(ks[18], (D_MODEL,), 0.05),
    }


def reference(x_prompt, x_sample, norm_mix_g, w_in, na_rpb, sg_norm_g, sg_w, sg_b, out_norm_g, w_out,
              norm_ffn_g, dense_w_gate, dense_w_up, dense_w_down,
              router_w, moe_w_gate, moe_w_up, moe_w_down, final_norm_g):
    y_prompt = trunk(x_prompt, norm_mix_g, w_in, na_rpb, sg_norm_g, sg_w, sg_b, out_norm_g, w_out,
                     norm_ffn_g, dense_w_gate, dense_w_up, dense_w_down,
                     router_w, moe_w_gate, moe_w_up, moe_w_down, final_norm_g)
    y_sample = trunk(x_sample, norm_mix_g, w_in, na_rpb, sg_norm_g, sg_w, sg_b, out_norm_g, w_out,
                     norm_ffn_g, dense_w_gate, dense_w_up, dense_w_down,
                     router_w, moe_w_gate, moe_w_up, moe_w_down, final_norm_g)
    return (y_prompt, y_sample)
```

```python
import functools

import numpy as np
import jax
import jax.numpy as jnp
from jax import lax
from jax.experimental import pallas as pl
from jax.experimental.pallas import tpu as pltpu

F32 = jnp.float32
BF16 = jnp.bfloat16

GRID_W = 64
WIN_ROWS = 8
WIN_COLS = 16
N_HEADS = 8
HEAD_DIM = 64
HALF = N_HEADS * HEAD_DIM
N_PAIRS = N_HEADS // 2
LANES = 128
SG_CHUNK = 128
N_EXPERTS = 8
RMS_EPS = 1e-6
MASK_VALUE = -1e30

TM = 512
ROWS_PER_BLOCK = TM // GRID_W
EXPERT_TILE = 512
VMEM_LIMIT = 56 * 1024 * 1024


def _params(sem, vmem=VMEM_LIMIT):
    return pltpu.CompilerParams(dimension_semantics=sem, vmem_limit_bytes=vmem)


def _const_spec(shape):
    nd = len(shape)
    return pl.BlockSpec(shape, lambda *_: (0,) * nd, pipeline_mode=pl.Buffered(1))


def _rms_scale(x):
    return lax.rsqrt(jnp.mean(x * x, axis=-1, keepdims=True) + RMS_EPS)


def _gelu_tanh(x):
    return x * (0.5 * (1.0 + jnp.tanh(0.7978845608028654 * (x + 0.044715 * (x * x * x)))))


def _inproj_kernel(x_ref, g_ref, w_ref, sgg_ref, bd_ref, q_ref, k_ref, v_ref, u_ref, sv_ref):
    x = x_ref[...]
    h = (x * _rms_scale(x) * g_ref[...]).astype(BF16)

    def proj(n):
        return jnp.dot(h, w_ref[:, n * HALF:(n + 1) * HALF], preferred_element_type=F32)

    q_ref[...] = (proj(0) * (HEAD_DIM ** -0.5)).astype(BF16)
    k_ref[...] = proj(1).astype(BF16)
    v_ref[...] = proj(2).astype(BF16)
    u_ref[...] = _gelu_tanh(proj(3)).astype(BF16)
    sv = _gelu_tanh(proj(4))
    ssq = jnp.dot((sv * sv).astype(BF16), bd_ref[...], preferred_element_type=F32)
    sv_ref[...] = (sv * lax.rsqrt(ssq * (1.0 / HEAD_DIM) + RMS_EPS) * sgg_ref[...]).astype(BF16)


def _inproj(x, g, w_in, sgg, bd):
    T, D = x.shape
    out = jax.ShapeDtypeStruct((T, HALF), BF16)
    tile = pl.BlockSpec((TM, HALF), lambda i: (i, 0))
    return pl.pallas_call(
        _inproj_kernel,
        grid=(T // TM,),
        in_specs=[pl.BlockSpec((TM, D), lambda i: (i, 0)),
                  _const_spec((1, D)), _const_spec(w_in.shape),
                  _const_spec((1, HALF)), _const_spec((HALF, HALF))],
        out_specs=[tile] * 5,
        out_shape=[out] * 5,
        compiler_params=_params(("parallel",)),
        name="inproj",
    )(x, g, w_in, sgg, bd)


def _natten_kernel(lo_ref, hi_ref, q_ref, kp_ref, kc_ref, kn_ref, vp_ref, vc_ref, vn_ref,
                   bias_ref, o_ref, kwin, vwin, qe_s, qo_s):
    b = pl.program_id(0)
    first = lo_ref[b]
    blk = b - first
    n_rows = (hi_ref[b] - first + 1) * ROWS_PER_BLOCK

    kwin[0:TM] = kp_ref[...]
    kwin[TM:2 * TM] = kc_ref[...]
    kwin[2 * TM:3 * TM] = kn_ref[...]
    vwin[0:TM] = vp_ref[...]
    vwin[TM:2 * TM] = vc_ref[...]
    vwin[2 * TM:3 * TM] = vn_ref[...]

    qf = q_ref[...].astype(F32)
    lane = lax.broadcasted_iota(jnp.int32, qf.shape, 1)
    is_even = (lane % LANES) < HEAD_DIM
    qe_s[...] = jnp.where(is_even, qf, 0.0).astype(BF16)
    qo_s[...] = jnp.where(is_even, 0.0, qf).astype(BF16)

    out_even = lax.broadcasted_iota(jnp.int32, (GRID_W, LANES), 1) < HEAD_DIM

    def row_body(i, carry):
        r = blk * ROWS_PER_BLOCK + i
        rs = jnp.clip(r - WIN_ROWS // 2, 0, n_rows - WIN_ROWS)
        ws = rs - (blk - 1) * ROWS_PER_BLOCK
        variant = rs - r + (WIN_ROWS - 1)
        koff = pl.multiple_of(ws * GRID_W, GRID_W)
        qoff = pl.multiple_of(i * GRID_W, GRID_W)
        for p in range(N_PAIRS):
            cols = slice(p * LANES, (p + 1) * LANES)
            q2 = jnp.concatenate([qe_s[pl.ds(qoff, GRID_W), cols],
                                  qo_s[pl.ds(qoff, GRID_W), cols]], axis=0)
            kw = kwin[pl.ds(koff, WIN_ROWS * GRID_W), cols]
            s = lax.dot_general(q2, kw, (((1,), (1,)), ((), ())),
                                preferred_element_type=F32)
            s = s + bias_ref[variant, p]
            m = jnp.max(s, axis=-1, keepdims=True)
            e = jnp.exp(s - m)
            denom = jnp.sum(e, axis=-1, keepdims=True)
            vw = vwin[pl.ds(koff, WIN_ROWS * GRID_W), cols]
            o2 = jnp.dot(e.astype(BF16), vw, preferred_element_type=F32) * (1.0 / denom)
            out = jnp.where(out_even, o2[:GRID_W], o2[GRID_W:])
            o_ref[pl.ds(qoff, GRID_W), cols] = out.astype(BF16)
        return carry

    lax.fori_loop(0, ROWS_PER_BLOCK, row_body, 0)


def _natten(q, k, v, bias, blk_lo, blk_hi):
    T = q.shape[0]
    cur = pl.BlockSpec((TM, HALF), lambda b, lo, hi: (b, 0))
    prev = pl.BlockSpec((TM, HALF), lambda b, lo, hi: (jnp.maximum(b - 1, lo[b]), 0))
    nxt = pl.BlockSpec((TM, HALF), lambda b, lo, hi: (jnp.minimum(b + 1, hi[b]), 0))
    grid_spec = pltpu.PrefetchScalarGridSpec(
        num_scalar_prefetch=2,
        grid=(T // TM,),
        in_specs=[cur, prev, cur, nxt, prev, cur, nxt,
                  pl.BlockSpec(bias.shape, lambda b, lo, hi: (0, 0, 0, 0),
                               pipeline_mode=pl.Buffered(1))],
        out_specs=cur,
        scratch_shapes=[pltpu.VMEM((3 * TM, HALF), BF16), pltpu.VMEM((3 * TM, HALF), BF16),
                        pltpu.VMEM((TM, HALF), BF16), pltpu.VMEM((TM, HALF), BF16)],
    )
    return pl.pallas_call(
        _natten_kernel,
        grid_spec=grid_spec,
        out_shape=jax.ShapeDtypeStruct((T, HALF), BF16),
        compiler_params=_params(("parallel",)),
        name="natten",
    )(blk_lo, blk_hi, q, k, k, k, v, v, v, bias)


def _na_bias_table(rpb):
    qc = np.arange(GRID_W)
    kc = np.arange(GRID_W)
    cs = np.clip(qc - WIN_COLS // 2, 0, GRID_W - WIN_COLS)
    col_in = (kc[None, :] >= cs[:, None]) & (kc[None, :] < cs[:, None] + WIN_COLS)
    dc = np.clip(kc[None, :] - qc[:, None], -(WIN_COLS - 1), WIN_COLS - 1) + (WIN_COLS - 1)
    dr = np.arange(WIN_ROWS)[:, None] + np.arange(WIN_ROWS)[None, :]
    t = rpb.astype(F32)[:, dr[:, :, None, None], dc[None, None, :, :]]
    t = jnp.where(col_in[None, None, None], t, MASK_VALUE)
    t = jnp.transpose(t, (1, 0, 3, 2, 4))
    return t.reshape(WIN_ROWS, N_PAIRS, 2 * GRID_W, WIN_ROWS * GRID_W)


def _mixout_kernel(a_ref, u_ref, sv_ref, x_ref, sgw_ref, sgb_ref, g_ref, w_ref, o_ref, sg_s):
    even = lax.broadcasted_iota(jnp.int32, (SG_CHUNK, LANES), 1) < HEAD_DIM
    for c in range(TM // SG_CHUNK):
        rows = slice(c * SG_CHUNK, (c + 1) * SG_CHUNK)
        for p in range(N_PAIRS):
            cols = slice(p * LANES, (p + 1) * LANES)
            m2 = jnp.dot(sgw_ref[p], sv_ref[rows, cols], preferred_element_type=F32)
            mixed = jnp.where(even, m2[:SG_CHUNK], m2[SG_CHUNK:])
            sg_s[rows, cols] = u_ref[rows, cols].astype(F32) * (mixed + sgb_ref[:, cols])
    sg = sg_s[...]
    sgn = (sg * _rms_scale(sg) * g_ref[:, HALF:]).astype(BF16)
    a = a_ref[...].astype(F32)
    an = (a * _rms_scale(a) * g_ref[:, :HALF]).astype(BF16)
    y = jnp.dot(an, w_ref[:HALF, :], preferred_element_type=F32)
    y = y + jnp.dot(sgn, w_ref[HALF:, :], preferred_element_type=F32)
    o_ref[...] = x_ref[...] + y


def _mixout(attn, u, svn, x, sgw2, sgb_t, g_out, w_out):
    T, D = x.shape
    half = pl.BlockSpec((TM, HALF), lambda i: (i, 0))
    full = pl.BlockSpec((TM, D), lambda i: (i, 0))
    return pl.pallas_call(
        _mixout_kernel,
        grid=(T // TM,),
        in_specs=[half, half, half, full,
                  _const_spec(sgw2.shape), _const_spec(sgb_t.shape),
                  _const_spec((1, 2 * HALF)), _const_spec(w_out.shape)],
        out_specs=full,
        out_shape=jax.ShapeDtypeStruct((T, D), F32),
        scratch_shapes=[pltpu.VMEM((TM, HALF), F32)],
        compiler_params=_params(("parallel",)),
        name="mixout",
    )(attn, u, svn, x, sgw2, sgb_t, g_out, w_out)


def _ff_chunks(d_ff):
    n = 1
    while d_ff // n > 2048 or d_ff % n or (d_ff // n) % LANES:
        n += 1
    return n


def _swiglu(h, wg_ref, wu_ref, wd_ref, n_chunks):
    d_ff = wd_ref.shape[-2]
    fc = d_ff // n_chunks
    y = None
    for c in range(n_chunks):
        cols = slice(c * fc, (c + 1) * fc)
        gate = jnp.dot(h, wg_ref[:, cols], preferred_element_type=F32)
        up = jnp.dot(h, wu_ref[:, cols], preferred_element_type=F32)
        act = (gate / (1.0 + jnp.exp(-gate)) * up).astype(BF16)
        part = jnp.dot(act, wd_ref[cols, :], preferred_element_type=F32)
        y = part if y is None else y + part
    return y


def _ffn_kernel(x_ref, g_ref, wg_ref, wu_ref, wd_ref, o_ref, *, n_chunks):
    x = x_ref[...]
    h = (x * _rms_scale(x) * g_ref[...]).astype(BF16)
    o_ref[...] = x + _swiglu(h, wg_ref, wu_ref, wd_ref, n_chunks)


def _ffn(x, g, wg, wu, wd):
    T, D = x.shape
    full = pl.BlockSpec((TM, D), lambda i: (i, 0))
    return pl.pallas_call(
        functools.partial(_ffn_kernel, n_chunks=_ff_chunks(wg.shape[1])),
        grid=(T // TM,),
        in_specs=[full, _const_spec((1, D)), _const_spec(wg.shape), _const_spec(wu.shape),
                  _const_spec(wd.shape)],
        out_specs=full,
        out_shape=jax.ShapeDtypeStruct((T, D), F32),
        compiler_params=_params(("parallel",)),
        name="ffn",
    )(x, g, wg, wu, wd)


def _route_kernel(x_ref, g_ref, rw_ref, tri_ref, idx_ref, gate_ref, cnt_ref, carry):
    @pl.when(pl.program_id(0) == 0)
    def _():
        carry[...] = jnp.zeros_like(carry)

    x = x_ref[...]
    h = x * _rms_scale(x) * g_ref[...]
    logits = jnp.dot(h, rw_ref[...], precision=lax.Precision.HIGHEST, preferred_element_type=F32)
    col = lax.broadcasted_iota(jnp.int32, logits.shape, 1)
    colf = col.astype(F32)
    neg = -jnp.inf
    lg = jnp.where(col < N_EXPERTS, logits, neg)
    m1 = jnp.max(lg, axis=-1, keepdims=True)
    i1 = jnp.min(jnp.where(lg == m1, colf, float(LANES)), axis=-1, keepdims=True)
    sel1 = colf == i1
    lg2 = jnp.where(sel1, neg, lg)
    m2 = jnp.max(lg2, axis=-1, keepdims=True)
    i2 = jnp.min(jnp.where(lg2 == m2, colf, float(LANES)), axis=-1, keepdims=True)
    sel2 = colf == i2
    e2 = jnp.exp(m2 - m1)
    g1 = 1.0 / (1.0 + e2)
    g2 = e2 / (1.0 + e2)

    cnt = jnp.where(sel1 | sel2, 1.0, 0.0)
    before = jnp.dot(tri_ref[...], cnt.astype(BF16), preferred_element_type=F32) + carry[...]
    r1 = jnp.sum(jnp.where(sel1, before, 0.0), axis=-1, keepdims=True)
    r2 = jnp.sum(jnp.where(sel2, before, 0.0), axis=-1, keepdims=True)
    total = carry[...] + jnp.sum(cnt, axis=0, keepdims=True)
    carry[...] = total
    cnt_ref[...] = jnp.broadcast_to(total, cnt_ref.shape)

    meta = jnp.where(col == 0, i1, jnp.where(col == 1, i2, jnp.where(col == 2, r1, r2)))
    idx_ref[...] = meta.astype(jnp.int32)
    gate_ref[...] = jnp.where(col == 0, g1, g2)


def _route(x, g, rw_pad, tri):
    T, D = x.shape
    meta = pl.BlockSpec((TM, LANES), lambda i: (i, 0))
    return pl.pallas_call(
        _route_kernel,
        grid=(T // TM,),
        in_specs=[pl.BlockSpec((TM, D), lambda i: (i, 0)), _const_spec((1, D)),
                  _const_spec(rw_pad.shape), _const_spec(tri.shape)],
        out_specs=[meta, meta, pl.BlockSpec((8, LANES), lambda i: (0, 0))],
        out_shape=[jax.ShapeDtypeStruct((T, LANES), jnp.int32),
                   jax.ShapeDtypeStruct((T, LANES), F32),
                   jax.ShapeDtypeStruct((8, LANES), F32)],
        scratch_shapes=[pltpu.VMEM((1, LANES), F32)],
        compiler_params=_params(("arbitrary",)),
        name="route",
    )(x, g, rw_pad, tri)


def _dispatch_kernel(dest_ref, x_ref, g_ref, buf_in, buf_out, h_s, sem):
    del buf_in
    x = x_ref[...]
    h_s[...] = x * _rms_scale(x) * g_ref[...]

    def row_copy(t, slot):
        d = dest_ref[2 * t + slot]
        return pltpu.make_async_copy(h_s.at[pl.ds(t, 1)], buf_out.at[pl.ds(d, 1)], sem)

    def issue(t, carry):
        row_copy(t, 0).start()
        row_copy(t, 1).start()
        return carry

    lax.fori_loop(0, TM, issue, 0, unroll=8)
    for _ in range(2):
        pltpu.make_async_copy(h_s, buf_out.at[pl.ds(0, TM)], sem).wait()


def _dispatch(x, g, dest, n_rows):
    T, D = x.shape
    zeros = jnp.zeros((n_rows, D), F32)
    return pl.pallas_call(
        _dispatch_kernel,
        grid=(T // TM,),
        in_specs=[pl.BlockSpec((2 * TM,), lambda i: (i,), memory_space=pltpu.SMEM),
                  pl.BlockSpec((TM, D), lambda i: (i, 0)), _const_spec((1, D)),
                  pl.BlockSpec(memory_space=pl.ANY)],
        out_specs=pl.BlockSpec(memory_space=pl.ANY),
        out_shape=jax.ShapeDtypeStruct((n_rows, D), F32),
        scratch_shapes=[pltpu.VMEM((TM, D), F32), pltpu.SemaphoreType.DMA(())],
        input_output_aliases={3: 0},
        compiler_params=_params(("arbitrary",)),
        name="dispatch",
    )(dest, x, g, zeros)


def _experts_kernel(te_ref, nu_ref, x_ref, wg_ref, wu_ref, wd_ref, o_ref, *, n_chunks):
    i = pl.program_id(0)

    @pl.when(i < nu_ref[0])
    def _():
        o_ref[...] = _swiglu(x_ref[...].astype(BF16), wg_ref.at[0], wu_ref.at[0], wd_ref.at[0],
                             n_chunks)

    @pl.when(i >= nu_ref[0])
    def _():
        o_ref[...] = jnp.zeros_like(o_ref)


def _experts(buf, tile_expert, n_used, wg, wu, wd):
    P, D = buf.shape
    d_ff = wg.shape[-1]
    last = lambda i, te, nu: jnp.minimum(i, nu[0] - 1)
    grid_spec = pltpu.PrefetchScalarGridSpec(
        num_scalar_prefetch=2,
        grid=(P // EXPERT_TILE,),
        in_specs=[pl.BlockSpec((EXPERT_TILE, D), lambda i, te, nu: (last(i, te, nu), 0)),
                  pl.BlockSpec((1, D, d_ff), lambda i, te, nu: (te[i], 0, 0),
                               pipeline_mode=pl.Buffered(1)),
                  pl.BlockSpec((1, D, d_ff), lambda i, te, nu: (te[i], 0, 0),
                               pipeline_mode=pl.Buffered(1)),
                  pl.BlockSpec((1, d_ff, D), lambda i, te, nu: (te[i], 0, 0),
                               pipeline_mode=pl.Buffered(1))],
        out_specs=pl.BlockSpec((EXPERT_TILE, D), lambda i, te, nu: (i, 0)),
    )
    return pl.pallas_call(
        functools.partial(_experts_kernel, n_chunks=_ff_chunks(d_ff)),
        grid_spec=grid_spec,
        out_shape=jax.ShapeDtypeStruct((P, D), F32),
        compiler_params=_params(("arbitrary",)),
        name="experts",
    )(tile_expert, n_used, buf, wg, wu, wd)


def _combine_kernel(dest_ref, x_ref, gate_ref, g_ref, ys_ref, o_ref, a_s, b_s, sem, *, final_norm):
    def row_copy(t, slot, dst):
        d = dest_ref[2 * t + slot]
        return pltpu.make_async_copy(ys_ref.at[pl.ds(d, 1)], dst.at[pl.ds(t, 1)], sem)

    def issue(t, carry):
        row_copy(t, 0, a_s).start()
        row_copy(t, 1, b_s).start()
        return carry

    lax.fori_loop(0, TM, issue, 0, unroll=8)
    for dst in (a_s, b_s):
        pltpu.make_async_copy(ys_ref.at[pl.ds(0, TM)], dst, sem).wait()

    gates = gate_ref[...]
    y = x_ref[...] + (a_s[...] * gates[:, 0:1] + b_s[...] * gates[:, 1:2])
    if final_norm:
        y = y * _rms_scale(y) * g_ref[...]
    o_ref[...] = y


def _combine(x, gates, dest, ys, g_final, final_norm):
    T, D = x.shape
    full = pl.BlockSpec((TM, D), lambda i: (i, 0))
    return pl.pallas_call(
        functools.partial(_combine_kernel, final_norm=final_norm),
        grid=(T // TM,),
        in_specs=[pl.BlockSpec((2 * TM,), lambda i: (i,), memory_space=pltpu.SMEM),
                  full, pl.BlockSpec((TM, LANES), lambda i: (i, 0)), _const_spec((1, D)),
                  pl.BlockSpec(memory_space=pl.ANY)],
        out_specs=full,
        out_shape=jax.ShapeDtypeStruct((T, D), F32),
        scratch_shapes=[pltpu.VMEM((TM, D), F32), pltpu.VMEM((TM, D), F32),
                        pltpu.SemaphoreType.DMA(())],
        compiler_params=_params(("arbitrary",)),
        name="combine",
    )(dest, x, gates, g_final, ys)


def _final_norm_kernel(x_ref, g_ref, o_ref):
    x = x_ref[...]
    o_ref[...] = x * _rms_scale(x) * g_ref[...]


def _final_norm(x, g):
    T, D = x.shape
    full = pl.BlockSpec((TM, D), lambda i: (i, 0))
    return pl.pallas_call(
        _final_norm_kernel,
        grid=(T // TM,),
        in_specs=[full, _const_spec((1, D))],
        out_specs=full,
        out_shape=jax.ShapeDtypeStruct((T, D), F32),
        compiler_params=_params(("parallel",)),
        name="final_norm",
    )(x, g)


def _moe(x, g_ffn, router_w, wg, wu, wd, g_final, final_norm):
    T, D = x.shape
    rw_pad = jnp.zeros((D, LANES), F32).at[:, :N_EXPERTS].set(router_w)
    tri = jnp.asarray(np.tril(np.ones((TM, TM), np.float32), -1), BF16)
    idx, gates, cnt = _route(x, g_ffn, rw_pad, tri)

    counts = cnt[0, :N_EXPERTS].astype(jnp.int32)
    padded = (counts + EXPERT_TILE - 1) // EXPERT_TILE * EXPERT_TILE
    pad_end = jnp.cumsum(padded)
    pad_start = pad_end - padded
    dest = (pad_start[idx[:, 0:2]] + idx[:, 2:4]).reshape(2 * T)
    n_rows = 2 * T + N_EXPERTS * EXPERT_TILE
    n_tiles = n_rows // EXPERT_TILE
    tile_expert = jnp.minimum(
        jnp.searchsorted(pad_end, jnp.arange(n_tiles, dtype=jnp.int32) * EXPERT_TILE, side="right"),
        N_EXPERTS - 1).astype(jnp.int32)
    n_used = (pad_end[-1:] // EXPERT_TILE).astype(jnp.int32)

    buf = _dispatch(x, g_ffn, dest, n_rows)
    ys = _experts(buf, tile_expert, n_used, wg, wu, wd)
    return _combine(x, gates, dest, ys, g_final, final_norm)


def _image_blocks(image_rows):
    lo, hi, start = [], [], 0
    for rows in image_rows:
        n = rows // ROWS_PER_BLOCK
        lo += [start] * n
        hi += [start + n - 1] * n
        start += n
    return jnp.asarray(lo, jnp.int32), jnp.asarray(hi, jnp.int32)


def kernel(x_prompt, x_sample, norm_mix_g, w_in, na_rpb, sg_norm_g, sg_w, sg_b, out_norm_g, w_out,
           norm_ffn_g, dense_w_gate, dense_w_up, dense_w_down,
           router_w, moe_w_gate, moe_w_up, moe_w_down, final_norm_g):
    D = x_prompt.shape[-1]
    depth = w_in.shape[0]
    assert D == 2 * HALF
    image_rows = []
    for xs in (x_prompt, x_sample):
        assert xs.shape[1] % TM == 0
        image_rows += [xs.shape[1] // GRID_W] * xs.shape[0]
    blk_lo, blk_hi = _image_blocks(image_rows)
    n_prompt = x_prompt.shape[0] * x_prompt.shape[1]

    x = jnp.concatenate([x_prompt.reshape(-1, D), x_sample.reshape(-1, D)], axis=0)
    bd = jnp.asarray(np.kron(np.eye(N_HEADS, dtype=np.float32),
                             np.ones((HEAD_DIM, HEAD_DIM), np.float32)), BF16)
    g_final = final_norm_g.reshape(1, D)

    for l in range(depth):
        q, k, v, u, svn = _inproj(x, norm_mix_g[l].reshape(1, D), w_in[l].astype(BF16),
                                  sg_norm_g[l].reshape(1, HALF), bd)
        attn = _natten(q, k, v, _na_bias_table(na_rpb[l]), blk_lo, blk_hi)
        sgw2 = sg_w[l].astype(BF16).reshape(N_PAIRS, 2 * SG_CHUNK, SG_CHUNK)
        sgb_t = jnp.repeat(jnp.transpose(sg_b[l]), HEAD_DIM, axis=1)
        x = _mixout(attn, u, svn, x, sgw2, sgb_t, out_norm_g[l].reshape(1, D), w_out[l].astype(BF16))
        g_ffn = norm_ffn_g[l].reshape(1, D)
        i = l // 2
        if l % 2 == 0:
            x = _ffn(x, g_ffn, dense_w_gate[i].astype(BF16), dense_w_up[i].astype(BF16),
                     dense_w_down[i].astype(BF16))
            if l == depth - 1:
                x = _final_norm(x, g_final)
        else:
            x = _moe(x, g_ffn, router_w[i], moe_w_gate[i].astype(BF16), moe_w_up[i].astype(BF16),
                     moe_w_down[i].astype(BF16), g_final, final_norm=(l == depth - 1))
    return (x[:n_prompt].reshape(x_prompt.shape), x[n_prompt:].reshape(x_sample.shape))
```

```python
import functools

import numpy as np
import jax
import jax.numpy as jnp
from jax import lax
from jax.experimental import pallas as pl
from jax.experimental.pallas import tpu as pltpu

F32 = jnp.float32
BF16 = jnp.bfloat16

GRID_W = 64
WIN_ROWS = 8
WIN_COLS = 16
N_HEADS = 8
HEAD_DIM = 64
HALF = N_HEADS * HEAD_DIM
N_PAIRS = N_HEADS // 2
LANES = 128
SG_CHUNK = 128
N_EXPERTS = 8
RMS_EPS = 1e-6
MASK_VALUE = -1e30

TM = 512
ROWS_PER_BLOCK = TM // GRID_W
EXPERT_TILE = 512
VMEM_LIMIT = 56 * 1024 * 1024


def _params(sem, vmem=VMEM_LIMIT):
    return pltpu.CompilerParams(dimension_semantics=sem, vmem_limit_bytes=vmem)


def _const_spec(shape):
    nd = len(shape)
    return pl.BlockSpec(shape, lambda *_: (0,) * nd, pipeline_mode=pl.Buffered(1))


def _rms_scale(x):
    return lax.rsqrt(jnp.mean(x * x, axis=-1, keepdims=True) + RMS_EPS)


def _gelu_tanh(x):
    return x * (0.5 * (1.0 + jnp.tanh(0.7978845608028654 * (x + 0.044715 * (x * x * x)))))


def _inproj_kernel(x_ref, g_ref, w_ref, sgg_ref, bd_ref, q_ref, k_ref, v_ref, u_ref, sv_ref):
    x = x_ref[...]
    h = (x * _rms_scale(x) * g_ref[...]).astype(BF16)

    def proj(n):
        return jnp.dot(h, w_ref[:, n * HALF:(n + 1) * HALF], preferred_element_type=F32)

    q_ref[...] = (proj(0) * (HEAD_DIM ** -0.5)).astype(BF16)
    k_ref[...] = proj(1).astype(BF16)
    v_ref[...] = proj(2).astype(BF16)
    u_ref[...] = _gelu_tanh(proj(3)).astype(BF16)
    sv = _gelu_tanh(proj(4))
    ssq = jnp.dot((sv * sv).astype(BF16), bd_ref[...], preferred_element_type=F32)
    sv_ref[...] = (sv * lax.rsqrt(ssq * (1.0 / HEAD_DIM) + RMS_EPS) * sgg_ref[...]).astype(BF16)


def _inproj(x, g, w_in, sgg, bd):
    T, D = x.shape
    out = jax.ShapeDtypeStruct((T, HALF), BF16)
    tile = pl.BlockSpec((TM, HALF), lambda i: (i, 0))
    return pl.pallas_call(
        _inproj_kernel,
        grid=(T // TM,),
        in_specs=[pl.BlockSpec((TM, D), lambda i: (i, 0)),
                  _const_spec((1, D)), _const_spec(w_in.shape),
                  _const_spec((1, HALF)), _const_spec((HALF, HALF))],
        out_specs=[tile] * 5,
        out_shape=[out] * 5,
        compiler_params=_params(("parallel",)),
        name="inproj",
    )(x, g, w_in, sgg, bd)


def _natten_kernel(lo_ref, hi_ref, q_ref, kp_ref, kc_ref, kn_ref, vp_ref, vc_ref, vn_ref,
                   bias_ref, o_ref, kwin, vwin, qe_s, qo_s):
    b = pl.program_id(0)
    first = lo_ref[b]
    blk = b - first
    n_rows = (hi_ref[b] - first + 1) * ROWS_PER_BLOCK

    kwin[0:TM] = kp_ref[...]
    kwin[TM:2 * TM] = kc_ref[...]
    kwin[2 * TM:3 * TM] = kn_ref[...]
    vwin[0:TM] = vp_ref[...]
    vwin[TM:2 * TM] = vc_ref[...]
    vwin[2 * TM:3 * TM] = vn_ref[...]

    qf = q_ref[...].astype(F32)
    lane = lax.broadcasted_iota(jnp.int32, qf.shape, 1)
    is_even = (lane % LANES) < HEAD_DIM
    qe_s[...] = jnp.where(is_even, qf, 0.0).astype(BF16)
    qo_s[...] = jnp.where(is_even, 0.0, qf).astype(BF16)

    out_even = lax.broadcasted_iota(jnp.int32, (GRID_W, LANES), 1) < HEAD_DIM

    def row_body(i, carry):
        r = blk * ROWS_PER_BLOCK + i
        rs = jnp.clip(r - WIN_ROWS // 2, 0, n_rows - WIN_ROWS)
        ws = rs - (blk - 1) * ROWS_PER_BLOCK
        variant = rs - r + (WIN_ROWS - 1)
        koff = pl.multiple_of(ws * GRID_W, GRID_W)
        qoff = pl.multiple_of(i * GRID_W, GRID_W)
        for p in range(N_PAIRS):
            cols = slice(p * LANES, (p + 1) * LANES)
            q2 = jnp.concatenate([qe_s[pl.ds(qoff, GRID_W), cols],
                                  qo_s[pl.ds(qoff, GRID_W), cols]], axis=0)
            kw = kwin[pl.ds(koff, WIN_ROWS * GRID_W), cols]
            s = lax.dot_general(q2, kw, (((1,), (1,)), ((), ())),
                                preferred_element_type=F32)
            s = s + bias_ref[variant, p]
            m = jnp.max(s, axis=-1, keepdims=True)
            e = jnp.exp(s - m)
            denom = jnp.sum(e, axis=-1, keepdims=True)
            vw = vwin[pl.ds(koff, WIN_ROWS * GRID_W), cols]
            o2 = jnp.dot(e.astype(BF16), vw, preferred_element_type=F32) * (1.0 / denom)
            out = jnp.where(out_even, o2[:GRID_W], o2[GRID_W:])
            o_ref[pl.ds(qoff, GRID_W), cols] = out.astype(BF16)
        return carry

    lax.fori_loop(0, ROWS_PER_BLOCK, row_body, 0)


def _natten(q, k, v, bias, blk_lo, blk_hi):
    T = q.shape[0]
    cur = pl.BlockSpec((TM, HALF), lambda b, lo, hi: (b, 0))
    prev = pl.BlockSpec((TM, HALF), lambda b, lo, hi: (jnp.maximum(b - 1, lo[b]), 0))
    nxt = pl.BlockSpec((TM, HALF), lambda b, lo, hi: (jnp.minimum(b + 1, hi[b]), 0))
    grid_spec = pltpu.PrefetchScalarGridSpec(
        num_scalar_prefetch=2,
        grid=(T // TM,),
        in_specs=[cur, prev, cur, nxt, prev, cur, nxt,
                  pl.BlockSpec(bias.shape, lambda b, lo, hi: (0, 0, 0, 0),
                               pipeline_mode=pl.Buffered(1))],
        out_specs=cur,
        scratch_shapes=[pltpu.VMEM((3 * TM, HALF), BF16), pltpu.VMEM((3 * TM, HALF), BF16),
                        pltpu.VMEM((TM, HALF), BF16), pltpu.VMEM((TM, HALF), BF16)],
    )
    return pl.pallas_call(
        _natten_kernel,
        grid_spec=grid_spec,
        out_shape=jax.ShapeDtypeStruct((T, HALF), BF16),
        compiler_params=_params(("parallel",)),
        name="natten",
    )(blk_lo, blk_hi, q, k, k, k, v, v, v, bias)


def _na_bias_tables(rpb):
    n_layers = rpb.shape[0]
    qc = np.arange(GRID_W)
    kc = np.arange(GRID_W)
    cs = np.clip(qc - WIN_COLS // 2, 0, GRID_W - WIN_COLS)
    col_in = (kc[None, :] >= cs[:, None]) & (kc[None, :] < cs[:, None] + WIN_COLS)
    dc = np.clip(kc[None, :] - qc[:, None], -(WIN_COLS - 1), WIN_COLS - 1) + (WIN_COLS - 1)
    onehot = (dc.reshape(-1)[None, :] == np.arange(2 * WIN_COLS - 1)[:, None]).astype(np.float32)
    band = jnp.dot(rpb.astype(F32).reshape(-1, 2 * WIN_COLS - 1), onehot,
                   precision=lax.Precision.HIGHEST)
    band = band.reshape(n_layers, N_HEADS, 2 * WIN_ROWS - 1, GRID_W, GRID_W)
    band = jnp.where(col_in[None, None, None], band, MASK_VALUE)
    band = jnp.transpose(band, (0, 1, 3, 2, 4))
    t = jnp.stack([band[:, :, :, o:o + WIN_ROWS, :] for o in range(WIN_ROWS)], axis=1)
    return t.reshape(n_layers, WIN_ROWS, N_PAIRS, 2 * GRID_W, WIN_ROWS * GRID_W)


def _mixout_kernel(a_ref, u_ref, sv_ref, x_ref, sgw_ref, sgb_ref, g_ref, w_ref, o_ref, sg_s):
    even = lax.broadcasted_iota(jnp.int32, (SG_CHUNK, LANES), 1) < HEAD_DIM
    for c in range(TM // SG_CHUNK):
        rows = slice(c * SG_CHUNK, (c + 1) * SG_CHUNK)
        for p in range(N_PAIRS):
            cols = slice(p * LANES, (p + 1) * LANES)
            m2 = jnp.dot(sgw_ref[p], sv_ref[rows, cols], preferred_element_type=F32)
            mixed = jnp.where(even, m2[:SG_CHUNK], m2[SG_CHUNK:])
            sg_s[rows, cols] = u_ref[rows, cols].astype(F32) * (mixed + sgb_ref[:, cols])
    sg = sg_s[...]
    sgn = (sg * _rms_scale(sg) * g_ref[:, HALF:]).astype(BF16)
    a = a_ref[...].astype(F32)
    an = (a * _rms_scale(a) * g_ref[:, :HALF]).astype(BF16)
    y = jnp.dot(an, w_ref[:HALF, :], preferred_element_type=F32)
    y = y + jnp.dot(sgn, w_ref[HALF:, :], preferred_element_type=F32)
    o_ref[...] = x_ref[...] + y


def _mixout(attn, u, svn, x, sgw2, sgb_t, g_out, w_out):
    T, D = x.shape
    half = pl.BlockSpec((TM, HALF), lambda i: (i, 0))
    full = pl.BlockSpec((TM, D), lambda i: (i, 0))
    return pl.pallas_call(
        _mixout_kernel,
        grid=(T // TM,),
        in_specs=[half, half, half, full,
                  _const_spec(sgw2.shape), _const_spec(sgb_t.shape),
                  _const_spec((1, 2 * HALF)), _const_spec(w_out.shape)],
        out_specs=full,
        out_shape=jax.ShapeDtypeStruct((T, D), F32),
        scratch_shapes=[pltpu.VMEM((TM, HALF), F32)],
        compiler_params=_params(("parallel",)),
        name="mixout",
    )(attn, u, svn, x, sgw2, sgb_t, g_out, w_out)


def _ff_chunks(d_ff):
    n = 1
    while d_ff // n > 2048 or d_ff % n or (d_ff // n) % LANES:
        n += 1
    return n


def _swiglu(h, wg_ref, wu_ref, wd_ref, n_chunks):
    d_ff = wd_ref.shape[-2]
    fc = d_ff // n_chunks
    y = None
    for c in range(n_chunks):
        cols = slice(c * fc, (c + 1) * fc)
        gate = jnp.dot(h, wg_ref[:, cols], preferred_element_type=F32)
        up = jnp.dot(h, wu_ref[:, cols], preferred_element_type=F32)
        act = (gate / (1.0 + jnp.exp(-gate)) * up).astype(BF16)
        part = jnp.dot(act, wd_ref[cols, :], preferred_element_type=F32)
        y = part if y is None else y + part
    return y


def _ffn_kernel(x_ref, g_ref, wg_ref, wu_ref, wd_ref, o_ref, *, n_chunks):
    x = x_ref[...]
    h = (x * _rms_scale(x) * g_ref[...]).astype(BF16)
    o_ref[...] = x + _swiglu(h, wg_ref, wu_ref, wd_ref, n_chunks)


def _ffn(x, g, wg, wu, wd):
    T, D = x.shape
    full = pl.BlockSpec((TM, D), lambda i: (i, 0))
    return pl.pallas_call(
        functools.partial(_ffn_kernel, n_chunks=_ff_chunks(wg.shape[1])),
        grid=(T // TM,),
        in_specs=[full, _const_spec((1, D)), _const_spec(wg.shape), _const_spec(wu.shape),
                  _const_spec(wd.shape)],
        out_specs=full,
        out_shape=jax.ShapeDtypeStruct((T, D), F32),
        compiler_params=_params(("parallel",)),
        name="ffn",
    )(x, g, wg, wu, wd)


def _route_kernel(x_ref, g_ref, rw_ref, tri_ref, idx_ref, gate_ref, cnt_ref, carry):
    @pl.when(pl.program_id(0) == 0)
    def _():
        carry[...] = jnp.zeros_like(carry)

    x = x_ref[...]
    h = x * _rms_scale(x) * g_ref[...]
    logits = jnp.dot(h, rw_ref[...], precision=lax.Precision.HIGHEST, preferred_element_type=F32)
    col = lax.broadcasted_iota(jnp.int32, logits.shape, 1)
    colf = col.astype(F32)
    neg = -jnp.inf
    lg = jnp.where(col < N_EXPERTS, logits, neg)
    m1 = jnp.max(lg, axis=-1, keepdims=True)
    i1 = jnp.min(jnp.where(lg == m1, colf, float(LANES)), axis=-1, keepdims=True)
    sel1 = colf == i1
    lg2 = jnp.where(sel1, neg, lg)
    m2 = jnp.max(lg2, axis=-1, keepdims=True)
    i2 = jnp.min(jnp.where(lg2 == m2, colf, float(LANES)), axis=-1, keepdims=True)
    sel2 = colf == i2
    e2 = jnp.exp(m2 - m1)
    g1 = 1.0 / (1.0 + e2)
    g2 = e2 / (1.0 + e2)

    cnt = jnp.where(sel1 | sel2, 1.0, 0.0)
    before = jnp.dot(tri_ref[...], cnt.astype(BF16), preferred_element_type=F32) + carry[...]
    r1 = jnp.sum(jnp.where(sel1, before, 0.0), axis=-1, keepdims=True)
    r2 = jnp.sum(jnp.where(sel2, before, 0.0), axis=-1, keepdims=True)
    total = carry[...] + jnp.sum(cnt, axis=0, keepdims=True)
    carry[...] = total
    cnt_ref[...] = jnp.broadcast_to(total, cnt_ref.shape)

    meta = jnp.where(col == 0, i1, jnp.where(col == 1, i2, jnp.where(col == 2, r1, r2)))
    idx_ref[...] = meta.astype(jnp.int32)
    gate_ref[...] = jnp.where(col == 0, g1, g2)


def _route(x, g, rw_pad, tri):
    T, D = x.shape
    meta = pl.BlockSpec((TM, LANES), lambda i: (i, 0))
    return pl.pallas_call(
        _route_kernel,
        grid=(T // TM,),
        in_specs=[pl.BlockSpec((TM, D), lambda i: (i, 0)), _const_spec((1, D)),
                  _const_spec(rw_pad.shape), _const_spec(tri.shape)],
        out_specs=[meta, meta, pl.BlockSpec((8, LANES), lambda i: (0, 0))],
        out_shape=[jax.ShapeDtypeStruct((T, LANES), jnp.int32),
                   jax.ShapeDtypeStruct((T, LANES), F32),
                   jax.ShapeDtypeStruct((8, LANES), F32)],
        scratch_shapes=[pltpu.VMEM((1, LANES), F32)],
        compiler_params=_params(("arbitrary",)),
        name="route",
    )(x, g, rw_pad, tri)


def _dispatch_kernel(dest_ref, x_ref, g_ref, buf_in, buf_out, h_s, sem):
    del buf_in
    x = x_ref[...]
    h_s[...] = x * _rms_scale(x) * g_ref[...]

    def row_copy(t, slot):
        d = dest_ref[2 * t + slot]
        return pltpu.make_async_copy(h_s.at[pl.ds(t, 1)], buf_out.at[pl.ds(d, 1)], sem)

    def issue(t, carry):
        row_copy(t, 0).start()
        row_copy(t, 1).start()
        return carry

    lax.fori_loop(0, TM, issue, 0, unroll=8)
    for _ in range(2):
        pltpu.make_async_copy(h_s, buf_out.at[pl.ds(0, TM)], sem).wait()


def _dispatch(x, g, dest, n_rows):
    T, D = x.shape
    zeros = jnp.zeros((n_rows, D), F32)
    return pl.pallas_call(
        _dispatch_kernel,
        grid=(T // TM,),
        in_specs=[pl.BlockSpec((2 * TM,), lambda i: (i,), memory_space=pltpu.SMEM),
                  pl.BlockSpec((TM, D), lambda i: (i, 0)), _const_spec((1, D)),
                  pl.BlockSpec(memory_space=pl.ANY)],
        out_specs=pl.BlockSpec(memory_space=pl.ANY),
        out_shape=jax.ShapeDtypeStruct((n_rows, D), F32),
        scratch_shapes=[pltpu.VMEM((TM, D), F32), pltpu.SemaphoreType.DMA(())],
        input_output_aliases={3: 0},
        compiler_params=_params(("arbitrary",)),
        name="dispatch",
    )(dest, x, g, zeros)


def _experts_kernel(te_ref, nu_ref, x_ref, wg_ref, wu_ref, wd_ref, o_ref, *, n_chunks):
    i = pl.program_id(0)

    @pl.when(i < nu_ref[0])
    def _():
        o_ref[...] = _swiglu(x_ref[...].astype(BF16), wg_ref.at[0], wu_ref.at[0], wd_ref.at[0],
                             n_chunks)

    @pl.when(i >= nu_ref[0])
    def _():
        o_ref[...] = jnp.zeros_like(o_ref)


def _experts(buf, tile_expert, n_used, wg, wu, wd):
    P, D = buf.shape
    d_ff = wg.shape[-1]
    last = lambda i, te, nu: jnp.minimum(i, nu[0] - 1)
    grid_spec = pltpu.PrefetchScalarGridSpec(
        num_scalar_prefetch=2,
        grid=(P // EXPERT_TILE,),
        in_specs=[pl.BlockSpec((EXPERT_TILE, D), lambda i, te, nu: (last(i, te, nu), 0)),
                  pl.BlockSpec((1, D, d_ff), lambda i, te, nu: (te[i], 0, 0),
                               pipeline_mode=pl.Buffered(1)),
                  pl.BlockSpec((1, D, d_ff), lambda i, te, nu: (te[i], 0, 0),
                               pipeline_mode=pl.Buffered(1)),
                  pl.BlockSpec((1, d_ff, D), lambda i, te, nu: (te[i], 0, 0),
                               pipeline_mode=pl.Buffered(1))],
        out_specs=pl.BlockSpec((EXPERT_TILE, D), lambda i, te, nu: (i, 0)),
    )
    return pl.pallas_call(
        functools.partial(_experts_kernel, n_chunks=_ff_chunks(d_ff)),
        grid_spec=grid_spec,
        out_shape=jax.ShapeDtypeStruct((P, D), F32),
        compiler_params=_params(("arbitrary",)),
        name="experts",
    )(tile_expert, n_used, buf, wg, wu, wd)


def _combine_kernel(dest_ref, x_ref, gate_ref, g_ref, ys_ref, o_ref, a_s, b_s, sem, *, final_norm):
    def row_copy(t, slot, dst):
        d = dest_ref[2 * t + slot]
        return pltpu.make_async_copy(ys_ref.at[pl.ds(d, 1)], dst.at[pl.ds(t, 1)], sem)

    def issue(t, carry):
        row_copy(t, 0, a_s).start()
        row_copy(t, 1, b_s).start()
        return carry

    lax.fori_loop(0, TM, issue, 0, unroll=8)
    for dst in (a_s, b_s):
        pltpu.make_async_copy(ys_ref.at[pl.ds(0, TM)], dst, sem).wait()

    gates = gate_ref[...]
    y = x_ref[...] + (a_s[...] * gates[:, 0:1] + b_s[...] * gates[:, 1:2])
    if final_norm:
        y = y * _rms_scale(y) * g_ref[...]
    o_ref[...] = y


def _combine(x, gates, dest, ys, g_final, final_norm):
    T, D = x.shape
    full = pl.BlockSpec((TM, D), lambda i: (i, 0))
    return pl.pallas_call(
        functools.partial(_combine_kernel, final_norm=final_norm),
        grid=(T // TM,),
        in_specs=[pl.BlockSpec((2 * TM,), lambda i: (i,), memory_space=pltpu.SMEM),
                  full, pl.BlockSpec((TM, LANES), lambda i: (i, 0)), _const_spec((1, D)),
                  pl.BlockSpec(memory_space=pl.ANY)],
        out_specs=full,
        out_shape=jax.ShapeDtypeStruct((T, D), F32),
        scratch_shapes=[pltpu.VMEM((TM, D), F32), pltpu.VMEM((TM, D), F32),
                        pltpu.SemaphoreType.DMA(())],
        compiler_params=_params(("arbitrary",)),
        name="combine",
    )(dest, x, gates, g_final, ys)


def _final_norm_kernel(x_ref, g_ref, o_ref):
    x = x_ref[...]
    o_ref[...] = x * _rms_scale(x) * g_ref[...]


def _final_norm(x, g):
    T, D = x.shape
    full = pl.BlockSpec((TM, D), lambda i: (i, 0))
    return pl.pallas_call(
        _final_norm_kernel,
        grid=(T // TM,),
        in_specs=[full, _const_spec((1, D))],
        out_specs=full,
        out_shape=jax.ShapeDtypeStruct((T, D), F32),
        compiler_params=_params(("parallel",)),
        name="final_norm",
    )(x, g)


def _moe(x, g_ffn, router_w, wg, wu, wd, g_final, final_norm):
    T, D = x.shape
    rw_pad = jnp.zeros((D, LANES), F32).at[:, :N_EXPERTS].set(router_w)
    tri = jnp.asarray(np.tril(np.ones((TM, TM), np.float32), -1), BF16)
    idx, gates, cnt = _route(x, g_ffn, rw_pad, tri)

    counts = cnt[0, :N_EXPERTS].astype(jnp.int32)
    padded = (counts + EXPERT_TILE - 1) // EXPERT_TILE * EXPERT_TILE
    pad_end = jnp.cumsum(padded)
    pad_start = pad_end - padded
    dest = (pad_start[idx[:, 0:2]] + idx[:, 2:4]).reshape(2 * T)
    n_rows = 2 * T + N_EXPERTS * EXPERT_TILE
    n_tiles = n_rows // EXPERT_TILE
    tile_expert = jnp.minimum(
        jnp.searchsorted(pad_end, jnp.arange(n_tiles, dtype=jnp.int32) * EXPERT_TILE, side="right"),
        N_EXPERTS - 1).astype(jnp.int32)
    n_used = (pad_end[-1:] // EXPERT_TILE).astype(jnp.int32)

    buf = _dispatch(x, g_ffn, dest, n_rows)
    ys = _experts(buf, tile_expert, n_used, wg, wu, wd)
    return _combine(x, gates, dest, ys, g_final, final_norm)


def _image_blocks(image_rows):
    lo, hi, start = [], [], 0
    for rows in image_rows:
        n = rows // ROWS_PER_BLOCK
        lo += [start] * n
        hi += [start + n - 1] * n
        start += n
    return jnp.asarray(lo, jnp.int32), jnp.asarray(hi, jnp.int32)


def kernel(x_prompt, x_sample, norm_mix_g, w_in, na_rpb, sg_norm_g, sg_w, sg_b, out_norm_g, w_out,
           norm_ffn_g, dense_w_gate, dense_w_up, dense_w_down,
           router_w, moe_w_gate, moe_w_up, moe_w_down, final_norm_g):
    D = x_prompt.shape[-1]
    depth = w_in.shape[0]
    assert D == 2 * HALF
    image_rows = []
    for xs in (x_prompt, x_sample):
        assert xs.shape[1] % TM == 0
        image_rows += [xs.shape[1] // GRID_W] * xs.shape[0]
    blk_lo, blk_hi = _image_blocks(image_rows)
    n_prompt = x_prompt.shape[0] * x_prompt.shape[1]

    x = jnp.concatenate([x_prompt.reshape(-1, D), x_sample.reshape(-1, D)], axis=0)
    bd = jnp.asarray(np.kron(np.eye(N_HEADS, dtype=np.float32),
                             np.ones((HEAD_DIM, HEAD_DIM), np.float32)), BF16)
    g_final = final_norm_g.reshape(1, D)
    na_bias = _na_bias_tables(na_rpb)

    for l in range(depth):
        q, k, v, u, svn = _inproj(x, norm_mix_g[l].reshape(1, D), w_in[l].astype(BF16),
                                  sg_norm_g[l].reshape(1, HALF), bd)
        attn = _natten(q, k, v, na_bias[l], blk_lo, blk_hi)
        sgw2 = sg_w[l].astype(BF16).reshape(N_PAIRS, 2 * SG_CHUNK, SG_CHUNK)
        sgb_t = jnp.repeat(jnp.transpose(sg_b[l]), HEAD_DIM, axis=1)
        x = _mixout(attn, u, svn, x, sgw2, sgb_t, out_norm_g[l].reshape(1, D), w_out[l].astype(BF16))
        g_ffn = norm_ffn_g[l].reshape(1, D)
        i = l // 2
        if l % 2 == 0:
            x = _ffn(x, g_ffn, dense_w_gate[i].astype(BF16), dense_w_up[i].astype(BF16),
                     dense_w_down[i].astype(BF16))
            if l == depth - 1:
                x = _final_norm(x, g_final)
        else:
            x = _moe(x, g_ffn, router_w[i], moe_w_gate[i].astype(BF16), moe_w_up[i].astype(BF16),
                     moe_w_down[i].astype(BF16), g_final, final_norm=(l == depth - 1))
    return (x[:n_prompt].reshape(x_prompt.shape), x[n_prompt:].reshape(x_sample.shape))
```

```python
import functools

import numpy as np
import jax
import jax.numpy as jnp
from jax import lax
from jax.experimental import pallas as pl
from jax.experimental.pallas import tpu as pltpu

F32 = jnp.float32
BF16 = jnp.bfloat16

GRID_W = 64
WIN_ROWS = 8
WIN_COLS = 16
N_HEADS = 8
HEAD_DIM = 64
HALF = N_HEADS * HEAD_DIM
N_PAIRS = N_HEADS // 2
LANES = 128
SG_CHUNK = 128
N_EXPERTS = 8
RMS_EPS = 1e-6
MASK_VALUE = -1e30

TM = 512
ROWS_PER_BLOCK = TM // GRID_W
WIN_BLOCKS = 3
LOG2_E = 1.4426950408889634
EXPERT_TILE = 512
VMEM_LIMIT = 56 * 1024 * 1024


def _params(sem, vmem=VMEM_LIMIT):
    return pltpu.CompilerParams(dimension_semantics=sem, vmem_limit_bytes=vmem)


def _const_spec(shape):
    nd = len(shape)
    return pl.BlockSpec(shape, lambda *_: (0,) * nd, pipeline_mode=pl.Buffered(1))


def _rms_scale(x):
    return lax.rsqrt(jnp.mean(x * x, axis=-1, keepdims=True) + RMS_EPS)


def _gelu_tanh(x):
    return x * (0.5 * (1.0 + jnp.tanh(0.7978845608028654 * (x + 0.044715 * (x * x * x)))))


def _inproj_kernel(x_ref, g_ref, w_ref, sgg_ref, bd_ref, q_ref, k_ref, v_ref, u_ref, sv_ref):
    x = x_ref[...]
    h = (x * _rms_scale(x) * g_ref[...]).astype(BF16)

    def proj(n):
        return jnp.dot(h, w_ref[:, n * HALF:(n + 1) * HALF], preferred_element_type=F32)

    q = proj(0) * (HEAD_DIM ** -0.5 * LOG2_E)
    is_even = (lax.broadcasted_iota(jnp.int32, q.shape, 1) % LANES) < HEAD_DIM
    qe = jnp.where(is_even, q, 0.0).astype(BF16)
    qo = jnp.where(is_even, 0.0, q).astype(BF16)
    for r in range(ROWS_PER_BLOCK):
        rows = slice(r * GRID_W, (r + 1) * GRID_W)
        q_ref[2 * r * GRID_W:(2 * r + 1) * GRID_W, :] = qe[rows]
        q_ref[(2 * r + 1) * GRID_W:(2 * r + 2) * GRID_W, :] = qo[rows]
    k_ref[...] = proj(1).astype(BF16)
    v_ref[...] = proj(2).astype(BF16)
    u_ref[...] = _gelu_tanh(proj(3)).astype(BF16)
    sv = _gelu_tanh(proj(4))
    ssq = jnp.dot((sv * sv).astype(BF16), bd_ref[...], preferred_element_type=F32)
    sv_ref[...] = (sv * lax.rsqrt(ssq * (1.0 / HEAD_DIM) + RMS_EPS) * sgg_ref[...]).astype(BF16)


def _inproj(x, g, w_in, sgg, bd):
    T, D = x.shape
    out = jax.ShapeDtypeStruct((T, HALF), BF16)
    tile = pl.BlockSpec((TM, HALF), lambda i: (i, 0))
    return pl.pallas_call(
        _inproj_kernel,
        grid=(T // TM,),
        in_specs=[pl.BlockSpec((TM, D), lambda i: (i, 0)),
                  _const_spec((1, D)), _const_spec(w_in.shape),
                  _const_spec((1, HALF)), _const_spec((HALF, HALF))],
        out_specs=[pl.BlockSpec((2 * TM, HALF), lambda i: (i, 0))] + [tile] * 4,
        out_shape=[jax.ShapeDtypeStruct((2 * T, HALF), BF16)] + [out] * 4,
        compiler_params=_params(("parallel",)),
        name="inproj",
    )(x, g, w_in, sgg, bd)


def _natten_kernel(lo_ref, hi_ref, q_ref, kwin, vwin, bias_ref, o_ref):
    b = pl.program_id(0)
    first = lo_ref[b]
    blk = b - first
    n_blk = hi_ref[b] - first + 1
    n_rows = n_blk * ROWS_PER_BLOCK
    win_row0 = jnp.clip(blk - 1, 0, n_blk - WIN_BLOCKS) * ROWS_PER_BLOCK
    out_even = lax.broadcasted_iota(jnp.int32, (GRID_W, LANES), 1) < HEAD_DIM
    n_keys = WIN_ROWS * GRID_W

    def row_body(i, carry):
        r = blk * ROWS_PER_BLOCK + i
        rs = jnp.clip(r - WIN_ROWS // 2, 0, n_rows - WIN_ROWS)
        variant = rs - r + (WIN_ROWS - 1)
        koff = pl.multiple_of((rs - win_row0) * GRID_W, GRID_W)
        qoff = pl.multiple_of(i * 2 * GRID_W, 2 * GRID_W)
        scores = []
        for p in range(N_PAIRS):
            cols = slice(p * LANES, (p + 1) * LANES)
            s = lax.dot_general(q_ref[pl.ds(qoff, 2 * GRID_W), cols], kwin[pl.ds(koff, n_keys), cols],
                                (((1,), (1,)), ((), ())), preferred_element_type=F32)
            scores.append(s + bias_ref[variant, p])
        for p in range(N_PAIRS):
            cols = slice(p * LANES, (p + 1) * LANES)
            s = scores[p]
            e = jnp.exp2(s - jnp.max(s, axis=-1, keepdims=True))
            denom = jnp.sum(e, axis=-1, keepdims=True)
            o2 = jnp.dot(e.astype(BF16), vwin[pl.ds(koff, n_keys), cols],
                         preferred_element_type=F32) * (1.0 / denom)
            out = jnp.where(out_even, o2[:GRID_W], o2[GRID_W:])
            o_ref[pl.ds(pl.multiple_of(i * GRID_W, GRID_W), GRID_W), cols] = out.astype(BF16)
        return carry

    lax.fori_loop(0, ROWS_PER_BLOCK, row_body, 0)


def _natten(q2, k, v, bias, blk_lo, blk_hi):
    T = k.shape[0]

    def win_map(b, lo, hi):
        return (jnp.clip(b - 1, lo[b], hi[b] - (WIN_BLOCKS - 1)) * TM, 0)

    window = pl.BlockSpec((pl.Element(WIN_BLOCKS * TM), pl.Element(HALF)), win_map)
    grid_spec = pltpu.PrefetchScalarGridSpec(
        num_scalar_prefetch=2,
        grid=(T // TM,),
        in_specs=[pl.BlockSpec((2 * TM, HALF), lambda b, lo, hi: (b, 0)), window, window,
                  pl.BlockSpec(bias.shape, lambda b, lo, hi: (0, 0, 0, 0),
                               pipeline_mode=pl.Buffered(1))],
        out_specs=pl.BlockSpec((TM, HALF), lambda b, lo, hi: (b, 0)),
    )
    return pl.pallas_call(
        _natten_kernel,
        grid_spec=grid_spec,
        out_shape=jax.ShapeDtypeStruct((T, HALF), BF16),
        compiler_params=_params(("parallel",)),
        name="natten",
    )(blk_lo, blk_hi, q2, k, v, bias)


def _na_bias_tables(rpb):
    n_layers = rpb.shape[0]
    qc = np.arange(GRID_W)
    kc = np.arange(GRID_W)
    cs = np.clip(qc - WIN_COLS // 2, 0, GRID_W - WIN_COLS)
    col_in = (kc[None, :] >= cs[:, None]) & (kc[None, :] < cs[:, None] + WIN_COLS)
    dc = np.clip(kc[None, :] - qc[:, None], -(WIN_COLS - 1), WIN_COLS - 1) + (WIN_COLS - 1)
    onehot = (dc.reshape(-1)[None, :] == np.arange(2 * WIN_COLS - 1)[:, None]).astype(np.float32)
    band = jnp.dot(rpb.astype(F32).reshape(-1, 2 * WIN_COLS - 1), onehot,
                   precision=lax.Precision.HIGHEST)
    band = band.reshape(n_layers, N_HEADS, 2 * WIN_ROWS - 1, GRID_W, GRID_W)
    band = jnp.where(col_in[None, None, None], band * LOG2_E, MASK_VALUE)
    band = jnp.transpose(band, (0, 1, 3, 2, 4))
    t = jnp.stack([band[:, :, :, o:o + WIN_ROWS, :] for o in range(WIN_ROWS)], axis=1)
    return t.reshape(n_layers, WIN_ROWS, N_PAIRS, 2 * GRID_W, WIN_ROWS * GRID_W)


def _mixout_kernel(a_ref, u_ref, sv_ref, x_ref, sgw_ref, sgb_ref, g_ref, w_ref, o_ref, sg_s):
    even = lax.broadcasted_iota(jnp.int32, (SG_CHUNK, LANES), 1) < HEAD_DIM
    for c in range(TM // SG_CHUNK):
        rows = slice(c * SG_CHUNK, (c + 1) * SG_CHUNK)
        for p in range(N_PAIRS):
            cols = slice(p * LANES, (p + 1) * LANES)
            m2 = jnp.dot(sgw_ref[p], sv_ref[rows, cols], preferred_element_type=F32)
            mixed = jnp.where(even, m2[:SG_CHUNK], m2[SG_CHUNK:])
            sg_s[rows, cols] = u_ref[rows, cols].astype(F32) * (mixed + sgb_ref[:, cols])
    sg = sg_s[...]
    sgn = (sg * _rms_scale(sg) * g_ref[:, HALF:]).astype(BF16)
    a = a_ref[...].astype(F32)
    an = (a * _rms_scale(a) * g_ref[:, :HALF]).astype(BF16)
    y = jnp.dot(an, w_ref[:HALF, :], preferred_element_type=F32)
    y = y + jnp.dot(sgn, w_ref[HALF:, :], preferred_element_type=F32)
    o_ref[...] = x_ref[...] + y


def _mixout(attn, u, svn, x, sgw2, sgb_t, g_out, w_out):
    T, D = x.shape
    half = pl.BlockSpec((TM, HALF), lambda i: (i, 0))
    full = pl.BlockSpec((TM, D), lambda i: (i, 0))
    return pl.pallas_call(
        _mixout_kernel,
        grid=(T // TM,),
        in_specs=[half, half, half, full,
                  _const_spec(sgw2.shape), _const_spec(sgb_t.shape),
                  _const_spec((1, 2 * HALF)), _const_spec(w_out.shape)],
        out_specs=full,
        out_shape=jax.ShapeDtypeStruct((T, D), F32),
        scratch_shapes=[pltpu.VMEM((TM, HALF), F32)],
        compiler_params=_params(("parallel",)),
        name="mixout",
    )(attn, u, svn, x, sgw2, sgb_t, g_out, w_out)


def _ff_chunks(d_ff):
    n = 1
    while d_ff // n > 2048 or d_ff % n or (d_ff // n) % LANES:
        n += 1
    return n


def _swiglu(h, wg_ref, wu_ref, wd_ref, n_chunks):
    d_ff = wd_ref.shape[-2]
    fc = d_ff // n_chunks
    y = None
    for c in range(n_chunks):
        cols = slice(c * fc, (c + 1) * fc)
        gate = jnp.dot(h, wg_ref[:, cols], preferred_element_type=F32)
        up = jnp.dot(h, wu_ref[:, cols], preferred_element_type=F32)
        act = (gate / (1.0 + jnp.exp(-gate)) * up).astype(BF16)
        part = jnp.dot(act, wd_ref[cols, :], preferred_element_type=F32)
        y = part if y is None else y + part
    return y


def _ffn_kernel(x_ref, g_ref, wg_ref, wu_ref, wd_ref, o_ref, *, n_chunks):
    x = x_ref[...]
    h = (x * _rms_scale(x) * g_ref[...]).astype(BF16)
    o_ref[...] = x + _swiglu(h, wg_ref, wu_ref, wd_ref, n_chunks)


def _ffn(x, g, wg, wu, wd):
    T, D = x.shape
    full = pl.BlockSpec((TM, D), lambda i: (i, 0))
    return pl.pallas_call(
        functools.partial(_ffn_kernel, n_chunks=_ff_chunks(wg.shape[1])),
        grid=(T // TM,),
        in_specs=[full, _const_spec((1, D)), _const_spec(wg.shape), _const_spec(wu.shape),
                  _const_spec(wd.shape)],
        out_specs=full,
        out_shape=jax.ShapeDtypeStruct((T, D), F32),
        compiler_params=_params(("parallel",)),
        name="ffn",
    )(x, g, wg, wu, wd)


def _route_kernel(x_ref, g_ref, rw_ref, tri_ref, idx_ref, gate_ref, cnt_ref, carry):
    @pl.when(pl.program_id(0) == 0)
    def _():
        carry[...] = jnp.zeros_like(carry)

    x = x_ref[...]
    h = x * _rms_scale(x) * g_ref[...]
    logits = jnp.dot(h, rw_ref[...], precision=lax.Precision.HIGHEST, preferred_element_type=F32)
    col = lax.broadcasted_iota(jnp.int32, logits.shape, 1)
    colf = col.astype(F32)
    neg = -jnp.inf
    lg = jnp.where(col < N_EXPERTS, logits, neg)
    m1 = jnp.max(lg, axis=-1, keepdims=True)
    i1 = jnp.min(jnp.where(lg == m1, colf, float(LANES)), axis=-1, keepdims=True)
    sel1 = colf == i1
    lg2 = jnp.where(sel1, neg, lg)
    m2 = jnp.max(lg2, axis=-1, keepdims=True)
    i2 = jnp.min(jnp.where(lg2 == m2, colf, float(LANES)), axis=-1, keepdims=True)
    sel2 = colf == i2
    e2 = jnp.exp(m2 - m1)
    g1 = 1.0 / (1.0 + e2)
    g2 = e2 / (1.0 + e2)

    cnt = jnp.where(sel1 | sel2, 1.0, 0.0)
    before = jnp.dot(tri_ref[...], cnt.astype(BF16), preferred_element_type=F32) + carry[...]
    r1 = jnp.sum(jnp.where(sel1, before, 0.0), axis=-1, keepdims=True)
    r2 = jnp.sum(jnp.where(sel2, before, 0.0), axis=-1, keepdims=True)
    total = carry[...] + jnp.sum(cnt, axis=0, keepdims=True)
    carry[...] = total
    cnt_ref[...] = jnp.broadcast_to(total, cnt_ref.shape)

    meta = jnp.where(col == 0, i1, jnp.where(col == 1, i2, jnp.where(col == 2, r1, r2)))
    idx_ref[...] = meta.astype(jnp.int32)
    gate_ref[...] = jnp.where(col == 0, g1, g2)


def _route(x, g, rw_pad, tri):
    T, D = x.shape
    meta = pl.BlockSpec((TM, LANES), lambda i: (i, 0))
    return pl.pallas_call(
        _route_kernel,
        grid=(T // TM,),
        in_specs=[pl.BlockSpec((TM, D), lambda i: (i, 0)), _const_spec((1, D)),
                  _const_spec(rw_pad.shape), _const_spec(tri.shape)],
        out_specs=[meta, meta, pl.BlockSpec((8, LANES), lambda i: (0, 0))],
        out_shape=[jax.ShapeDtypeStruct((T, LANES), jnp.int32),
                   jax.ShapeDtypeStruct((T, LANES), F32),
                   jax.ShapeDtypeStruct((8, LANES), F32)],
        scratch_shapes=[pltpu.VMEM((1, LANES), F32)],
        compiler_params=_params(("arbitrary",)),
        name="route",
    )(x, g, rw_pad, tri)


def _dispatch_kernel(dest_ref, x_ref, g_ref, buf_in, buf_out, h_s, sem):
    del buf_in
    x = x_ref[...]
    h_s[...] = x * _rms_scale(x) * g_ref[...]

    def row_copy(t, slot):
        d = dest_ref[2 * t + slot]
        return pltpu.make_async_copy(h_s.at[pl.ds(t, 1)], buf_out.at[pl.ds(d, 1)], sem)

    def issue(t, carry):
        row_copy(t, 0).start()
        row_copy(t, 1).start()
        return carry

    lax.fori_loop(0, TM, issue, 0, unroll=8)
    for _ in range(2):
        pltpu.make_async_copy(h_s, buf_out.at[pl.ds(0, TM)], sem).wait()


def _dispatch(x, g, dest, n_rows):
    T, D = x.shape
    zeros = jnp.zeros((n_rows, D), F32)
    return pl.pallas_call(
        _dispatch_kernel,
        grid=(T // TM,),
        in_specs=[pl.BlockSpec((2 * TM,), lambda i: (i,), memory_space=pltpu.SMEM),
                  pl.BlockSpec((TM, D), lambda i: (i, 0)), _const_spec((1, D)),
                  pl.BlockSpec(memory_space=pl.ANY)],
        out_specs=pl.BlockSpec(memory_space=pl.ANY),
        out_shape=jax.ShapeDtypeStruct((n_rows, D), F32),
        scratch_shapes=[pltpu.VMEM((TM, D), F32), pltpu.SemaphoreType.DMA(())],
        input_output_aliases={3: 0},
        compiler_params=_params(("arbitrary",)),
        name="dispatch",
    )(dest, x, g, zeros)


def _experts_kernel(te_ref, nu_ref, x_ref, wg_ref, wu_ref, wd_ref, o_ref, *, n_chunks):
    i = pl.program_id(0)

    @pl.when(i < nu_ref[0])
    def _():
        o_ref[...] = _swiglu(x_ref[...].astype(BF16), wg_ref.at[0], wu_ref.at[0], wd_ref.at[0],
                             n_chunks)

    @pl.when(i >= nu_ref[0])
    def _():
        o_ref[...] = jnp.zeros_like(o_ref)


def _experts(buf, tile_expert, n_used, wg, wu, wd):
    P, D = buf.shape
    d_ff = wg.shape[-1]
    last = lambda i, te, nu: jnp.maximum(jnp.minimum(i, nu[0] - 1), 0)
    grid_spec = pltpu.PrefetchScalarGridSpec(
        num_scalar_prefetch=2,
        grid=(P // EXPERT_TILE,),
        in_specs=[pl.BlockSpec((EXPERT_TILE, D), lambda i, te, nu: (last(i, te, nu), 0)),
                  pl.BlockSpec((1, D, d_ff), lambda i, te, nu: (te[i], 0, 0),
                               pipeline_mode=pl.Buffered(1)),
                  pl.BlockSpec((1, D, d_ff), lambda i, te, nu: (te[i], 0, 0),
                               pipeline_mode=pl.Buffered(1)),
                  pl.BlockSpec((1, d_ff, D), lambda i, te, nu: (te[i], 0, 0),
                               pipeline_mode=pl.Buffered(1))],
        out_specs=pl.BlockSpec((EXPERT_TILE, D), lambda i, te, nu: (i, 0)),
    )
    return pl.pallas_call(
        functools.partial(_experts_kernel, n_chunks=_ff_chunks(d_ff)),
        grid_spec=grid_spec,
        out_shape=jax.ShapeDtypeStruct((P, D), F32),
        compiler_params=_params(("arbitrary",)),
        name="experts",
    )(tile_expert, n_used, buf, wg, wu, wd)


def _combine_kernel(dest_ref, x_ref, gate_ref, g_ref, ys_ref, o_ref, a_s, b_s, sem, *, final_norm):
    def row_copy(t, slot, dst):
        d = dest_ref[2 * t + slot]
        return pltpu.make_async_copy(ys_ref.at[pl.ds(d, 1)], dst.at[pl.ds(t, 1)], sem)

    def issue(t, carry):
        row_copy(t, 0, a_s).start()
        row_copy(t, 1, b_s).start()
        return carry

    lax.fori_loop(0, TM, issue, 0, unroll=8)
    for dst in (a_s, b_s):
        pltpu.make_async_copy(ys_ref.at[pl.ds(0, TM)], dst, sem).wait()

    gates = gate_ref[...]
    y = x_ref[...] + (a_s[...] * gates[:, 0:1] + b_s[...] * gates[:, 1:2])
    if final_norm:
        y = y * _rms_scale(y) * g_ref[...]
    o_ref[...] = y


def _combine(x, gates, dest, ys, g_final, final_norm):
    T, D = x.shape
    full = pl.BlockSpec((TM, D), lambda i: (i, 0))
    return pl.pallas_call(
        functools.partial(_combine_kernel, final_norm=final_norm),
        grid=(T // TM,),
        in_specs=[pl.BlockSpec((2 * TM,), lambda i: (i,), memory_space=pltpu.SMEM),
                  full, pl.BlockSpec((TM, LANES), lambda i: (i, 0)), _const_spec((1, D)),
                  pl.BlockSpec(memory_space=pl.ANY)],
        out_specs=full,
        out_shape=jax.ShapeDtypeStruct((T, D), F32),
        scratch_shapes=[pltpu.VMEM((TM, D), F32), pltpu.VMEM((TM, D), F32),
                        pltpu.SemaphoreType.DMA(())],
        compiler_params=_params(("arbitrary",)),
        name="combine",
    )(dest, x, gates, g_final, ys)


def _final_norm_kernel(x_ref, g_ref, o_ref):
    x = x_ref[...]
    o_ref[...] = x * _rms_scale(x) * g_ref[...]


def _final_norm(x, g):
    T, D = x.shape
    full = pl.BlockSpec((TM, D), lambda i: (i, 0))
    return pl.pallas_call(
        _final_norm_kernel,
        grid=(T // TM,),
        in_specs=[full, _const_spec((1, D))],
        out_specs=full,
        out_shape=jax.ShapeDtypeStruct((T, D), F32),
        compiler_params=_params(("parallel",)),
        name="final_norm",
    )(x, g)


def _moe(x, g_ffn, router_w, wg, wu, wd, g_final, final_norm):
    T, D = x.shape
    rw_pad = jnp.zeros((D, LANES), F32).at[:, :N_EXPERTS].set(router_w)
    tri = jnp.asarray(np.tril(np.ones((TM, TM), np.float32), -1), BF16)
    idx, gates, cnt = _route(x, g_ffn, rw_pad, tri)

    counts = cnt[0, :N_EXPERTS].astype(jnp.int32)
    padded = (counts + EXPERT_TILE - 1) // EXPERT_TILE * EXPERT_TILE
    pad_end = jnp.cumsum(padded)
    pad_start = pad_end - padded
    dest = (pad_start[idx[:, 0:2]] + idx[:, 2:4]).reshape(2 * T)
    n_rows = 2 * T + N_EXPERTS * EXPERT_TILE
    n_tiles = n_rows // EXPERT_TILE
    tile_expert = jnp.minimum(
        jnp.searchsorted(pad_end, jnp.arange(n_tiles, dtype=jnp.int32) * EXPERT_TILE, side="right"),
        N_EXPERTS - 1).astype(jnp.int32)
    n_used = (pad_end[-1:] // EXPERT_TILE).astype(jnp.int32)

    buf = _dispatch(x, g_ffn, dest, n_rows)
    ys = _experts(buf, tile_expert, n_used, wg, wu, wd)
    return _combine(x, gates, dest, ys, g_final, final_norm)


def _image_blocks(image_rows):
    lo, hi, start = [], [], 0
    for rows in image_rows:
        n = rows // ROWS_PER_BLOCK
        lo += [start] * n
        hi += [start + n - 1] * n
        start += n
    return jnp.asarray(lo, jnp.int32), jnp.asarray(hi, jnp.int32)


def kernel(x_prompt, x_sample, norm_mix_g, w_in, na_rpb, sg_norm_g, sg_w, sg_b, out_norm_g, w_out,
           norm_ffn_g, dense_w_gate, dense_w_up, dense_w_down,
           router_w, moe_w_gate, moe_w_up, moe_w_down, final_norm_g):
    D = x_prompt.shape[-1]
    depth = w_in.shape[0]
    assert D == 2 * HALF
    image_rows = []
    for xs in (x_prompt, x_sample):
        assert xs.shape[1] % TM == 0 and xs.shape[1] >= WIN_BLOCKS * TM
        image_rows += [xs.shape[1] // GRID_W] * xs.shape[0]
    blk_lo, blk_hi = _image_blocks(image_rows)
    n_prompt = x_prompt.shape[0] * x_prompt.shape[1]

    x = jnp.concatenate([x_prompt.reshape(-1, D), x_sample.reshape(-1, D)], axis=0)
    bd = jnp.asarray(np.kron(np.eye(N_HEADS, dtype=np.float32),
                             np.ones((HEAD_DIM, HEAD_DIM), np.float32)), BF16)
    g_final = final_norm_g.reshape(1, D)
    na_bias = _na_bias_tables(na_rpb)

    for l in range(depth):
        q, k, v, u, svn = _inproj(x, norm_mix_g[l].reshape(1, D), w_in[l].astype(BF16),
                                  sg_norm_g[l].reshape(1, HALF), bd)
        attn = _natten(q, k, v, na_bias[l], blk_lo, blk_hi)
        sgw2 = sg_w[l].astype(BF16).reshape(N_PAIRS, 2 * SG_CHUNK, SG_CHUNK)
        sgb_t = jnp.repeat(jnp.transpose(sg_b[l]), HEAD_DIM, axis=1)
        x = _mixout(attn, u, svn, x, sgw2, sgb_t, out_norm_g[l].reshape(1, D), w_out[l].astype(BF16))
        g_ffn = norm_ffn_g[l].reshape(1, D)
        i = l // 2
        if l % 2 == 0:
            x = _ffn(x, g_ffn, dense_w_gate[i].astype(BF16), dense_w_up[i].astype(BF16),
                     dense_w_down[i].astype(BF16))
            if l == depth - 1:
                x = _final_norm(x, g_final)
        else:
            x = _moe(x, g_ffn, router_w[i], moe_w_gate[i].astype(BF16), moe_w_up[i].astype(BF16),
                     moe_w_down[i].astype(BF16), g_final, final_norm=(l == depth - 1))
    return (x[:n_prompt].reshape(x_prompt.shape), x[n_prompt:].reshape(x_sample.shape))
```

```python
import functools

import numpy as np
import jax
import jax.numpy as jnp
from jax import lax
from jax.experimental import pallas as pl
from jax.experimental.pallas import tpu as pltpu

F32 = jnp.float32
BF16 = jnp.bfloat16

GRID_W = 64
WIN_ROWS = 8
WIN_COLS = 16
N_HEADS = 8
HEAD_DIM = 64
HALF = N_HEADS * HEAD_DIM
N_PAIRS = N_HEADS // 2
LANES = 128
SG_CHUNK = 128
N_EXPERTS = 8
RMS_EPS = 1e-6
MASK_VALUE = -1e30

TM = 512
ROWS_PER_BLOCK = TM // GRID_W
WIN_BLOCKS = 3
LOG2_E = 1.4426950408889634
EXPERT_TILE = 512
VMEM_LIMIT = 56 * 1024 * 1024


def _params(sem, vmem=VMEM_LIMIT):
    return pltpu.CompilerParams(dimension_semantics=sem, vmem_limit_bytes=vmem)


def _const_spec(shape):
    nd = len(shape)
    return pl.BlockSpec(shape, lambda *_: (0,) * nd, pipeline_mode=pl.Buffered(1))


def _rms_scale(x):
    return lax.rsqrt(jnp.mean(x * x, axis=-1, keepdims=True) + RMS_EPS)


def _gelu_tanh(x):
    return x * (0.5 * (1.0 + jnp.tanh(0.7978845608028654 * (x + 0.044715 * (x * x * x)))))


def _inproj_kernel(x_ref, g_ref, w_ref, sgg_ref, bd_ref, q_ref, k_ref, v_ref, u_ref, sv_ref):
    x = x_ref[...]
    h = (x * _rms_scale(x) * g_ref[...]).astype(BF16)

    def proj(n):
        return jnp.dot(h, w_ref[:, n * HALF:(n + 1) * HALF], preferred_element_type=F32)

    q = proj(0) * (HEAD_DIM ** -0.5 * LOG2_E)
    is_even = (lax.broadcasted_iota(jnp.int32, q.shape, 1) % LANES) < HEAD_DIM
    qe = jnp.where(is_even, q, 0.0).astype(BF16)
    qo = jnp.where(is_even, 0.0, q).astype(BF16)
    for r in range(ROWS_PER_BLOCK):
        rows = slice(r * GRID_W, (r + 1) * GRID_W)
        q_ref[2 * r * GRID_W:(2 * r + 1) * GRID_W, :] = qe[rows]
        q_ref[(2 * r + 1) * GRID_W:(2 * r + 2) * GRID_W, :] = qo[rows]
    k_ref[...] = proj(1).astype(BF16)
    v_ref[...] = proj(2).astype(BF16)
    u_ref[...] = _gelu_tanh(proj(3)).astype(BF16)
    sv = _gelu_tanh(proj(4))
    ssq = jnp.dot((sv * sv).astype(BF16), bd_ref[...], preferred_element_type=F32)
    sv_ref[...] = (sv * lax.rsqrt(ssq * (1.0 / HEAD_DIM) + RMS_EPS) * sgg_ref[...]).astype(BF16)


def _inproj(x, g, w_in, sgg, bd):
    T, D = x.shape
    out = jax.ShapeDtypeStruct((T, HALF), BF16)
    tile = pl.BlockSpec((TM, HALF), lambda i: (i, 0))
    return pl.pallas_call(
        _inproj_kernel,
        grid=(T // TM,),
        in_specs=[pl.BlockSpec((TM, D), lambda i: (i, 0)),
                  _const_spec((1, D)), _const_spec(w_in.shape),
                  _const_spec((1, HALF)), _const_spec((HALF, HALF))],
        out_specs=[pl.BlockSpec((2 * TM, HALF), lambda i: (i, 0))] + [tile] * 4,
        out_shape=[jax.ShapeDtypeStruct((2 * T, HALF), BF16)] + [out] * 4,
        compiler_params=_params(("parallel",)),
        name="inproj",
    )(x, g, w_in, sgg, bd)


def _natten_kernel(lo_ref, hi_ref, q_ref, kwin, vwin, bias_ref, o_ref):
    b = pl.program_id(0)
    first = lo_ref[b]
    blk = b - first
    n_blk = hi_ref[b] - first + 1
    n_rows = n_blk * ROWS_PER_BLOCK
    win_row0 = jnp.clip(blk - 1, 0, n_blk - WIN_BLOCKS) * ROWS_PER_BLOCK
    out_even = lax.broadcasted_iota(jnp.int32, (GRID_W, LANES), 1) < HEAD_DIM
    n_keys = WIN_ROWS * GRID_W

    def row_body(i, carry):
        r = blk * ROWS_PER_BLOCK + i
        rs = jnp.clip(r - WIN_ROWS // 2, 0, n_rows - WIN_ROWS)
        variant = rs - r + (WIN_ROWS - 1)
        koff = pl.multiple_of((rs - win_row0) * GRID_W, GRID_W)
        qoff = pl.multiple_of(i * 2 * GRID_W, 2 * GRID_W)
        scores = []
        for p in range(N_PAIRS):
            cols = slice(p * LANES, (p + 1) * LANES)
            s = lax.dot_general(q_ref[pl.ds(qoff, 2 * GRID_W), cols], kwin[pl.ds(koff, n_keys), cols],
                                (((1,), (1,)), ((), ())), preferred_element_type=F32)
            scores.append(s + bias_ref[variant, p])
        for p in range(N_PAIRS):
            cols = slice(p * LANES, (p + 1) * LANES)
            s = scores[p]
            e = jnp.exp2(s - jnp.max(s, axis=-1, keepdims=True))
            denom = jnp.sum(e, axis=-1, keepdims=True)
            o2 = jnp.dot(e.astype(BF16), vwin[pl.ds(koff, n_keys), cols],
                         preferred_element_type=F32) * (1.0 / denom)
            out = jnp.where(out_even, o2[:GRID_W], o2[GRID_W:])
            o_ref[pl.ds(pl.multiple_of(i * GRID_W, GRID_W), GRID_W), cols] = out.astype(BF16)
        return carry

    lax.fori_loop(0, ROWS_PER_BLOCK, row_body, 0)


def _natten(q2, k, v, bias, blk_lo, blk_hi):
    T = k.shape[0]

    def win_map(b, lo, hi):
        return (jnp.clip(b - 1, lo[b], hi[b] - (WIN_BLOCKS - 1)) * TM, 0)

    window = pl.BlockSpec((pl.Element(WIN_BLOCKS * TM), pl.Element(HALF)), win_map)
    grid_spec = pltpu.PrefetchScalarGridSpec(
        num_scalar_prefetch=2,
        grid=(T // TM,),
        in_specs=[pl.BlockSpec((2 * TM, HALF), lambda b, lo, hi: (b, 0)), window, window,
                  pl.BlockSpec(bias.shape, lambda b, lo, hi: (0, 0, 0, 0),
                               pipeline_mode=pl.Buffered(1))],
        out_specs=pl.BlockSpec((TM, HALF), lambda b, lo, hi: (b, 0)),
    )
    return pl.pallas_call(
        _natten_kernel,
        grid_spec=grid_spec,
        out_shape=jax.ShapeDtypeStruct((T, HALF), BF16),
        compiler_params=_params(("parallel",)),
        name="natten",
    )(blk_lo, blk_hi, q2, k, v, bias)


def _na_bias_tables(rpb):
    n_layers = rpb.shape[0]
    qc = np.arange(GRID_W)
    kc = np.arange(GRID_W)
    cs = np.clip(qc - WIN_COLS // 2, 0, GRID_W - WIN_COLS)
    col_in = (kc[None, :] >= cs[:, None]) & (kc[None, :] < cs[:, None] + WIN_COLS)
    dc = np.clip(kc[None, :] - qc[:, None], -(WIN_COLS - 1), WIN_COLS - 1) + (WIN_COLS - 1)
    onehot = (dc.reshape(-1)[None, :] == np.arange(2 * WIN_COLS - 1)[:, None]).astype(np.float32)
    band = jnp.dot(rpb.astype(F32).reshape(-1, 2 * WIN_COLS - 1), onehot,
                   precision=lax.Precision.HIGHEST)
    band = band.reshape(n_layers, N_HEADS, 2 * WIN_ROWS - 1, GRID_W, GRID_W)
    band = jnp.where(col_in[None, None, None], band * LOG2_E, MASK_VALUE)
    band = jnp.transpose(band, (0, 1, 3, 2, 4))
    t = jnp.stack([band[:, :, :, o:o + WIN_ROWS, :] for o in range(WIN_ROWS)], axis=1)
    return t.reshape(n_layers, WIN_ROWS, N_PAIRS, 2 * GRID_W, WIN_ROWS * GRID_W)


def _mixout_kernel(a_ref, u_ref, sv_ref, x_ref, sgw_ref, sgb_ref, g_ref, w_ref, o_ref, sg_s):
    even = lax.broadcasted_iota(jnp.int32, (SG_CHUNK, LANES), 1) < HEAD_DIM
    for c in range(TM // SG_CHUNK):
        rows = slice(c * SG_CHUNK, (c + 1) * SG_CHUNK)
        for p in range(N_PAIRS):
            cols = slice(p * LANES, (p + 1) * LANES)
            m2 = jnp.dot(sgw_ref[p], sv_ref[rows, cols], preferred_element_type=F32)
            mixed = jnp.where(even, m2[:SG_CHUNK], m2[SG_CHUNK:])
            sg_s[rows, cols] = u_ref[rows, cols].astype(F32) * (mixed + sgb_ref[:, cols])
    sg = sg_s[...]
    sgn = (sg * _rms_scale(sg) * g_ref[:, HALF:]).astype(BF16)
    a = a_ref[...].astype(F32)
    an = (a * _rms_scale(a) * g_ref[:, :HALF]).astype(BF16)
    y = jnp.dot(an, w_ref[:HALF, :], preferred_element_type=F32)
    y = y + jnp.dot(sgn, w_ref[HALF:, :], preferred_element_type=F32)
    o_ref[...] = x_ref[...] + y


def _mixout(attn, u, svn, x, sgw2, sgb_t, g_out, w_out):
    T, D = x.shape
    half = pl.BlockSpec((TM, HALF), lambda i: (i, 0))
    full = pl.BlockSpec((TM, D), lambda i: (i, 0))
    return pl.pallas_call(
        _mixout_kernel,
        grid=(T // TM,),
        in_specs=[half, half, half, full,
                  _const_spec(sgw2.shape), _const_spec(sgb_t.shape),
                  _const_spec((1, 2 * HALF)), _const_spec(w_out.shape)],
        out_specs=full,
        out_shape=jax.ShapeDtypeStruct((T, D), F32),
        scratch_shapes=[pltpu.VMEM((TM, HALF), F32)],
        compiler_params=_params(("parallel",)),
        name="mixout",
    )(attn, u, svn, x, sgw2, sgb_t, g_out, w_out)


def _ff_chunks(d_ff):
    n = 1
    while d_ff // n > 2048 or d_ff % n or (d_ff // n) % LANES:
        n += 1
    return n


def _swiglu(h, wg_ref, wu_ref, wd_ref, n_chunks):
    d_ff = wd_ref.shape[-2]
    fc = d_ff // n_chunks
    y = None
    for c in range(n_chunks):
        cols = slice(c * fc, (c + 1) * fc)
        gate = jnp.dot(h, wg_ref[:, cols], preferred_element_type=F32)
        up = jnp.dot(h, wu_ref[:, cols], preferred_element_type=F32)
        act = (gate / (1.0 + jnp.exp(-gate)) * up).astype(BF16)
        part = jnp.dot(act, wd_ref[cols, :], preferred_element_type=F32)
        y = part if y is None else y + part
    return y


def _ffn_kernel(x_ref, g_ref, wg_ref, wu_ref, wd_ref, o_ref, *, n_chunks):
    x = x_ref[...]
    h = (x * _rms_scale(x) * g_ref[...]).astype(BF16)
    o_ref[...] = x + _swiglu(h, wg_ref, wu_ref, wd_ref, n_chunks)


def _ffn(x, g, wg, wu, wd):
    T, D = x.shape
    full = pl.BlockSpec((TM, D), lambda i: (i, 0))
    return pl.pallas_call(
        functools.partial(_ffn_kernel, n_chunks=_ff_chunks(wg.shape[1])),
        grid=(T // TM,),
        in_specs=[full, _const_spec((1, D)), _const_spec(wg.shape), _const_spec(wu.shape),
                  _const_spec(wd.shape)],
        out_specs=full,
        out_shape=jax.ShapeDtypeStruct((T, D), F32),
        compiler_params=_params(("parallel",)),
        name="ffn",
    )(x, g, wg, wu, wd)


def _route_kernel(x_ref, g_ref, rw_ref, tri_ref, idx_ref, gate_ref, cnt_ref, carry):
    @pl.when(pl.program_id(0) == 0)
    def _():
        carry[...] = jnp.zeros_like(carry)

    x = x_ref[...]
    h = x * _rms_scale(x) * g_ref[...]
    logits = jnp.dot(h, rw_ref[...], precision=lax.Precision.HIGHEST, preferred_element_type=F32)
    col = lax.broadcasted_iota(jnp.int32, logits.shape, 1)
    colf = col.astype(F32)
    neg = -jnp.inf
    lg = jnp.where(col < N_EXPERTS, logits, neg)
    m1 = jnp.max(lg, axis=-1, keepdims=True)
    i1 = jnp.min(jnp.where(lg == m1, colf, float(LANES)), axis=-1, keepdims=True)
    sel1 = colf == i1
    lg2 = jnp.where(sel1, neg, lg)
    m2 = jnp.max(lg2, axis=-1, keepdims=True)
    i2 = jnp.min(jnp.where(lg2 == m2, colf, float(LANES)), axis=-1, keepdims=True)
    sel2 = colf == i2
    e2 = jnp.exp(m2 - m1)
    g1 = 1.0 / (1.0 + e2)
    g2 = e2 / (1.0 + e2)

    cnt = jnp.where(sel1 | sel2, 1.0, 0.0)
    before = jnp.dot(tri_ref[...], cnt.astype(BF16), preferred_element_type=F32) + carry[...]
    r1 = jnp.sum(jnp.where(sel1, before, 0.0), axis=-1, keepdims=True)
    r2 = jnp.sum(jnp.where(sel2, before, 0.0), axis=-1, keepdims=True)
    total = carry[...] + jnp.sum(cnt, axis=0, keepdims=True)
    carry[...] = total
    cnt_ref[...] = jnp.broadcast_to(total, cnt_ref.shape)

    meta = jnp.where(col == 0, i1, jnp.where(col == 1, i2, jnp.where(col == 2, r1, r2)))
    idx_ref[...] = meta.astype(jnp.int32)
    gate_ref[...] = jnp.where(col == 0, g1, g2)


def _route(x, g, rw_pad, tri):
    T, D = x.shape
    meta = pl.BlockSpec((TM, LANES), lambda i: (i, 0))
    return pl.pallas_call(
        _route_kernel,
        grid=(T // TM,),
        in_specs=[pl.BlockSpec((TM, D), lambda i: (i, 0)), _const_spec((1, D)),
                  _const_spec(rw_pad.shape), _const_spec(tri.shape)],
        out_specs=[meta, meta, pl.BlockSpec((8, LANES), lambda i: (0, 0))],
        out_shape=[jax.ShapeDtypeStruct((T, LANES), jnp.int32),
                   jax.ShapeDtypeStruct((T, LANES), F32),
                   jax.ShapeDtypeStruct((8, LANES), F32)],
        scratch_shapes=[pltpu.VMEM((1, LANES), F32)],
        compiler_params=_params(("arbitrary",)),
        name="route",
    )(x, g, rw_pad, tri)


def _experts_kernel(te_ref, src_ref, dst_ref, x_hbm, g_ref, wg_ref, wu_ref, wd_ref, ys_hbm,
                    xbuf, obuf, gsem, ssem, *, n_chunks):
    del te_ref
    i = pl.program_id(0)
    slot = i % 2

    def gather_wait(s):
        pltpu.make_async_copy(x_hbm.at[pl.ds(0, EXPERT_TILE)], xbuf.at[s], gsem.at[s]).wait()

    @pl.when(i == 0)
    def _():
        xbuf[...] = jnp.zeros_like(xbuf)
        obuf[...] = jnp.zeros_like(obuf)

    @pl.when(i >= 1)
    def _():
        gather_wait(1 - slot)

    x = xbuf[1 - slot]
    h = (x * _rms_scale(x) * g_ref[...]).astype(BF16)
    for t in range(EXPERT_TILE):
        pltpu.make_async_copy(x_hbm.at[pl.ds(src_ref[t], 1)], xbuf.at[slot, pl.ds(t, 1)],
                              gsem.at[slot]).start()
        pltpu.make_async_copy(obuf.at[slot, pl.ds(t, 1)], ys_hbm.at[pl.ds(dst_ref[t], 1)],
                              ssem).start()
    y = _swiglu(h, wg_ref.at[0], wu_ref.at[0], wd_ref.at[0], n_chunks)
    pltpu.make_async_copy(obuf.at[slot], ys_hbm.at[pl.ds(0, EXPERT_TILE)], ssem).wait()
    obuf[1 - slot] = y

    @pl.when(i == pl.num_programs(0) - 1)
    def _():
        gather_wait(slot)


def _experts(x, g, src_tok, dst_row, tile_expert, wg, wu, wd):
    T, D = x.shape
    d_ff = wg.shape[-1]
    n_tiles = src_tok.shape[0] // EXPERT_TILE
    n_rows = dst_row.shape[0]

    def w_spec(shape):
        return pl.BlockSpec(shape, lambda i, te: (te[jnp.clip(i - 1, 0, n_tiles - 1)], 0, 0),
                            pipeline_mode=pl.Buffered(1))

    grid_spec = pltpu.PrefetchScalarGridSpec(
        num_scalar_prefetch=1,
        grid=(n_tiles + 2,),
        in_specs=[pl.BlockSpec((EXPERT_TILE,), lambda i, te: (jnp.minimum(i, n_tiles - 1),),
                               memory_space=pltpu.SMEM),
                  pl.BlockSpec((EXPERT_TILE,), lambda i, te: (jnp.maximum(i - 1, 0),),
                               memory_space=pltpu.SMEM),
                  pl.BlockSpec(memory_space=pl.ANY),
                  pl.BlockSpec((1, D), lambda i, te: (0, 0), pipeline_mode=pl.Buffered(1)),
                  w_spec((1, D, d_ff)), w_spec((1, D, d_ff)), w_spec((1, d_ff, D))],
        out_specs=pl.BlockSpec(memory_space=pl.ANY),
        scratch_shapes=[pltpu.VMEM((2, EXPERT_TILE, D), F32), pltpu.VMEM((2, EXPERT_TILE, D), F32),
                        pltpu.SemaphoreType.DMA((2,)), pltpu.SemaphoreType.DMA(())],
    )
    return pl.pallas_call(
        functools.partial(_experts_kernel, n_chunks=_ff_chunks(d_ff)),
        grid_spec=grid_spec,
        out_shape=jax.ShapeDtypeStruct((n_rows, D), F32),
        compiler_params=_params(("arbitrary",)),
        name="experts",
    )(tile_expert, src_tok, dst_row, x, g, wg, wu, wd)


def _combine_kernel(x_ref, gate_ref, g_ref, y0_ref, y1_ref, o_ref, *, final_norm):
    gates = gate_ref[...]
    y = x_ref[...] + (y0_ref[...] * gates[:, 0:1] + y1_ref[...] * gates[:, 1:2])
    if final_norm:
        y = y * _rms_scale(y) * g_ref[...]
    o_ref[...] = y


def _combine(x, gates, ys, g_final, final_norm, first_block, n_blocks):
    T, D = x.shape
    rows = lambda i: (i + first_block, 0)
    return pl.pallas_call(
        functools.partial(_combine_kernel, final_norm=final_norm),
        grid=(n_blocks,),
        in_specs=[pl.BlockSpec((TM, D), rows), pl.BlockSpec((TM, LANES), rows), _const_spec((1, D)),
                  pl.BlockSpec((TM, D), rows),
                  pl.BlockSpec((TM, D), lambda i: (i + first_block + T // TM, 0))],
        out_specs=pl.BlockSpec((TM, D), lambda i: (i, 0)),
        out_shape=jax.ShapeDtypeStruct((n_blocks * TM, D), F32),
        compiler_params=_params(("parallel",)),
        name="combine",
    )(x, gates, g_final, ys, ys)


def _final_norm_kernel(x_ref, g_ref, o_ref):
    x = x_ref[...]
    o_ref[...] = x * _rms_scale(x) * g_ref[...]


def _final_norm(x, g):
    T, D = x.shape
    full = pl.BlockSpec((TM, D), lambda i: (i, 0))
    return pl.pallas_call(
        _final_norm_kernel,
        grid=(T // TM,),
        in_specs=[full, _const_spec((1, D))],
        out_specs=full,
        out_shape=jax.ShapeDtypeStruct((T, D), F32),
        compiler_params=_params(("parallel",)),
        name="final_norm",
    )(x, g)


def _moe(x, g_ffn, router_w, wg, wu, wd, g_final, final_norm, out_blocks):
    T, D = x.shape
    rw_pad = jnp.zeros((D, LANES), F32).at[:, :N_EXPERTS].set(router_w)
    tri = jnp.asarray(np.tril(np.ones((TM, TM), np.float32), -1), BF16)
    idx, gates, cnt = _route(x, g_ffn, rw_pad, tri)

    counts = cnt[0, :N_EXPERTS].astype(jnp.int32)
    padded = (counts + EXPERT_TILE - 1) // EXPERT_TILE * EXPERT_TILE
    pad_end = jnp.cumsum(padded)
    pad_start = pad_end - padded
    dest = (pad_start[idx[:, 0:2]] + idx[:, 2:4]).reshape(2 * T)
    n_rows = 2 * T + N_EXPERTS * EXPERT_TILE
    n_tiles = n_rows // EXPERT_TILE
    tile_expert = jnp.minimum(
        jnp.searchsorted(pad_end, jnp.arange(n_tiles, dtype=jnp.int32) * EXPERT_TILE, side="right"),
        N_EXPERTS - 1).astype(jnp.int32)

    assign = jnp.arange(2 * T, dtype=jnp.int32)
    out_of_assign = (assign % 2) * T + assign // 2
    filled = jnp.zeros((n_rows,), jnp.int32).at[dest].set(out_of_assign + 1)
    is_pad = filled == 0
    pad_rank = jnp.cumsum(is_pad.astype(jnp.int32)) - 1
    src_tok = jnp.where(is_pad, 0, (filled - 1) % T)
    dst_row = jnp.where(is_pad, 2 * T + pad_rank, filled - 1)
    spare = n_rows + jnp.arange(EXPERT_TILE, dtype=jnp.int32)
    dst_row = jnp.concatenate([spare, dst_row])

    ys = _experts(x, g_ffn, src_tok, dst_row, tile_expert, wg, wu, wd)
    return [_combine(x, gates, ys, g_final, final_norm, b0, nb) for b0, nb in out_blocks]


def _image_blocks(image_rows):
    lo, hi, start = [], [], 0
    for rows in image_rows:
        n = rows // ROWS_PER_BLOCK
        lo += [start] * n
        hi += [start + n - 1] * n
        start += n
    return jnp.asarray(lo, jnp.int32), jnp.asarray(hi, jnp.int32)


def kernel(x_prompt, x_sample, norm_mix_g, w_in, na_rpb, sg_norm_g, sg_w, sg_b, out_norm_g, w_out,
           norm_ffn_g, dense_w_gate, dense_w_up, dense_w_down,
           router_w, moe_w_gate, moe_w_up, moe_w_down, final_norm_g):
    D = x_prompt.shape[-1]
    depth = w_in.shape[0]
    assert D == 2 * HALF
    image_rows = []
    for xs in (x_prompt, x_sample):
        assert xs.shape[1] % TM == 0 and xs.shape[1] >= WIN_BLOCKS * TM
        image_rows += [xs.shape[1] // GRID_W] * xs.shape[0]
    blk_lo, blk_hi = _image_blocks(image_rows)
    n_prompt = x_prompt.shape[0] * x_prompt.shape[1]

    x = jnp.concatenate([x_prompt.reshape(-1, D), x_sample.reshape(-1, D)], axis=0)
    bd = jnp.asarray(np.kron(np.eye(N_HEADS, dtype=np.float32),
                             np.ones((HEAD_DIM, HEAD_DIM), np.float32)), BF16)
    g_final = final_norm_g.reshape(1, D)
    na_bias = _na_bias_tables(na_rpb)

    for l in range(depth):
        q, k, v, u, svn = _inproj(x, norm_mix_g[l].reshape(1, D), w_in[l].astype(BF16),
                                  sg_norm_g[l].reshape(1, HALF), bd)
        attn = _natten(q, k, v, na_bias[l], blk_lo, blk_hi)
        sgw2 = sg_w[l].astype(BF16).reshape(N_PAIRS, 2 * SG_CHUNK, SG_CHUNK)
        sgb_t = jnp.repeat(jnp.transpose(sg_b[l]), HEAD_DIM, axis=1)
        x = _mixout(attn, u, svn, x, sgw2, sgb_t, out_norm_g[l].reshape(1, D), w_out[l].astype(BF16))
        g_ffn = norm_ffn_g[l].reshape(1, D)
        i = l // 2
        if l % 2 == 0:
            x = _ffn(x, g_ffn, dense_w_gate[i].astype(BF16), dense_w_up[i].astype(BF16),
                     dense_w_down[i].astype(BF16))
            if l == depth - 1:
                x = _final_norm(x, g_final)
                outs = [x[:n_prompt], x[n_prompt:]]
        else:
            last = l == depth - 1
            nb = x.shape[0] // TM
            npb = n_prompt // TM
            outs = _moe(x, g_ffn, router_w[i], moe_w_gate[i].astype(BF16), moe_w_up[i].astype(BF16),
                        moe_w_down[i].astype(BF16), g_final, final_norm=last,
                        out_blocks=[(0, npb), (npb, nb - npb)] if last else [(0, nb)])
            x = outs[0]
    return (outs[0].reshape(x_prompt.shape), outs[1].reshape(x_sample.shape))
```

```python
import functools

import numpy as np
import jax
import jax.numpy as jnp
from jax import lax
from jax.experimental import pallas as pl
from jax.experimental.pallas import tpu as pltpu

F32 = jnp.float32
BF16 = jnp.bfloat16

GRID_W = 64
WIN_ROWS = 8
WIN_COLS = 16
N_HEADS = 8
HEAD_DIM = 64
HALF = N_HEADS * HEAD_DIM
N_PAIRS = N_HEADS // 2
LANES = 128
SG_CHUNK = 128
N_EXPERTS = 8
RMS_EPS = 1e-6
MASK_VALUE = -1e30

TM = 512
ROWS_PER_BLOCK = TM // GRID_W
WIN_BLOCKS = 3
LOG2_E = 1.4426950408889634
EXPERT_TILE = 512
VMEM_LIMIT = 56 * 1024 * 1024


def _params(sem, vmem=VMEM_LIMIT):
    return pltpu.CompilerParams(dimension_semantics=sem, vmem_limit_bytes=vmem)


def _const_spec(shape):
    nd = len(shape)
    return pl.BlockSpec(shape, lambda *_: (0,) * nd, pipeline_mode=pl.Buffered(1))


def _rms_scale(x):
    return lax.rsqrt(jnp.mean(x * x, axis=-1, keepdims=True) + RMS_EPS)


def _stream_specs(parts):
    D = parts[0].shape[1]
    if len(parts) == 1:
        return [pl.BlockSpec((TM, D), lambda i: (i, 0))]
    first = parts[0].shape[0] // TM
    return [pl.BlockSpec((TM, D), lambda i: (jnp.minimum(i, first - 1), 0)),
            pl.BlockSpec((TM, D), lambda i: (jnp.maximum(i - first, 0), 0))]


def _stream_tile(refs, first):
    if len(refs) == 1:
        return refs[0][...]
    return jnp.where(pl.program_id(0) < first, refs[0][...], refs[1][...])


def _gelu_tanh(x):
    return x * (0.5 * (1.0 + jnp.tanh(0.7978845608028654 * (x + 0.044715 * (x * x * x)))))


def _inproj_kernel(*refs, n_x, first):
    g_ref, w_ref, sgg_ref, bd_ref, q_ref, k_ref, v_ref, u_ref, sv_ref = refs[n_x:]
    x = _stream_tile(refs[:n_x], first)
    h = (x * _rms_scale(x) * g_ref[...]).astype(BF16)

    def proj(n):
        return jnp.dot(h, w_ref[:, n * HALF:(n + 1) * HALF], preferred_element_type=F32)

    q = proj(0) * (HEAD_DIM ** -0.5 * LOG2_E)
    is_even = (lax.broadcasted_iota(jnp.int32, q.shape, 1) % LANES) < HEAD_DIM
    qe = jnp.where(is_even, q, 0.0).astype(BF16)
    qo = jnp.where(is_even, 0.0, q).astype(BF16)
    for r in range(ROWS_PER_BLOCK):
        rows = slice(r * GRID_W, (r + 1) * GRID_W)
        q_ref[2 * r * GRID_W:(2 * r + 1) * GRID_W, :] = qe[rows]
        q_ref[(2 * r + 1) * GRID_W:(2 * r + 2) * GRID_W, :] = qo[rows]
    k_ref[...] = proj(1).astype(BF16)
    v_ref[...] = proj(2).astype(BF16)
    u_ref[...] = _gelu_tanh(proj(3)).astype(BF16)
    sv = _gelu_tanh(proj(4))
    ssq = jnp.dot((sv * sv).astype(BF16), bd_ref[...], preferred_element_type=F32)
    sv_ref[...] = (sv * lax.rsqrt(ssq * (1.0 / HEAD_DIM) + RMS_EPS) * sgg_ref[...]).astype(BF16)


def _inproj(xs, g, w_in, sgg, bd):
    T = sum(x.shape[0] for x in xs)
    D = xs[0].shape[1]
    out = jax.ShapeDtypeStruct((T, HALF), BF16)
    tile = pl.BlockSpec((TM, HALF), lambda i: (i, 0))
    return pl.pallas_call(
        functools.partial(_inproj_kernel, n_x=len(xs), first=xs[0].shape[0] // TM),
        grid=(T // TM,),
        in_specs=_stream_specs(xs) + [_const_spec((1, D)), _const_spec(w_in.shape),
                                      _const_spec((1, HALF)), _const_spec((HALF, HALF))],
        out_specs=[pl.BlockSpec((2 * TM, HALF), lambda i: (i, 0))] + [tile] * 4,
        out_shape=[jax.ShapeDtypeStruct((2 * T, HALF), BF16)] + [out] * 4,
        compiler_params=_params(("parallel",)),
        name="inproj",
    )(*xs, g, w_in, sgg, bd)


def _natten_kernel(lo_ref, hi_ref, q_ref, kwin, vwin, bias_ref, o_ref):
    b = pl.program_id(0)
    first = lo_ref[b]
    blk = b - first
    n_blk = hi_ref[b] - first + 1
    n_rows = n_blk * ROWS_PER_BLOCK
    win_row0 = jnp.clip(blk - 1, 0, n_blk - WIN_BLOCKS) * ROWS_PER_BLOCK
    out_even = lax.broadcasted_iota(jnp.int32, (GRID_W, LANES), 1) < HEAD_DIM
    n_keys = WIN_ROWS * GRID_W

    def row_body(i, carry):
        r = blk * ROWS_PER_BLOCK + i
        rs = jnp.clip(r - WIN_ROWS // 2, 0, n_rows - WIN_ROWS)
        variant = rs - r + (WIN_ROWS - 1)
        koff = pl.multiple_of((rs - win_row0) * GRID_W, GRID_W)
        qoff = pl.multiple_of(i * 2 * GRID_W, 2 * GRID_W)
        scores = []
        for p in range(N_PAIRS):
            cols = slice(p * LANES, (p + 1) * LANES)
            s = lax.dot_general(q_ref[pl.ds(qoff, 2 * GRID_W), cols], kwin[pl.ds(koff, n_keys), cols],
                                (((1,), (1,)), ((), ())), preferred_element_type=F32)
            scores.append(s + bias_ref[variant, p])
        for p in range(N_PAIRS):
            cols = slice(p * LANES, (p + 1) * LANES)
            s = scores[p]
            e = jnp.exp2(s - jnp.max(s, axis=-1, keepdims=True))
            denom = jnp.sum(e, axis=-1, keepdims=True)
            o2 = jnp.dot(e.astype(BF16), vwin[pl.ds(koff, n_keys), cols],
                         preferred_element_type=F32) * (1.0 / denom)
            out = jnp.where(out_even, o2[:GRID_W], o2[GRID_W:])
            o_ref[pl.ds(pl.multiple_of(i * GRID_W, GRID_W), GRID_W), cols] = out.astype(BF16)
        return carry

    lax.fori_loop(0, ROWS_PER_BLOCK, row_body, 0)


def _natten(q2, k, v, bias, blk_lo, blk_hi):
    T = k.shape[0]

    def win_map(b, lo, hi):
        return (jnp.clip(b - 1, lo[b], hi[b] - (WIN_BLOCKS - 1)) * TM, 0)

    window = pl.BlockSpec((pl.Element(WIN_BLOCKS * TM), pl.Element(HALF)), win_map)
    grid_spec = pltpu.PrefetchScalarGridSpec(
        num_scalar_prefetch=2,
        grid=(T // TM,),
        in_specs=[pl.BlockSpec((2 * TM, HALF), lambda b, lo, hi: (b, 0)), window, window,
                  pl.BlockSpec(bias.shape, lambda b, lo, hi: (0, 0, 0, 0),
                               pipeline_mode=pl.Buffered(1))],
        out_specs=pl.BlockSpec((TM, HALF), lambda b, lo, hi: (b, 0)),
    )
    return pl.pallas_call(
        _natten_kernel,
        grid_spec=grid_spec,
        out_shape=jax.ShapeDtypeStruct((T, HALF), BF16),
        compiler_params=_params(("parallel",)),
        name="natten",
    )(blk_lo, blk_hi, q2, k, v, bias)


def _na_bias_tables(rpb):
    n_layers = rpb.shape[0]
    qc = np.arange(GRID_W)
    kc = np.arange(GRID_W)
    cs = np.clip(qc - WIN_COLS // 2, 0, GRID_W - WIN_COLS)
    col_in = (kc[None, :] >= cs[:, None]) & (kc[None, :] < cs[:, None] + WIN_COLS)
    dc = np.clip(kc[None, :] - qc[:, None], -(WIN_COLS - 1), WIN_COLS - 1) + (WIN_COLS - 1)
    onehot = (dc.reshape(-1)[None, :] == np.arange(2 * WIN_COLS - 1)[:, None]).astype(np.float32)
    band = jnp.dot(rpb.astype(F32).reshape(-1, 2 * WIN_COLS - 1), onehot,
                   precision=lax.Precision.HIGHEST)
    band = band.reshape(n_layers, N_HEADS, 2 * WIN_ROWS - 1, GRID_W, GRID_W)
    band = jnp.where(col_in[None, None, None], band * LOG2_E, MASK_VALUE)
    band = jnp.transpose(band, (0, 1, 3, 2, 4))
    t = jnp.stack([band[:, :, :, o:o + WIN_ROWS, :] for o in range(WIN_ROWS)], axis=1)
    return t.reshape(n_layers, WIN_ROWS, N_PAIRS, 2 * GRID_W, WIN_ROWS * GRID_W)


def _mixout_kernel(*refs, n_x, first):
    a_ref, u_ref, sv_ref, sgw_ref, sgb_ref, g_ref, w_ref, o_ref, sg_s = refs[n_x:]
    even = lax.broadcasted_iota(jnp.int32, (SG_CHUNK, LANES), 1) < HEAD_DIM
    for c in range(TM // SG_CHUNK):
        rows = slice(c * SG_CHUNK, (c + 1) * SG_CHUNK)
        for p in range(N_PAIRS):
            cols = slice(p * LANES, (p + 1) * LANES)
            m2 = jnp.dot(sgw_ref[p], sv_ref[rows, cols], preferred_element_type=F32)
            mixed = jnp.where(even, m2[:SG_CHUNK], m2[SG_CHUNK:])
            sg_s[rows, cols] = u_ref[rows, cols].astype(F32) * (mixed + sgb_ref[:, cols])
    sg = sg_s[...]
    sgn = (sg * _rms_scale(sg) * g_ref[:, HALF:]).astype(BF16)
    a = a_ref[...].astype(F32)
    an = (a * _rms_scale(a) * g_ref[:, :HALF]).astype(BF16)
    y = jnp.dot(an, w_ref[:HALF, :], preferred_element_type=F32)
    y = y + jnp.dot(sgn, w_ref[HALF:, :], preferred_element_type=F32)
    o_ref[...] = _stream_tile(refs[:n_x], first) + y


def _mixout(attn, u, svn, xs, sgw2, sgb_t, g_out, w_out):
    T = sum(x.shape[0] for x in xs)
    D = xs[0].shape[1]
    half = pl.BlockSpec((TM, HALF), lambda i: (i, 0))
    full = pl.BlockSpec((TM, D), lambda i: (i, 0))
    return pl.pallas_call(
        functools.partial(_mixout_kernel, n_x=len(xs), first=xs[0].shape[0] // TM),
        grid=(T // TM,),
        in_specs=_stream_specs(xs) + [
                  half, half, half, _const_spec(sgw2.shape), _const_spec(sgb_t.shape),
                  _const_spec((1, 2 * HALF)), _const_spec(w_out.shape)],
        out_specs=full,
        out_shape=jax.ShapeDtypeStruct((T, D), F32),
        scratch_shapes=[pltpu.VMEM((TM, HALF), F32)],
        compiler_params=_params(("parallel",)),
        name="mixout",
    )(*xs, attn, u, svn, sgw2, sgb_t, g_out, w_out)


def _ff_chunks(d_ff):
    n = 1
    while d_ff // n > 2048 or d_ff % n or (d_ff // n) % LANES:
        n += 1
    return n


def _swiglu(h, wg_ref, wu_ref, wd_ref, n_chunks):
    d_ff = wd_ref.shape[-2]
    fc = d_ff // n_chunks
    y = None
    for c in range(n_chunks):
        cols = slice(c * fc, (c + 1) * fc)
        gate = jnp.dot(h, wg_ref[:, cols], preferred_element_type=F32)
        up = jnp.dot(h, wu_ref[:, cols], preferred_element_type=F32)
        act = (gate / (1.0 + jnp.exp(-gate)) * up).astype(BF16)
        part = jnp.dot(act, wd_ref[cols, :], preferred_element_type=F32)
        y = part if y is None else y + part
    return y


def _ffn_kernel(x_ref, g_ref, wg_ref, wu_ref, wd_ref, o_ref, *, n_chunks):
    x = x_ref[...]
    h = (x * _rms_scale(x) * g_ref[...]).astype(BF16)
    o_ref[...] = x + _swiglu(h, wg_ref, wu_ref, wd_ref, n_chunks)


def _ffn(x, g, wg, wu, wd):
    T, D = x.shape
    full = pl.BlockSpec((TM, D), lambda i: (i, 0))
    return pl.pallas_call(
        functools.partial(_ffn_kernel, n_chunks=_ff_chunks(wg.shape[1])),
        grid=(T // TM,),
        in_specs=[full, _const_spec((1, D)), _const_spec(wg.shape), _const_spec(wu.shape),
                  _const_spec(wd.shape)],
        out_specs=full,
        out_shape=jax.ShapeDtypeStruct((T, D), F32),
        compiler_params=_params(("parallel",)),
        name="ffn",
    )(x, g, wg, wu, wd)


def _route_kernel(x_ref, g_ref, rw_ref, tri_ref, idx_ref, gate_ref, cnt_ref, carry):
    @pl.when(pl.program_id(0) == 0)
    def _():
        carry[...] = jnp.zeros_like(carry)

    x = x_ref[...]
    h = x * _rms_scale(x) * g_ref[...]
    h_hi = h.astype(BF16)
    h_lo = (h - h_hi.astype(F32)).astype(BF16)
    logits = (jnp.dot(h_hi, rw_ref[0], preferred_element_type=F32)
              + jnp.dot(h_lo, rw_ref[0], preferred_element_type=F32)
              + jnp.dot(h_hi, rw_ref[1], preferred_element_type=F32))
    col = lax.broadcasted_iota(jnp.int32, logits.shape, 1)
    colf = col.astype(F32)
    neg = -jnp.inf
    lg = jnp.where(col < N_EXPERTS, logits, neg)
    m1 = jnp.max(lg, axis=-1, keepdims=True)
    i1 = jnp.min(jnp.where(lg == m1, colf, float(LANES)), axis=-1, keepdims=True)
    sel1 = colf == i1
    lg2 = jnp.where(sel1, neg, lg)
    m2 = jnp.max(lg2, axis=-1, keepdims=True)
    i2 = jnp.min(jnp.where(lg2 == m2, colf, float(LANES)), axis=-1, keepdims=True)
    sel2 = colf == i2
    e2 = jnp.exp(m2 - m1)
    g1 = 1.0 / (1.0 + e2)
    g2 = e2 / (1.0 + e2)

    cnt = jnp.where(sel1 | sel2, 1.0, 0.0)
    before = jnp.dot(tri_ref[...], cnt.astype(BF16), preferred_element_type=F32) + carry[...]
    r1 = jnp.sum(jnp.where(sel1, before, 0.0), axis=-1, keepdims=True)
    r2 = jnp.sum(jnp.where(sel2, before, 0.0), axis=-1, keepdims=True)
    total = carry[...] + jnp.sum(cnt, axis=0, keepdims=True)
    carry[...] = total
    cnt_ref[...] = jnp.broadcast_to(total, cnt_ref.shape)

    meta = jnp.where(col == 0, i1, jnp.where(col == 1, i2, jnp.where(col == 2, r1, r2)))
    idx_ref[...] = meta.astype(jnp.int32)
    gate_ref[...] = jnp.where(col == 0, g1, g2)


def _route(x, g, rw_pad, tri):
    T, D = x.shape
    meta = pl.BlockSpec((TM, LANES), lambda i: (i, 0))
    return pl.pallas_call(
        _route_kernel,
        grid=(T // TM,),
        in_specs=[pl.BlockSpec((TM, D), lambda i: (i, 0)), _const_spec((1, D)),
                  _const_spec(rw_pad.shape), _const_spec(tri.shape)],
        out_specs=[meta, meta, pl.BlockSpec((8, LANES), lambda i: (0, 0))],
        out_shape=[jax.ShapeDtypeStruct((T, LANES), jnp.int32),
                   jax.ShapeDtypeStruct((T, LANES), F32),
                   jax.ShapeDtypeStruct((8, LANES), F32)],
        scratch_shapes=[pltpu.VMEM((1, LANES), F32)],
        compiler_params=_params(("arbitrary",)),
        name="route",
    )(x, g, rw_pad, tri)


def _experts_kernel(te_ref, src_ref, dst_ref, x_hbm, g_ref, wg_ref, wu_ref, wd_ref, ys_hbm,
                    xbuf, obuf, gsem, ssem, *, n_chunks):
    del te_ref
    i = pl.program_id(0)
    slot = i % 2

    def gather_wait(s):
        pltpu.make_async_copy(x_hbm.at[pl.ds(0, EXPERT_TILE)], xbuf.at[s], gsem.at[s]).wait()

    @pl.when(i == 0)
    def _():
        xbuf[...] = jnp.zeros_like(xbuf)
        obuf[...] = jnp.zeros_like(obuf)

    @pl.when(i >= 1)
    def _():
        gather_wait(1 - slot)

    x = xbuf[1 - slot]
    h = (x * _rms_scale(x) * g_ref[...]).astype(BF16)
    for t in range(EXPERT_TILE):
        pltpu.make_async_copy(x_hbm.at[pl.ds(src_ref[t], 1)], xbuf.at[slot, pl.ds(t, 1)],
                              gsem.at[slot]).start()
        pltpu.make_async_copy(obuf.at[slot, pl.ds(t, 1)], ys_hbm.at[pl.ds(dst_ref[t], 1)],
                              ssem).start()
    y = _swiglu(h, wg_ref.at[0], wu_ref.at[0], wd_ref.at[0], n_chunks)
    pltpu.make_async_copy(obuf.at[slot], ys_hbm.at[pl.ds(0, EXPERT_TILE)], ssem).wait()
    obuf[1 - slot] = y

    @pl.when(i == pl.num_programs(0) - 1)
    def _():
        gather_wait(slot)


def _experts(x, g, src_tok, dst_row, tile_expert, wg, wu, wd):
    T, D = x.shape
    d_ff = wg.shape[-1]
    n_tiles = src_tok.shape[0] // EXPERT_TILE
    n_rows = dst_row.shape[0]

    def w_spec(shape):
        return pl.BlockSpec(shape, lambda i, te: (te[jnp.clip(i - 1, 0, n_tiles - 1)], 0, 0),
                            pipeline_mode=pl.Buffered(1))

    grid_spec = pltpu.PrefetchScalarGridSpec(
        num_scalar_prefetch=1,
        grid=(n_tiles + 2,),
        in_specs=[pl.BlockSpec((EXPERT_TILE,), lambda i, te: (jnp.minimum(i, n_tiles - 1),),
                               memory_space=pltpu.SMEM),
                  pl.BlockSpec((EXPERT_TILE,), lambda i, te: (jnp.maximum(i - 1, 0),),
                               memory_space=pltpu.SMEM),
                  pl.BlockSpec(memory_space=pl.ANY),
                  pl.BlockSpec((1, D), lambda i, te: (0, 0), pipeline_mode=pl.Buffered(1)),
                  w_spec((1, D, d_ff)), w_spec((1, D, d_ff)), w_spec((1, d_ff, D))],
        out_specs=pl.BlockSpec(memory_space=pl.ANY),
        scratch_shapes=[pltpu.VMEM((2, EXPERT_TILE, D), F32), pltpu.VMEM((2, EXPERT_TILE, D), F32),
                        pltpu.SemaphoreType.DMA((2,)), pltpu.SemaphoreType.DMA(())],
    )
    return pl.pallas_call(
        functools.partial(_experts_kernel, n_chunks=_ff_chunks(d_ff)),
        grid_spec=grid_spec,
        out_shape=jax.ShapeDtypeStruct((n_rows, D), F32),
        compiler_params=_params(("arbitrary",)),
        name="experts",
    )(tile_expert, src_tok, dst_row, x, g, wg, wu, wd)


def _combine_kernel(x_ref, gate_ref, g_ref, y0_ref, y1_ref, o_ref, *, final_norm):
    gates = gate_ref[...]
    y = x_ref[...] + (y0_ref[...] * gates[:, 0:1] + y1_ref[...] * gates[:, 1:2])
    if final_norm:
        y = y * _rms_scale(y) * g_ref[...]
    o_ref[...] = y


def _combine(x, gates, ys, g_final, final_norm, first_block, n_blocks):
    T, D = x.shape
    rows = lambda i: (i + first_block, 0)
    return pl.pallas_call(
        functools.partial(_combine_kernel, final_norm=final_norm),
        grid=(n_blocks,),
        in_specs=[pl.BlockSpec((TM, D), rows), pl.BlockSpec((TM, LANES), rows), _const_spec((1, D)),
                  pl.BlockSpec((TM, D), rows),
                  pl.BlockSpec((TM, D), lambda i: (i + first_block + T // TM, 0))],
        out_specs=pl.BlockSpec((TM, D), lambda i: (i, 0)),
        out_shape=jax.ShapeDtypeStruct((n_blocks * TM, D), F32),
        compiler_params=_params(("parallel",)),
        name="combine",
    )(x, gates, g_final, ys, ys)


def _final_norm_kernel(x_ref, g_ref, o_ref):
    x = x_ref[...]
    o_ref[...] = x * _rms_scale(x) * g_ref[...]


def _final_norm(x, g):
    T, D = x.shape
    full = pl.BlockSpec((TM, D), lambda i: (i, 0))
    return pl.pallas_call(
        _final_norm_kernel,
        grid=(T // TM,),
        in_specs=[full, _const_spec((1, D))],
        out_specs=full,
        out_shape=jax.ShapeDtypeStruct((T, D), F32),
        compiler_params=_params(("parallel",)),
        name="final_norm",
    )(x, g)


def _moe(x, g_ffn, router_w, wg, wu, wd, g_final, final_norm, out_blocks):
    T, D = x.shape
    rw_pad = jnp.zeros((D, LANES), F32).at[:, :N_EXPERTS].set(router_w)
    rw_hi = rw_pad.astype(BF16)
    rw_pad = jnp.stack([rw_hi, (rw_pad - rw_hi.astype(F32)).astype(BF16)])
    tri = jnp.asarray(np.tril(np.ones((TM, TM), np.float32), -1), BF16)
    idx, gates, cnt = _route(x, g_ffn, rw_pad, tri)

    counts = cnt[0, :N_EXPERTS].astype(jnp.int32)
    padded = (counts + EXPERT_TILE - 1) // EXPERT_TILE * EXPERT_TILE
    pad_end = jnp.cumsum(padded)
    pad_start = pad_end - padded
    dest = (pad_start[idx[:, 0:2]] + idx[:, 2:4]).reshape(2 * T)
    n_rows = 2 * T + N_EXPERTS * EXPERT_TILE
    n_tiles = n_rows // EXPERT_TILE
    tile_expert = jnp.minimum(
        jnp.searchsorted(pad_end, jnp.arange(n_tiles, dtype=jnp.int32) * EXPERT_TILE, side="right"),
        N_EXPERTS - 1).astype(jnp.int32)

    assign = jnp.arange(2 * T, dtype=jnp.int32)
    out_of_assign = (assign % 2) * T + assign // 2
    filled = jnp.zeros((n_rows,), jnp.int32).at[dest].set(
        out_of_assign + 1, unique_indices=True, mode="promise_in_bounds")
    is_pad = filled == 0
    pad_rank = jnp.cumsum(is_pad.astype(jnp.int32)) - 1
    src_tok = jnp.where(is_pad, 0, (filled - 1) % T)
    dst_row = jnp.where(is_pad, 2 * T + pad_rank, filled - 1)
    spare = n_rows + jnp.arange(EXPERT_TILE, dtype=jnp.int32)
    dst_row = jnp.concatenate([spare, dst_row])

    ys = _experts(x, g_ffn, src_tok, dst_row, tile_expert, wg, wu, wd)
    return [_combine(x, gates, ys, g_final, final_norm, b0, nb) for b0, nb in out_blocks]


def _image_blocks(image_rows):
    lo, hi, start = [], [], 0
    for rows in image_rows:
        n = rows // ROWS_PER_BLOCK
        lo += [start] * n
        hi += [start + n - 1] * n
        start += n
    return jnp.asarray(lo, jnp.int32), jnp.asarray(hi, jnp.int32)


def kernel(x_prompt, x_sample, norm_mix_g, w_in, na_rpb, sg_norm_g, sg_w, sg_b, out_norm_g, w_out,
           norm_ffn_g, dense_w_gate, dense_w_up, dense_w_down,
           router_w, moe_w_gate, moe_w_up, moe_w_down, final_norm_g):
    D = x_prompt.shape[-1]
    depth = w_in.shape[0]
    assert D == 2 * HALF
    image_rows = []
    for xs in (x_prompt, x_sample):
        assert xs.shape[1] % TM == 0 and xs.shape[1] >= WIN_BLOCKS * TM
        image_rows += [xs.shape[1] // GRID_W] * xs.shape[0]
    blk_lo, blk_hi = _image_blocks(image_rows)
    n_prompt = x_prompt.shape[0] * x_prompt.shape[1]

    xs = (x_prompt.reshape(-1, D), x_sample.reshape(-1, D))
    bd =jnp.asarray(np.kron(np.eye(N_HEADS, dtype=np.float32),
                             np.ones((HEAD_DIM, HEAD_DIM), np.float32)), BF16)
    g_final = final_norm_g.reshape(1, D)
    na_bias = _na_bias_tables(na_rpb)

    for l in range(depth):
        q, k, v, u, svn = _inproj(xs, norm_mix_g[l].reshape(1, D), w_in[l].astype(BF16),
                                  sg_norm_g[l].reshape(1, HALF), bd)
        attn = _natten(q, k, v, na_bias[l], blk_lo, blk_hi)
        sgw2 = sg_w[l].astype(BF16).reshape(N_PAIRS, 2 * SG_CHUNK, SG_CHUNK)
        sgb_t = jnp.repeat(jnp.transpose(sg_b[l]), HEAD_DIM, axis=1)
        x = _mixout(attn, u, svn, xs, sgw2, sgb_t, out_norm_g[l].reshape(1, D),
                    w_out[l].astype(BF16))
        g_ffn = norm_ffn_g[l].reshape(1, D)
        i = l // 2
        if l % 2 == 0:
            x = _ffn(x, g_ffn, dense_w_gate[i].astype(BF16), dense_w_up[i].astype(BF16),
                     dense_w_down[i].astype(BF16))
            if l == depth - 1:
                x = _final_norm(x, g_final)
                outs = [x[:n_prompt], x[n_prompt:]]
        else:
            last = l == depth - 1
            nb = x.shape[0] // TM
            npb = n_prompt // TM
            outs = _moe(x, g_ffn, router_w[i], moe_w_gate[i].astype(BF16), moe_w_up[i].astype(BF16),
                        moe_w_down[i].astype(BF16), g_final, final_norm=last,
                        out_blocks=[(0, npb), (npb, nb - npb)] if last else [(0, nb)])
            x = outs[0]
        xs = (x,)
    return (outs[0].reshape(x_prompt.shape), outs[1].reshape(x_sample.shape))
```

```python
import functools

import numpy as np
import jax
import jax.numpy as jnp
from jax import lax
from jax.experimental import pallas as pl
from jax.experimental.pallas import tpu as pltpu

F32 = jnp.float32
BF16 = jnp.bfloat16

GRID_W = 64
WIN_ROWS = 8
WIN_COLS = 16
N_HEADS = 8
HEAD_DIM = 64
HALF = N_HEADS * HEAD_DIM
N_PAIRS = N_HEADS // 2
LANES = 128
SG_CHUNK = 128
N_EXPERTS = 8
RMS_EPS = 1e-6
MASK_VALUE = -1e30

TM = 512
ROWS_PER_BLOCK = TM // GRID_W
WIN_BLOCKS = 3
LOG2_E = 1.4426950408889634
NATTEN_UNROLL = ROWS_PER_BLOCK
EXPERT_TILE = 512
VMEM_LIMIT = 56 * 1024 * 1024


def _params(sem, vmem=VMEM_LIMIT):
    return pltpu.CompilerParams(dimension_semantics=sem, vmem_limit_bytes=vmem)


def _const_spec(shape):
    nd = len(shape)
    return pl.BlockSpec(shape, lambda *_: (0,) * nd, pipeline_mode=pl.Buffered(1))


def _layer_spec(w, layer):
    nd = w.ndim - 1
    return pl.BlockSpec((None,) + w.shape[1:], lambda *_: (layer,) + (0,) * nd,
                        pipeline_mode=pl.Buffered(1))


def _rms_scale(x):
    return lax.rsqrt(jnp.mean(x * x, axis=-1, keepdims=True) + RMS_EPS)


def _stream_specs(parts):
    D = parts[0].shape[1]
    if len(parts) == 1:
        return [pl.BlockSpec((TM, D), lambda i: (i, 0))]
    first = parts[0].shape[0] // TM
    return [pl.BlockSpec((TM, D), lambda i: (jnp.minimum(i, first - 1), 0)),
            pl.BlockSpec((TM, D), lambda i: (jnp.maximum(i - first, 0), 0))]


def _stream_tile(refs, first):
    if len(refs) == 1:
        return refs[0][...]
    return jnp.where(pl.program_id(0) < first, refs[0][...], refs[1][...])


def _gelu_tanh(x):
    return x * (0.5 * (1.0 + jnp.tanh(0.7978845608028654 * (x + 0.044715 * (x * x * x)))))


def _inproj_kernel(*refs, n_x, first, pending_moe):
    x = _stream_tile(refs[:n_x], first)
    if pending_moe:
        y0_ref, y1_ref, gate_ref = refs[n_x:n_x + 3]
        g_ref, w_ref, sgg_ref, bd_ref, q_ref, k_ref, v_ref, u_ref, sv_ref, x_out_ref = refs[n_x + 3:]
        gates = gate_ref[...]
        x = x + (y0_ref[...] * gates[:, 0:1] + y1_ref[...] * gates[:, 1:2])
        x_out_ref[...] = x
    else:
        g_ref, w_ref, sgg_ref, bd_ref, q_ref, k_ref, v_ref, u_ref, sv_ref = refs[n_x:]
    h = (x * _rms_scale(x) * g_ref[...]).astype(BF16)

    def proj(n):
        return jnp.dot(h, w_ref[:, n * HALF:(n + 1) * HALF], preferred_element_type=F32)

    q = proj(0) * (HEAD_DIM ** -0.5 * LOG2_E)
    is_even = (lax.broadcasted_iota(jnp.int32, q.shape, 1) % LANES) < HEAD_DIM
    qe = jnp.where(is_even, q, 0.0).astype(BF16)
    qo = jnp.where(is_even, 0.0, q).astype(BF16)
    for r in range(ROWS_PER_BLOCK):
        rows = slice(r * GRID_W, (r + 1) * GRID_W)
        q_ref[2 * r * GRID_W:(2 * r + 1) * GRID_W, :] = qe[rows]
        q_ref[(2 * r + 1) * GRID_W:(2 * r + 2) * GRID_W, :] = qo[rows]
    k_ref[...] = proj(1).astype(BF16)
    v_ref[...] = proj(2).astype(BF16)
    u_ref[...] = _gelu_tanh(proj(3)).astype(BF16)
    sv = _gelu_tanh(proj(4))
    ssq = jnp.dot((sv * sv).astype(BF16), bd_ref[...], preferred_element_type=F32)
    sv_ref[...] = (sv * lax.rsqrt(ssq * (1.0 / HEAD_DIM) + RMS_EPS) * sgg_ref[...]).astype(BF16)


def _inproj(xs, g, w_in, layer, sgg, bd, pending_moe=None):
    T = sum(x.shape[0] for x in xs)
    D = xs[0].shape[1]
    out = jax.ShapeDtypeStruct((T, HALF), BF16)
    tile = pl.BlockSpec((TM, HALF), lambda i: (i, 0))
    full = pl.BlockSpec((TM, D), lambda i: (i, 0))
    moe_args, moe_specs, moe_out_specs, moe_out_shape = [], [], [], []
    if pending_moe is not None:
        ys, gates = pending_moe
        moe_args = [ys, ys, gates]
        moe_specs = [full, pl.BlockSpec((TM, D), lambda i: (i + T // TM, 0)),
                     pl.BlockSpec((TM, LANES), lambda i: (i, 0))]
        moe_out_specs = [full]
        moe_out_shape = [jax.ShapeDtypeStruct((T, D), F32)]
    return pl.pallas_call(
        functools.partial(_inproj_kernel, n_x=len(xs), first=xs[0].shape[0] // TM,
                          pending_moe=pending_moe is not None),
        grid=(T // TM,),
        in_specs=_stream_specs(xs) + moe_specs + [
            _const_spec((1, D)), _layer_spec(w_in, layer),
            _const_spec((1, HALF)), _const_spec((HALF, HALF))],
        out_specs=[pl.BlockSpec((2 * TM, HALF), lambda i: (i, 0))] + [tile] * 4 + moe_out_specs,
        out_shape=[jax.ShapeDtypeStruct((2 * T, HALF), BF16)] + [out] * 4 + moe_out_shape,
        compiler_params=_params(("parallel",)),
        name="inproj",
    )(*xs, *moe_args, g, w_in, sgg, bd)


def _natten_kernel(lo_ref, hi_ref, q_ref, kwin, vwin, bias_ref, o_ref):
    b = pl.program_id(0)
    first = lo_ref[b]
    blk = b - first
    n_blk = hi_ref[b] - first + 1
    n_rows = n_blk * ROWS_PER_BLOCK
    win_row0 = jnp.clip(blk - 1, 0, n_blk - WIN_BLOCKS) * ROWS_PER_BLOCK
    out_even = lax.broadcasted_iota(jnp.int32, (GRID_W, LANES), 1) < HEAD_DIM
    n_keys = WIN_ROWS * GRID_W

    def row_body(i, carry):
        r = blk * ROWS_PER_BLOCK + i
        rs = jnp.clip(r - WIN_ROWS // 2, 0, n_rows - WIN_ROWS)
        variant = rs - r + (WIN_ROWS - 1)
        koff = pl.multiple_of((rs - win_row0) * GRID_W, GRID_W)
        qoff = pl.multiple_of(i * 2 * GRID_W, 2 * GRID_W)
        scores = []
        for p in range(N_PAIRS):
            cols = slice(p * LANES, (p + 1) * LANES)
            s = lax.dot_general(q_ref[pl.ds(qoff, 2 * GRID_W), cols], kwin[pl.ds(koff, n_keys), cols],
                                (((1,), (1,)), ((), ())), preferred_element_type=F32)
            scores.append(s + bias_ref[variant, p])
        for p in range(N_PAIRS):
            cols = slice(p * LANES, (p + 1) * LANES)
            s = scores[p]
            e = jnp.exp2(s - jnp.max(s, axis=-1, keepdims=True))
            denom = jnp.sum(e, axis=-1, keepdims=True)
            o2 = jnp.dot(e.astype(BF16), vwin[pl.ds(koff, n_keys), cols],
                         preferred_element_type=F32) * (1.0 / denom)
            out = jnp.where(out_even, o2[:GRID_W], o2[GRID_W:])
            o_ref[pl.ds(pl.multiple_of(i * GRID_W, GRID_W), GRID_W), cols] = out.astype(BF16)
        return carry

    lax.fori_loop(0, ROWS_PER_BLOCK, row_body, 0, unroll=NATTEN_UNROLL)


def _natten(q2, k, v, bias, blk_lo, blk_hi):
    T = k.shape[0]

    def win_map(b, lo, hi):
        return (jnp.clip(b - 1, lo[b], hi[b] - (WIN_BLOCKS - 1)) * TM, 0)

    window = pl.BlockSpec((pl.Element(WIN_BLOCKS * TM), pl.Element(HALF)), win_map)
    grid_spec = pltpu.PrefetchScalarGridSpec(
        num_scalar_prefetch=2,
        grid=(T // TM,),
        in_specs=[pl.BlockSpec((2 * TM, HALF), lambda b, lo, hi: (b, 0)), window, window,
                  pl.BlockSpec(bias.shape, lambda b, lo, hi: (0, 0, 0, 0),
                               pipeline_mode=pl.Buffered(1))],
        out_specs=pl.BlockSpec((TM, HALF), lambda b, lo, hi: (b, 0)),
    )
    return pl.pallas_call(
        _natten_kernel,
        grid_spec=grid_spec,
        out_shape=jax.ShapeDtypeStruct((T, HALF), BF16),
        compiler_params=_params(("parallel",)),
        name="natten",
    )(blk_lo, blk_hi, q2, k, v, bias)


def _na_bias_tables(rpb):
    n_layers = rpb.shape[0]
    qc = np.arange(GRID_W)
    kc = np.arange(GRID_W)
    cs = np.clip(qc - WIN_COLS // 2, 0, GRID_W - WIN_COLS)
    col_in = (kc[None, :] >= cs[:, None]) & (kc[None, :] < cs[:, None] + WIN_COLS)
    dc = np.clip(kc[None, :] - qc[:, None], -(WIN_COLS - 1), WIN_COLS - 1) + (WIN_COLS - 1)
    onehot = (dc.reshape(-1)[None, :] == np.arange(2 * WIN_COLS - 1)[:, None]).astype(np.float32)
    band = jnp.dot(rpb.astype(F32).reshape(-1, 2 * WIN_COLS - 1), onehot,
                   precision=lax.Precision.HIGHEST)
    band = band.reshape(n_layers, N_HEADS, 2 * WIN_ROWS - 1, GRID_W, GRID_W)
    band = jnp.where(col_in[None, None, None], band * LOG2_E, MASK_VALUE)
    band = jnp.transpose(band, (0, 1, 3, 2, 4))
    t = jnp.stack([band[:, :, :, o:o + WIN_ROWS, :] for o in range(WIN_ROWS)], axis=1)
    return t.reshape(n_layers, WIN_ROWS, N_PAIRS, 2 * GRID_W, WIN_ROWS * GRID_W)


def _mixout_kernel(*refs, n_x, first):
    a_ref, u_ref, sv_ref, sgw_ref, sgb_ref, g_ref, w_ref, o_ref, sg_s = refs[n_x:]
    even = lax.broadcasted_iota(jnp.int32, (SG_CHUNK, LANES), 1) < HEAD_DIM
    for c in range(TM // SG_CHUNK):
        rows = slice(c * SG_CHUNK, (c + 1) * SG_CHUNK)
        for p in range(N_PAIRS):
            cols = slice(p * LANES, (p + 1) * LANES)
            m2 = jnp.dot(sgw_ref[p], sv_ref[rows, cols], preferred_element_type=F32)
            mixed = jnp.where(even, m2[:SG_CHUNK], m2[SG_CHUNK:])
            sg_s[rows, cols] = u_ref[rows, cols].astype(F32) * (mixed + sgb_ref[:, cols])
    sg = sg_s[...]
    sgn = (sg * _rms_scale(sg) * g_ref[:, HALF:]).astype(BF16)
    a = a_ref[...].astype(F32)
    an = (a * _rms_scale(a) * g_ref[:, :HALF]).astype(BF16)
    y = jnp.dot(an, w_ref[:HALF, :], preferred_element_type=F32)
    y = y + jnp.dot(sgn, w_ref[HALF:, :], preferred_element_type=F32)
    o_ref[...] = _stream_tile(refs[:n_x], first) + y


def _mixout(attn, u, svn, xs, sgw2, sgb_t, g_out, w_out, layer):
    T = sum(x.shape[0] for x in xs)
    D = xs[0].shape[1]
    half = pl.BlockSpec((TM, HALF), lambda i: (i, 0))
    full = pl.BlockSpec((TM, D), lambda i: (i, 0))
    return pl.pallas_call(
        functools.partial(_mixout_kernel, n_x=len(xs), first=xs[0].shape[0] // TM),
        grid=(T // TM,),
        in_specs=_stream_specs(xs) + [
                  half, half, half, _const_spec(sgw2.shape), _const_spec(sgb_t.shape),
                  _const_spec((1, 2 * HALF)), _layer_spec(w_out, layer)],
        out_specs=full,
        out_shape=jax.ShapeDtypeStruct((T, D), F32),
        scratch_shapes=[pltpu.VMEM((TM, HALF), F32)],
        compiler_params=_params(("parallel",)),
        name="mixout",
    )(*xs, attn, u, svn, sgw2, sgb_t, g_out, w_out)


def _ff_chunks(d_ff):
    n = 1
    while d_ff // n > 2048 or d_ff % n or (d_ff // n) % LANES:
        n += 1
    return n


def _swiglu(h, wg_ref, wu_ref, wd_ref, n_chunks):
    d_ff = wd_ref.shape[-2]
    fc = d_ff // n_chunks
    y = None
    for c in range(n_chunks):
        cols = slice(c * fc, (c + 1) * fc)
        gate = jnp.dot(h, wg_ref[:, cols], preferred_element_type=F32)
        up = jnp.dot(h, wu_ref[:, cols], preferred_element_type=F32)
        act = (gate / (1.0 + jnp.exp(-gate)) * up).astype(BF16)
        part = jnp.dot(act, wd_ref[cols, :], preferred_element_type=F32)
        y = part if y is None else y + part
    return y


def _ffn_kernel(x_ref, g_ref, wg_ref, wu_ref, wd_ref, o_ref, *, n_chunks):
    x = x_ref[...]
    h = (x * _rms_scale(x) * g_ref[...]).astype(BF16)
    o_ref[...] = x + _swiglu(h, wg_ref, wu_ref, wd_ref, n_chunks)


def _ffn(x, g, wg, wu, wd, layer):
    T, D = x.shape
    full = pl.BlockSpec((TM, D), lambda i: (i, 0))
    return pl.pallas_call(
        functools.partial(_ffn_kernel, n_chunks=_ff_chunks(wg.shape[-1])),
        grid=(T // TM,),
        in_specs=[full, _const_spec((1, D)), _layer_spec(wg, layer), _layer_spec(wu, layer),
                  _layer_spec(wd, layer)],
        out_specs=full,
        out_shape=jax.ShapeDtypeStruct((T, D), F32),
        compiler_params=_params(("parallel",)),
        name="ffn",
    )(x, g, wg, wu, wd)


def _route_kernel(x_ref, g_ref, rw_ref, tri_ref, idx_ref, gate_ref, cnt_ref, carry):
    @pl.when(pl.program_id(0) == 0)
    def _():
        carry[...] = jnp.zeros_like(carry)

    x = x_ref[...]
    h = x * _rms_scale(x) * g_ref[...]
    h_hi = h.astype(BF16)
    h_lo = (h - h_hi.astype(F32)).astype(BF16)
    logits = (jnp.dot(h_hi, rw_ref[0], preferred_element_type=F32)
              + jnp.dot(h_lo, rw_ref[0], preferred_element_type=F32)
              + jnp.dot(h_hi, rw_ref[1], preferred_element_type=F32))
    col = lax.broadcasted_iota(jnp.int32, logits.shape, 1)
    colf = col.astype(F32)
    neg = -jnp.inf
    lg = jnp.where(col < N_EXPERTS, logits, neg)
    m1 = jnp.max(lg, axis=-1, keepdims=True)
    i1 = jnp.min(jnp.where(lg == m1, colf, float(LANES)), axis=-1, keepdims=True)
    sel1 = colf == i1
    lg2 = jnp.where(sel1, neg, lg)
    m2 = jnp.max(lg2, axis=-1, keepdims=True)
    i2 = jnp.min(jnp.where(lg2 == m2, colf, float(LANES)), axis=-1, keepdims=True)
    sel2 = colf == i2
    e2 = jnp.exp(m2 - m1)
    g1 = 1.0 / (1.0 + e2)
    g2 = e2 / (1.0 + e2)

    cnt = jnp.where(sel1 | sel2, 1.0, 0.0)
    before = jnp.dot(tri_ref[...], cnt.astype(BF16), preferred_element_type=F32) + carry[...]
    r1 = jnp.sum(jnp.where(sel1, before, 0.0), axis=-1, keepdims=True)
    r2 = jnp.sum(jnp.where(sel2, before, 0.0), axis=-1, keepdims=True)
    total = carry[...] + jnp.sum(cnt, axis=0, keepdims=True)
    carry[...] = total
    cnt_ref[...] = jnp.broadcast_to(total, cnt_ref.shape)

    meta = jnp.where(col == 0, i1, jnp.where(col == 1, i2, jnp.where(col == 2, r1, r2)))
    idx_ref[...] = meta.astype(jnp.int32)
    gate_ref[...] = jnp.where(col == 0, g1, g2)


def _route(x, g, rw_pad, tri):
    T, D = x.shape
    meta = pl.BlockSpec((TM, LANES), lambda i: (i, 0))
    return pl.pallas_call(
        _route_kernel,
        grid=(T // TM,),
        in_specs=[pl.BlockSpec((TM, D), lambda i: (i, 0)), _const_spec((1, D)),
                  _const_spec(rw_pad.shape), _const_spec(tri.shape)],
        out_specs=[meta, meta, pl.BlockSpec((8, LANES), lambda i: (0, 0))],
        out_shape=[jax.ShapeDtypeStruct((T, LANES), jnp.int32),
                   jax.ShapeDtypeStruct((T, LANES), F32),
                   jax.ShapeDtypeStruct((8, LANES), F32)],
        scratch_shapes=[pltpu.VMEM((1, LANES), F32)],
        compiler_params=_params(("arbitrary",)),
        name="route",
    )(x, g, rw_pad, tri)


def _experts_kernel(te_ref, src_ref, dst_ref, x_hbm, g_ref, wg_ref, wu_ref, wd_ref, ys_hbm,
                    xbuf, obuf, gsem, ssem, *, n_chunks):
    del te_ref
    i = pl.program_id(0)
    slot = i % 2

    def gather_wait(s):
        pltpu.make_async_copy(x_hbm.at[pl.ds(0, EXPERT_TILE)], xbuf.at[s], gsem.at[s]).wait()

    @pl.when(i == 0)
    def _():
        xbuf[...] = jnp.zeros_like(xbuf)
        obuf[...] = jnp.zeros_like(obuf)

    @pl.when(i >= 1)
    def _():
        gather_wait(1 - slot)

    x = xbuf[1 - slot]
    h = (x * _rms_scale(x) * g_ref[...]).astype(BF16)
    for t in range(EXPERT_TILE):
        pltpu.make_async_copy(x_hbm.at[pl.ds(src_ref[t], 1)], xbuf.at[slot, pl.ds(t, 1)],
                              gsem.at[slot]).start()
        pltpu.make_async_copy(obuf.at[slot, pl.ds(t, 1)], ys_hbm.at[pl.ds(dst_ref[t], 1)],
                              ssem).start()
    y = _swiglu(h, wg_ref.at[0], wu_ref.at[0], wd_ref.at[0], n_chunks)
    pltpu.make_async_copy(obuf.at[slot], ys_hbm.at[pl.ds(0, EXPERT_TILE)], ssem).wait()
    obuf[1 - slot] = y

    @pl.when(i == pl.num_programs(0) - 1)
    def _():
        gather_wait(slot)


def _experts(x, g, src_tok, dst_row, tile_expert, wg, wu, wd, layer):
    T, D = x.shape
    d_ff = wg.shape[-1]
    n_tiles = src_tok.shape[0] // EXPERT_TILE
    n_rows = dst_row.shape[0]

    def w_spec(w):
        return pl.BlockSpec((None, 1) + w.shape[2:],
                            lambda i, te: (layer, te[jnp.clip(i - 1, 0, n_tiles - 1)], 0, 0),
                            pipeline_mode=pl.Buffered(1))

    grid_spec = pltpu.PrefetchScalarGridSpec(
        num_scalar_prefetch=1,
        grid=(n_tiles + 2,),
        in_specs=[pl.BlockSpec((EXPERT_TILE,), lambda i, te: (jnp.minimum(i, n_tiles - 1),),
                               memory_space=pltpu.SMEM),
                  pl.BlockSpec((EXPERT_TILE,), lambda i, te: (jnp.maximum(i - 1, 0),),
                               memory_space=pltpu.SMEM),
                  pl.BlockSpec(memory_space=pl.ANY),
                  pl.BlockSpec((1, D), lambda i, te: (0, 0), pipeline_mode=pl.Buffered(1)),
                  w_spec(wg), w_spec(wu), w_spec(wd)],
        out_specs=pl.BlockSpec(memory_space=pl.ANY),
        scratch_shapes=[pltpu.VMEM((2, EXPERT_TILE, D), F32), pltpu.VMEM((2, EXPERT_TILE, D), F32),
                        pltpu.SemaphoreType.DMA((2,)), pltpu.SemaphoreType.DMA(())],
    )
    return pl.pallas_call(
        functools.partial(_experts_kernel, n_chunks=_ff_chunks(d_ff)),
        grid_spec=grid_spec,
        out_shape=jax.ShapeDtypeStruct((n_rows, D), F32),
        compiler_params=_params(("arbitrary",)),
        name="experts",
    )(tile_expert, src_tok, dst_row, x, g, wg, wu, wd)


def _combine_kernel(x_ref, gate_ref, g_ref, y0_ref, y1_ref, o_ref, *, final_norm):
    gates = gate_ref[...]
    y = x_ref[...] + (y0_ref[...] * gates[:, 0:1] + y1_ref[...] * gates[:, 1:2])
    if final_norm:
        y = y * _rms_scale(y) * g_ref[...]
    o_ref[...] = y


def _combine(x, gates, ys, g_final, final_norm, first_block, n_blocks):
    T, D = x.shape
    rows = lambda i: (i + first_block, 0)
    return pl.pallas_call(
        functools.partial(_combine_kernel, final_norm=final_norm),
        grid=(n_blocks,),
        in_specs=[pl.BlockSpec((TM, D), rows), pl.BlockSpec((TM, LANES), rows), _const_spec((1, D)),
                  pl.BlockSpec((TM, D), rows),
                  pl.BlockSpec((TM, D), lambda i: (i + first_block + T // TM, 0))],
        out_specs=pl.BlockSpec((TM, D), lambda i: (i, 0)),
        out_shape=jax.ShapeDtypeStruct((n_blocks * TM, D), F32),
        compiler_params=_params(("parallel",)),
        name="combine",
    )(x, gates, g_final, ys, ys)


def _final_norm_kernel(x_ref, g_ref, o_ref):
    x = x_ref[...]
    o_ref[...] = x * _rms_scale(x) * g_ref[...]


def _final_norm(x, g):
    T, D = x.shape
    full = pl.BlockSpec((TM, D), lambda i: (i, 0))
    return pl.pallas_call(
        _final_norm_kernel,
        grid=(T // TM,),
        in_specs=[full, _const_spec((1, D))],
        out_specs=full,
        out_shape=jax.ShapeDtypeStruct((T, D), F32),
        compiler_params=_params(("parallel",)),
        name="final_norm",
    )(x, g)


def _moe(x, g_ffn, router_w, wg, wu, wd, layer, g_final, final_norm, out_blocks):
    T, D = x.shape
    rw_pad = jnp.zeros((D, LANES), F32).at[:, :N_EXPERTS].set(router_w)
    rw_hi = rw_pad.astype(BF16)
    rw_pad = jnp.stack([rw_hi, (rw_pad - rw_hi.astype(F32)).astype(BF16)])
    tri = jnp.asarray(np.tril(np.ones((TM, TM), np.float32), -1), BF16)
    idx, gates, cnt = _route(x, g_ffn, rw_pad, tri)

    counts = cnt[0, :N_EXPERTS].astype(jnp.int32)
    padded = (counts + EXPERT_TILE - 1) // EXPERT_TILE * EXPERT_TILE
    pad_end = jnp.cumsum(padded)
    pad_start = pad_end - padded
    dest = (pad_start[idx[:, 0:2]] + idx[:, 2:4]).reshape(2 * T)
    n_rows = 2 * T + N_EXPERTS * EXPERT_TILE
    n_tiles = n_rows // EXPERT_TILE
    tile_expert = jnp.minimum(
        jnp.searchsorted(pad_end, jnp.arange(n_tiles, dtype=jnp.int32) * EXPERT_TILE, side="right"),
        N_EXPERTS - 1).astype(jnp.int32)

    assign = jnp.arange(2 * T, dtype=jnp.int32)
    out_of_assign = (assign % 2) * T + assign // 2
    filled = jnp.zeros((n_rows,), jnp.int32).at[dest].set(
        out_of_assign + 1, unique_indices=True, mode="promise_in_bounds")
    is_pad = filled == 0
    pad_rank = jnp.cumsum(is_pad.astype(jnp.int32)) - 1
    src_tok = jnp.where(is_pad, 0, (filled - 1) % T)
    dst_row = jnp.where(is_pad, 2 * T + pad_rank, filled - 1)
    spare = n_rows + jnp.arange(EXPERT_TILE, dtype=jnp.int32)
    dst_row = jnp.concatenate([spare, dst_row])

    ys = _experts(x, g_ffn, src_tok, dst_row, tile_expert, wg, wu, wd, layer)
    if out_blocks is None:
        return ys, gates
    return [_combine(x, gates, ys, g_final, final_norm, b0, nb) for b0, nb in out_blocks]


def _image_blocks(image_rows):
    lo, hi, start = [], [], 0
    for rows in image_rows:
        n = rows // ROWS_PER_BLOCK
        lo += [start] * n
        hi += [start + n - 1] * n
        start += n
    return jnp.asarray(lo, jnp.int32), jnp.asarray(hi, jnp.int32)


def kernel(x_prompt, x_sample, norm_mix_g, w_in, na_rpb, sg_norm_g, sg_w, sg_b, out_norm_g, w_out,
           norm_ffn_g, dense_w_gate, dense_w_up, dense_w_down,
           router_w, moe_w_gate, moe_w_up, moe_w_down, final_norm_g):
    D = x_prompt.shape[-1]
    depth = w_in.shape[0]
    assert D == 2 * HALF
    image_rows = []
    for xs in (x_prompt, x_sample):
        assert xs.shape[1] % TM == 0 and xs.shape[1] >= WIN_BLOCKS * TM
        image_rows += [xs.shape[1] // GRID_W] * xs.shape[0]
    blk_lo, blk_hi = _image_blocks(image_rows)
    n_prompt = x_prompt.shape[0] * x_prompt.shape[1]

    xs = (x_prompt.reshape(-1, D), x_sample.reshape(-1, D))
    bd = jnp.asarray(np.kron(np.eye(N_HEADS, dtype=np.float32),
                             np.ones((HEAD_DIM, HEAD_DIM), np.float32)), BF16)
    g_final = final_norm_g.reshape(1, D)
    na_bias = _na_bias_tables(na_rpb)

    w_in, w_out, dense_w_gate, dense_w_up, dense_w_down, moe_w_gate, moe_w_up, moe_w_down = (
        w.astype(BF16) for w in (w_in, w_out, dense_w_gate, dense_w_up, dense_w_down,
                                 moe_w_gate, moe_w_up, moe_w_down))

    pending_moe = None
    for l in range(depth):
        q, k, v, u, svn, *x_new = _inproj(xs, norm_mix_g[l].reshape(1, D), w_in, l,
                                          sg_norm_g[l].reshape(1, HALF), bd, pending_moe)
        if pending_moe is not None:
            xs, pending_moe = (x_new[0],), None
        attn = _natten(q, k, v, na_bias[l], blk_lo, blk_hi)
        sgw2 = sg_w[l].astype(BF16).reshape(N_PAIRS, 2 * SG_CHUNK, SG_CHUNK)
        sgb_t = jnp.repeat(jnp.transpose(sg_b[l]), HEAD_DIM, axis=1)
        x = _mixout(attn, u, svn, xs, sgw2, sgb_t, out_norm_g[l].reshape(1, D), w_out, l)
        g_ffn = norm_ffn_g[l].reshape(1, D)
        i = l // 2
        last = l == depth - 1
        if l % 2 == 0:
            x = _ffn(x, g_ffn, dense_w_gate, dense_w_up, dense_w_down, i)
            if last:
                x = _final_norm(x, g_final)
                outs = [x[:n_prompt], x[n_prompt:]]
        elif last:
            nb, npb = x.shape[0] // TM, n_prompt // TM
            outs = _moe(x, g_ffn, router_w[i], moe_w_gate, moe_w_up, moe_w_down, i, g_final,
                        final_norm=True, out_blocks=[(0, npb), (npb, nb - npb)])
        else:
            pending_moe = _moe(x, g_ffn, router_w[i], moe_w_gate, moe_w_up, moe_w_down, i, g_final,
                               final_norm=False, out_blocks=None)
        xs = (x,)
    return (outs[0].reshape(x_prompt.shape), outs[1].reshape(x_sample.shape))
```

```python
import functools

import numpy as np
import jax
import jax.numpy as jnp
from jax import lax
from jax.experimental import pallas as pl
from jax.experimental.pallas import tpu as pltpu

F32 = jnp.float32
BF16 = jnp.bfloat16

GRID_W = 64
WIN_ROWS = 8
WIN_COLS = 16
N_HEADS = 8
HEAD_DIM = 64
HALF = N_HEADS * HEAD_DIM
N_PAIRS = N_HEADS // 2
LANES = 128
SG_CHUNK = 128
N_EXPERTS = 8
RMS_EPS = 1e-6
MASK_VALUE = -1e30

TM = 512
ROWS_PER_BLOCK = TM // GRID_W
WIN_BLOCKS = 3
LOG2_E = 1.4426950408889634
NATTEN_UNROLL = ROWS_PER_BLOCK
EXPERT_TILE = 512
INVERT_CHUNK = 2048
VMEM_LIMIT = 56 * 1024 * 1024


def _params(sem, vmem=VMEM_LIMIT):
    return pltpu.CompilerParams(dimension_semantics=sem, vmem_limit_bytes=vmem)


def _const_spec(shape):
    nd = len(shape)
    return pl.BlockSpec(shape, lambda *_: (0,) * nd, pipeline_mode=pl.Buffered(1))


def _layer_spec(w, layer):
    nd = w.ndim - 1
    return pl.BlockSpec((None,) + w.shape[1:], lambda *_: (layer,) + (0,) * nd,
                        pipeline_mode=pl.Buffered(1))


def _rms_scale(x):
    return lax.rsqrt(jnp.mean(x * x, axis=-1, keepdims=True) + RMS_EPS)


def _stream_specs(parts):
    D = parts[0].shape[1]
    if len(parts) == 1:
        return [pl.BlockSpec((TM, D), lambda i: (i, 0))]
    first = parts[0].shape[0] // TM
    return [pl.BlockSpec((TM, D), lambda i: (jnp.minimum(i, first - 1), 0)),
            pl.BlockSpec((TM, D), lambda i: (jnp.maximum(i - first, 0), 0))]


def _stream_tile(refs, first):
    if len(refs) == 1:
        return refs[0][...]
    return jnp.where(pl.program_id(0) < first, refs[0][...], refs[1][...])


def _gelu_tanh(x):
    return x * (0.5 * (1.0 + jnp.tanh(0.7978845608028654 * (x + 0.044715 * (x * x * x)))))


def _inproj_kernel(*refs, n_x, first, pending_moe):
    x = _stream_tile(refs[:n_x], first)
    if pending_moe:
        y0_ref, y1_ref, gate_ref = refs[n_x:n_x + 3]
        g_ref, w_ref, sgg_ref, bd_ref, q_ref, k_ref, v_ref, u_ref, sv_ref, x_out_ref = refs[n_x + 3:]
        gates = gate_ref[...]
        x = x + (y0_ref[...] * gates[:, 0:1] + y1_ref[...] * gates[:, 1:2])
        x_out_ref[...] = x
    else:
        g_ref, w_ref, sgg_ref, bd_ref, q_ref, k_ref, v_ref, u_ref, sv_ref = refs[n_x:]
    h = (x * _rms_scale(x) * g_ref[...]).astype(BF16)

    def proj(n):
        return jnp.dot(h, w_ref[:, n * HALF:(n + 1) * HALF], preferred_element_type=F32)

    q = proj(0) * (HEAD_DIM ** -0.5 * LOG2_E)
    is_even = (lax.broadcasted_iota(jnp.int32, q.shape, 1) % LANES) < HEAD_DIM
    qe = jnp.where(is_even, q, 0.0).astype(BF16)
    qo = jnp.where(is_even, 0.0, q).astype(BF16)
    for r in range(ROWS_PER_BLOCK):
        rows = slice(r * GRID_W, (r + 1) * GRID_W)
        q_ref[2 * r * GRID_W:(2 * r + 1) * GRID_W, :] = qe[rows]
        q_ref[(2 * r + 1) * GRID_W:(2 * r + 2) * GRID_W, :] = qo[rows]
    k_ref[...] = proj(1).astype(BF16)
    v_ref[...] = proj(2).astype(BF16)
    u_ref[...] = _gelu_tanh(proj(3)).astype(BF16)
    sv = _gelu_tanh(proj(4))
    ssq = jnp.dot((sv * sv).astype(BF16), bd_ref[...], preferred_element_type=F32)
    sv_ref[...] = (sv * lax.rsqrt(ssq * (1.0 / HEAD_DIM) + RMS_EPS) * sgg_ref[...]).astype(BF16)


def _inproj(xs, g, w_in, layer, sgg, bd, pending_moe=None):
    T = sum(x.shape[0] for x in xs)
    D = xs[0].shape[1]
    out = jax.ShapeDtypeStruct((T, HALF), BF16)
    tile = pl.BlockSpec((TM, HALF), lambda i: (i, 0))
    full = pl.BlockSpec((TM, D), lambda i: (i, 0))
    moe_args, moe_specs, moe_out_specs, moe_out_shape = [], [], [], []
    if pending_moe is not None:
        ys, gates = pending_moe
        moe_args = [ys, ys, gates]
        moe_specs = [full, pl.BlockSpec((TM, D), lambda i: (i + T // TM, 0)),
                     pl.BlockSpec((TM, LANES), lambda i: (i, 0))]
        moe_out_specs = [full]
        moe_out_shape = [jax.ShapeDtypeStruct((T, D), F32)]
    return pl.pallas_call(
        functools.partial(_inproj_kernel, n_x=len(xs), first=xs[0].shape[0] // TM,
                          pending_moe=pending_moe is not None),
        grid=(T // TM,),
        in_specs=_stream_specs(xs) + moe_specs + [
            _const_spec((1, D)), _layer_spec(w_in, layer),
            _const_spec((1, HALF)), _const_spec((HALF, HALF))],
        out_specs=[pl.BlockSpec((2 * TM, HALF), lambda i: (i, 0))] + [tile] * 4 + moe_out_specs,
        out_shape=[jax.ShapeDtypeStruct((2 * T, HALF), BF16)] + [out] * 4 + moe_out_shape,
        compiler_params=_params(("parallel",)),
        name="inproj",
    )(*xs, *moe_args, g, w_in, sgg, bd)


def _natten_kernel(lo_ref, hi_ref, q_ref, kwin, vwin, bias_ref, o_ref):
    b = pl.program_id(0)
    first = lo_ref[b]
    blk = b - first
    n_blk = hi_ref[b] - first + 1
    n_rows = n_blk * ROWS_PER_BLOCK
    win_row0 = jnp.clip(blk - 1, 0, n_blk - WIN_BLOCKS) * ROWS_PER_BLOCK
    out_even = lax.broadcasted_iota(jnp.int32, (GRID_W, LANES), 1) < HEAD_DIM
    n_keys = WIN_ROWS * GRID_W

    def row_body(i, carry):
        r = blk * ROWS_PER_BLOCK + i
        rs = jnp.clip(r - WIN_ROWS // 2, 0, n_rows - WIN_ROWS)
        variant = rs - r + (WIN_ROWS - 1)
        koff = pl.multiple_of((rs - win_row0) * GRID_W, GRID_W)
        qoff = pl.multiple_of(i * 2 * GRID_W, 2 * GRID_W)
        scores = []
        for p in range(N_PAIRS):
            cols = slice(p * LANES, (p + 1) * LANES)
            s = lax.dot_general(q_ref[pl.ds(qoff, 2 * GRID_W), cols], kwin[pl.ds(koff, n_keys), cols],
                                (((1,), (1,)), ((), ())), preferred_element_type=F32)
            scores.append(s + bias_ref[variant, p])
        for p in range(N_PAIRS):
            cols = slice(p * LANES, (p + 1) * LANES)
            s = scores[p]
            e = jnp.exp2(s - jnp.max(s, axis=-1, keepdims=True))
            denom = jnp.sum(e, axis=-1, keepdims=True)
            o2 = jnp.dot(e.astype(BF16), vwin[pl.ds(koff, n_keys), cols],
                         preferred_element_type=F32) * (1.0 / denom)
            out = jnp.where(out_even, o2[:GRID_W], o2[GRID_W:])
            o_ref[pl.ds(pl.multiple_of(i * GRID_W, GRID_W), GRID_W), cols] = out.astype(BF16)
        return carry

    lax.fori_loop(0, ROWS_PER_BLOCK, row_body, 0, unroll=NATTEN_UNROLL)


def _natten(q2, k, v, bias, blk_lo, blk_hi):
    T = k.shape[0]

    def win_map(b, lo, hi):
        return (jnp.clip(b - 1, lo[b], hi[b] - (WIN_BLOCKS - 1)) * TM, 0)

    window = pl.BlockSpec((pl.Element(WIN_BLOCKS * TM), pl.Element(HALF)), win_map)
    grid_spec = pltpu.PrefetchScalarGridSpec(
        num_scalar_prefetch=2,
        grid=(T // TM,),
        in_specs=[pl.BlockSpec((2 * TM, HALF), lambda b, lo, hi: (b, 0)), window, window,
                  pl.BlockSpec(bias.shape, lambda b, lo, hi: (0, 0, 0, 0),
                               pipeline_mode=pl.Buffered(1))],
        out_specs=pl.BlockSpec((TM, HALF), lambda b, lo, hi: (b, 0)),
    )
    return pl.pallas_call(
        _natten_kernel,
        grid_spec=grid_spec,
        out_shape=jax.ShapeDtypeStruct((T, HALF), BF16),
        compiler_params=_params(("parallel",)),
        name="natten",
    )(blk_lo, blk_hi, q2, k, v, bias)


def _na_bias_tables(rpb):
    n_layers = rpb.shape[0]
    qc = np.arange(GRID_W)
    kc = np.arange(GRID_W)
    cs = np.clip(qc - WIN_COLS // 2, 0, GRID_W - WIN_COLS)
    col_in = (kc[None, :] >= cs[:, None]) & (kc[None, :] < cs[:, None] + WIN_COLS)
    dc = np.clip(kc[None, :] - qc[:, None], -(WIN_COLS - 1), WIN_COLS - 1) + (WIN_COLS - 1)
    onehot = (dc.reshape(-1)[None, :] == np.arange(2 * WIN_COLS - 1)[:, None]).astype(np.float32)
    band = jnp.dot(rpb.astype(F32).reshape(-1, 2 * WIN_COLS - 1), onehot,
                   precision=lax.Precision.HIGHEST)
    band = band.reshape(n_layers, N_HEADS, 2 * WIN_ROWS - 1, GRID_W, GRID_W)
    band = jnp.where(col_in[None, None, None], band * LOG2_E, MASK_VALUE)
    band = jnp.transpose(band, (0, 1, 3, 2, 4))
    t = jnp.stack([band[:, :, :, o:o + WIN_ROWS, :] for o in range(WIN_ROWS)], axis=1)
    return t.reshape(n_layers, WIN_ROWS, N_PAIRS, 2 * GRID_W, WIN_ROWS * GRID_W)


def _mixout_kernel(*refs, n_x, first, with_route):
    if with_route:
        (a_ref, u_ref, sv_ref, sgw_ref, sgb_ref, g_ref, w_ref, gffn_ref, rw_ref, tri_ref,
         o_ref, idx_ref, gate_ref, cnt_ref, sg_s, carry) = refs[n_x:]
    else:
        a_ref, u_ref, sv_ref, sgw_ref, sgb_ref, g_ref, w_ref, o_ref, sg_s = refs[n_x:]
    even = lax.broadcasted_iota(jnp.int32, (SG_CHUNK, LANES), 1) < HEAD_DIM
    for c in range(TM // SG_CHUNK):
        rows = slice(c * SG_CHUNK, (c + 1) * SG_CHUNK)
        for p in range(N_PAIRS):
            cols = slice(p * LANES, (p + 1) * LANES)
            m2 = jnp.dot(sgw_ref[p], sv_ref[rows, cols], preferred_element_type=F32)
            mixed = jnp.where(even, m2[:SG_CHUNK], m2[SG_CHUNK:])
            sg_s[rows, cols] = u_ref[rows, cols].astype(F32) * (mixed + sgb_ref[:, cols])
    sg = sg_s[...]
    sgn = (sg * _rms_scale(sg) * g_ref[:, HALF:]).astype(BF16)
    a = a_ref[...].astype(F32)
    an = (a * _rms_scale(a) * g_ref[:, :HALF]).astype(BF16)
    y = jnp.dot(an, w_ref[:HALF, :], preferred_element_type=F32)
    y = y + jnp.dot(sgn, w_ref[HALF:, :], preferred_element_type=F32)
    x = _stream_tile(refs[:n_x], first) + y
    o_ref[...] = x
    if with_route:
        @pl.when(pl.program_id(0) == 0)
        def _():
            carry[...] = jnp.zeros_like(carry)

        _route_tile(x, gffn_ref, rw_ref, tri_ref, carry, idx_ref, gate_ref, cnt_ref)


def _route_tile(x, g_ref, rw_ref, tri_ref, carry, idx_ref, gate_ref, cnt_ref):
    h = x * _rms_scale(x) * g_ref[...]
    h_hi = h.astype(BF16)
    h_lo = (h - h_hi.astype(F32)).astype(BF16)
    prod = jnp.dot(jnp.concatenate([h_hi, h_lo], axis=0), rw_ref[...], preferred_element_type=F32)
    logits = (prod[:TM, :LANES] + prod[:TM, LANES:]) + (prod[TM:, :LANES] + prod[TM:, LANES:])
    col = lax.broadcasted_iota(jnp.int32, logits.shape, 1)
    colf = col.astype(F32)
    neg = -jnp.inf
    lg = jnp.where(col < N_EXPERTS, logits, neg)
    m1 = jnp.max(lg, axis=-1, keepdims=True)
    i1 = jnp.min(jnp.where(lg == m1, colf, float(LANES)), axis=-1, keepdims=True)
    sel1 = colf == i1
    lg2 = jnp.where(sel1, neg, lg)
    m2 = jnp.max(lg2, axis=-1, keepdims=True)
    i2 = jnp.min(jnp.where(lg2 == m2, colf, float(LANES)), axis=-1, keepdims=True)
    sel2 = colf == i2
    e2 = jnp.exp(m2 - m1)
    g1 = 1.0 / (1.0 + e2)
    g2 = e2 / (1.0 + e2)

    cnt = jnp.where(sel1 | sel2, 1.0, 0.0)
    before = jnp.dot(tri_ref[...], cnt.astype(BF16), preferred_element_type=F32) + carry[...]
    r1 = jnp.sum(jnp.where(sel1, before, 0.0), axis=-1, keepdims=True)
    r2 = jnp.sum(jnp.where(sel2, before, 0.0), axis=-1, keepdims=True)
    total = carry[...] + jnp.sum(cnt, axis=0, keepdims=True)
    carry[...] = total
    cnt_ref[...] = jnp.broadcast_to(total, cnt_ref.shape)

    meta = jnp.where(col == 0, i1, jnp.where(col == 1, i2, jnp.where(col == 2, r1, r2)))
    idx_ref[...] = meta.astype(jnp.int32)
    gate_ref[...] = jnp.where(col == 0, g1, g2)


def _mixout(attn, u, svn, xs, sgw2, sgb_t, g_out, w_out, layer, route=None):
    T = sum(x.shape[0] for x in xs)
    D = xs[0].shape[1]
    half = pl.BlockSpec((TM, HALF), lambda i: (i, 0))
    full = pl.BlockSpec((TM, D), lambda i: (i, 0))
    meta = pl.BlockSpec((TM, LANES), lambda i: (i, 0))
    route_args = list(route) if route else []
    return pl.pallas_call(
        functools.partial(_mixout_kernel, n_x=len(xs), first=xs[0].shape[0] // TM,
                          with_route=bool(route)),
        grid=(T // TM,),
        in_specs=_stream_specs(xs) + [
            half, half, half, _const_spec(sgw2.shape), _const_spec(sgb_t.shape),
            _const_spec((1, 2 * HALF)), _layer_spec(w_out, layer)
        ] + [_const_spec(a.shape) for a in route_args],
        out_specs=[full] + ([meta, meta, pl.BlockSpec((8, LANES), lambda i: (0, 0))] if route else []),
        out_shape=[jax.ShapeDtypeStruct((T, D), F32)] + (
            [jax.ShapeDtypeStruct((T, LANES), jnp.int32), jax.ShapeDtypeStruct((T, LANES), F32),
             jax.ShapeDtypeStruct((8, LANES), F32)] if route else []),
        scratch_shapes=[pltpu.VMEM((TM, HALF), F32)] + (
            [pltpu.VMEM((1, LANES), F32)] if route else []),
        compiler_params=_params(("arbitrary",) if route else ("parallel",)),
        name="mixout",
    )(*xs, attn, u, svn, sgw2, sgb_t, g_out, w_out, *route_args)


def _ff_chunks(d_ff):
    n = 1
    while d_ff // n > 2048 or d_ff % n or (d_ff // n) % LANES:
        n += 1
    return n


def _swiglu(h, wg_ref, wu_ref, wd_ref, n_chunks):
    d_ff = wd_ref.shape[-2]
    fc = d_ff // n_chunks
    y = None
    for c in range(n_chunks):
        cols = slice(c * fc, (c + 1) * fc)
        gate = jnp.dot(h, wg_ref[:, cols], preferred_element_type=F32)
        up = jnp.dot(h, wu_ref[:, cols], preferred_element_type=F32)
        act = (gate / (1.0 + jnp.exp(-gate)) * up).astype(BF16)
        part = jnp.dot(act, wd_ref[cols, :], preferred_element_type=F32)
        y = part if y is None else y + part
    return y


def _ffn_kernel(x_ref, g_ref, wg_ref, wu_ref, wd_ref, o_ref, *, n_chunks):
    x = x_ref[...]
    h = (x * _rms_scale(x) * g_ref[...]).astype(BF16)
    o_ref[...] = x + _swiglu(h, wg_ref, wu_ref, wd_ref, n_chunks)


def _ffn(x, g, wg, wu, wd, layer):
    T, D = x.shape
    full = pl.BlockSpec((TM, D), lambda i: (i, 0))
    return pl.pallas_call(
        functools.partial(_ffn_kernel, n_chunks=_ff_chunks(wg.shape[-1])),
        grid=(T // TM,),
        in_specs=[full, _const_spec((1, D)), _layer_spec(wg, layer), _layer_spec(wu, layer),
                  _layer_spec(wd, layer)],
        out_specs=full,
        out_shape=jax.ShapeDtypeStruct((T, D), F32),
        compiler_params=_params(("parallel",)),
        name="ffn",
    )(x, g, wg, wu, wd)


def _invert_kernel(lo_ref, hi_ref, dest_ref, out_ref):
    i = pl.program_id(0)

    @pl.when(i == 0)
    def _():
        def clear(j, carry):
            out_ref[j] = 0
            return carry

        for r in range(N_EXPERTS + 1):
            lax.fori_loop(lo_ref[r], hi_ref[r], clear, 0)

    first = i * INVERT_CHUNK + 1

    def place(a, carry):
        out_ref[dest_ref[a]] = first + a
        return carry

    lax.fori_loop(0, INVERT_CHUNK, place, 0, unroll=8)


def _invert(dest, empty_lo, empty_hi, n_rows):
    n = dest.shape[0]
    assert n % INVERT_CHUNK == 0
    grid_spec = pltpu.PrefetchScalarGridSpec(
        num_scalar_prefetch=2,
        grid=(n // INVERT_CHUNK,),
        in_specs=[pl.BlockSpec((INVERT_CHUNK,), lambda i, lo, hi: (i,), memory_space=pltpu.SMEM)],
        out_specs=pl.BlockSpec((n_rows,), lambda i, lo, hi: (0,), memory_space=pltpu.SMEM),
    )
    return pl.pallas_call(
        _invert_kernel,
        grid_spec=grid_spec,
        out_shape=jax.ShapeDtypeStruct((n_rows,), jnp.int32),
        compiler_params=_params(("arbitrary",)),
        name="invert",
    )(empty_lo, empty_hi, dest)


def _experts_kernel(te_ref, src_ref, dst_ref, x_hbm, g_ref, wg_ref, wu_ref, wd_ref, ys_hbm,
                    xbuf, obuf, gsem, ssem, *, n_chunks):
    del te_ref
    i = pl.program_id(0)
    slot = i % 2

    def gather_wait(s):
        pltpu.make_async_copy(x_hbm.at[pl.ds(0, EXPERT_TILE)], xbuf.at[s], gsem.at[s]).wait()

    @pl.when(i == 0)
    def _():
        xbuf[...] = jnp.zeros_like(xbuf)
        obuf[...] = jnp.zeros_like(obuf)

    @pl.when(i >= 1)
    def _():
        gather_wait(1 - slot)

    x = xbuf[1 - slot]
    h = (x * _rms_scale(x) * g_ref[...]).astype(BF16)
    for t in range(EXPERT_TILE):
        pltpu.make_async_copy(x_hbm.at[pl.ds(src_ref[t], 1)], xbuf.at[slot, pl.ds(t, 1)],
                              gsem.at[slot]).start()
        pltpu.make_async_copy(obuf.at[slot, pl.ds(t, 1)], ys_hbm.at[pl.ds(dst_ref[t], 1)],
                              ssem).start()
    y = _swiglu(h, wg_ref.at[0], wu_ref.at[0], wd_ref.at[0], n_chunks)
    pltpu.make_async_copy(obuf.at[slot], ys_hbm.at[pl.ds(0, EXPERT_TILE)], ssem).wait()
    obuf[1 - slot] = y

    @pl.when(i == pl.num_programs(0) - 1)
    def _():
        gather_wait(slot)


def _experts(x, g, src_tok, dst_row, tile_expert, wg, wu, wd, layer):
    T, D = x.shape
    d_ff = wg.shape[-1]
    n_tiles = src_tok.shape[0] // EXPERT_TILE
    n_rows = dst_row.shape[0]

    def w_spec(w):
        return pl.BlockSpec((None, 1) + w.shape[2:],
                            lambda i, te: (layer, te[jnp.clip(i - 1, 0, n_tiles - 1)], 0, 0),
                            pipeline_mode=pl.Buffered(1))

    grid_spec = pltpu.PrefetchScalarGridSpec(
        num_scalar_prefetch=1,
        grid=(n_tiles + 2,),
        in_specs=[pl.BlockSpec((EXPERT_TILE,), lambda i, te: (jnp.minimum(i, n_tiles - 1),),
                               memory_space=pltpu.SMEM),
                  pl.BlockSpec((EXPERT_TILE,), lambda i, te: (jnp.maximum(i - 1, 0),),
                               memory_space=pltpu.SMEM),
                  pl.BlockSpec(memory_space=pl.ANY),
                  pl.BlockSpec((1, D), lambda i, te: (0, 0), pipeline_mode=pl.Buffered(1)),
                  w_spec(wg), w_spec(wu), w_spec(wd)],
        out_specs=pl.BlockSpec(memory_space=pl.ANY),
        scratch_shapes=[pltpu.VMEM((2, EXPERT_TILE, D), F32), pltpu.VMEM((2, EXPERT_TILE, D), F32),
                        pltpu.SemaphoreType.DMA((2,)), pltpu.SemaphoreType.DMA(())],
    )
    return pl.pallas_call(
        functools.partial(_experts_kernel, n_chunks=_ff_chunks(d_ff)),
        grid_spec=grid_spec,
        out_shape=jax.ShapeDtypeStruct((n_rows, D), F32),
        compiler_params=_params(("arbitrary",)),
        name="experts",
    )(tile_expert, src_tok, dst_row, x, g, wg, wu, wd)


def _combine_kernel(x_ref, gate_ref, g_ref, y0_ref, y1_ref, o_ref, *, final_norm):
    gates = gate_ref[...]
    y = x_ref[...] + (y0_ref[...] * gates[:, 0:1] + y1_ref[...] * gates[:, 1:2])
    if final_norm:
        y = y * _rms_scale(y) * g_ref[...]
    o_ref[...] = y


def _combine(x, gates, ys, g_final, final_norm, first_block, n_blocks):
    T, D = x.shape
    rows = lambda i: (i + first_block, 0)
    return pl.pallas_call(
        functools.partial(_combine_kernel, final_norm=final_norm),
        grid=(n_blocks,),
        in_specs=[pl.BlockSpec((TM, D), rows), pl.BlockSpec((TM, LANES), rows), _const_spec((1, D)),
                  pl.BlockSpec((TM, D), rows),
                  pl.BlockSpec((TM, D), lambda i: (i + first_block + T // TM, 0))],
        out_specs=pl.BlockSpec((TM, D), lambda i: (i, 0)),
        out_shape=jax.ShapeDtypeStruct((n_blocks * TM, D), F32),
        compiler_params=_params(("parallel",)),
        name="combine",
    )(x, gates, g_final, ys, ys)


def _final_norm_kernel(x_ref, g_ref, o_ref):
    x = x_ref[...]
    o_ref[...] = x * _rms_scale(x) * g_ref[...]


def _final_norm(x, g):
    T, D = x.shape
    full = pl.BlockSpec((TM, D), lambda i: (i, 0))
    return pl.pallas_call(
        _final_norm_kernel,
        grid=(T // TM,),
        in_specs=[full, _const_spec((1, D))],
        out_specs=full,
        out_shape=jax.ShapeDtypeStruct((T, D), F32),
        compiler_params=_params(("parallel",)),
        name="final_norm",
    )(x, g)


def _route_operands(g_ffn, router_w):
    D = router_w.shape[0]
    rw_pad = jnp.zeros((D, LANES), F32).at[:, :N_EXPERTS].set(router_w)
    rw_hi = rw_pad.astype(BF16)
    rw2 = jnp.concatenate([rw_hi, (rw_pad - rw_hi.astype(F32)).astype(BF16)], axis=1)
    tri = jnp.asarray(np.tril(np.ones((TM, TM), np.float32), -1), BF16)
    return g_ffn, rw2, tri


def _moe(x, routing, g_ffn, wg, wu, wd, layer, g_final, final_norm, out_blocks):
    T, D = x.shape
    idx, gates, cnt = routing

    counts = cnt[0, :N_EXPERTS].astype(jnp.int32)
    padded = (counts + EXPERT_TILE - 1) // EXPERT_TILE * EXPERT_TILE
    pad_end = jnp.cumsum(padded)
    pad_start = pad_end - padded
    dest = (pad_start[idx[:, 0:2]] + idx[:, 2:4]).reshape(2 * T)
    n_rows = 2 * T + N_EXPERTS * EXPERT_TILE
    n_tiles = n_rows // EXPERT_TILE
    tile_expert = jnp.minimum(
        jnp.searchsorted(pad_end, jnp.arange(n_tiles, dtype=jnp.int32) * EXPERT_TILE, side="right"),
        N_EXPERTS - 1).astype(jnp.int32)

    empty_lo = jnp.concatenate([pad_start + counts, pad_end[-1:]]).astype(jnp.int32)
    empty_hi = jnp.concatenate([pad_end, jnp.full((1,), n_rows)]).astype(jnp.int32)
    filled = _invert(dest, empty_lo, empty_hi, n_rows)
    is_pad = filled == 0
    pad_rank = jnp.cumsum(is_pad.astype(jnp.int32)) - 1
    assign = filled - 1
    src_tok = jnp.where(is_pad, 0, assign // 2)
    dst_row = jnp.where(is_pad, 2 * T + pad_rank, (assign % 2) * T + assign // 2)
    spare = n_rows + jnp.arange(EXPERT_TILE, dtype=jnp.int32)
    dst_row = jnp.concatenate([spare, dst_row])

    ys = _experts(x, g_ffn, src_tok, dst_row, tile_expert, wg, wu, wd, layer)
    if out_blocks is None:
        return ys, gates
    return [_combine(x, gates, ys, g_final, final_norm, b0, nb) for b0, nb in out_blocks]


def _image_blocks(image_rows):
    lo, hi, start = [], [], 0
    for rows in image_rows:
        n = rows // ROWS_PER_BLOCK
        lo += [start] * n
        hi += [start + n - 1] * n
        start += n
    return jnp.asarray(lo, jnp.int32), jnp.asarray(hi, jnp.int32)


def kernel(x_prompt, x_sample, norm_mix_g, w_in, na_rpb, sg_norm_g, sg_w, sg_b, out_norm_g, w_out,
           norm_ffn_g, dense_w_gate, dense_w_up, dense_w_down,
           router_w, moe_w_gate, moe_w_up, moe_w_down, final_norm_g):
    D = x_prompt.shape[-1]
    depth = w_in.shape[0]
    assert D == 2 * HALF
    image_rows = []
    for xs in (x_prompt, x_sample):
        assert xs.shape[1] % TM == 0 and xs.shape[1] >= WIN_BLOCKS * TM
        image_rows += [xs.shape[1] // GRID_W] * xs.shape[0]
    blk_lo, blk_hi = _image_blocks(image_rows)
    n_prompt = x_prompt.shape[0] * x_prompt.shape[1]

    xs = (x_prompt.reshape(-1, D), x_sample.reshape(-1, D))
    bd = jnp.asarray(np.kron(np.eye(N_HEADS, dtype=np.float32),
                             np.ones((HEAD_DIM, HEAD_DIM), np.float32)), BF16)
    g_final = final_norm_g.reshape(1, D)
    na_bias = _na_bias_tables(na_rpb)

    w_in, w_out, dense_w_gate, dense_w_up, dense_w_down, moe_w_gate, moe_w_up, moe_w_down = (
        w.astype(BF16) for w in (w_in, w_out, dense_w_gate, dense_w_up, dense_w_down,
                                 moe_w_gate, moe_w_up, moe_w_down))

    pending_moe = None
    for l in range(depth):
        q, k, v, u, svn, *x_new = _inproj(xs, norm_mix_g[l].reshape(1, D), w_in, l,
                                          sg_norm_g[l].reshape(1, HALF), bd, pending_moe)
        if pending_moe is not None:
            xs, pending_moe = (x_new[0],), None
        attn = _natten(q, k, v, na_bias[l], blk_lo, blk_hi)
        sgw2 = sg_w[l].astype(BF16).reshape(N_PAIRS, 2 * SG_CHUNK, SG_CHUNK)
        sgb_t = jnp.repeat(jnp.transpose(sg_b[l]), HEAD_DIM, axis=1)
        g_ffn = norm_ffn_g[l].reshape(1, D)
        i = l // 2
        last = l == depth - 1
        dense = l % 2 == 0
        x, *routing = _mixout(attn, u, svn, xs, sgw2, sgb_t, out_norm_g[l].reshape(1, D), w_out, l,
                              route=None if dense else _route_operands(g_ffn, router_w[i]))
        if dense:
            x = _ffn(x, g_ffn, dense_w_gate, dense_w_up, dense_w_down, i)
            if last:
                x = _final_norm(x, g_final)
                outs = [x[:n_prompt], x[n_prompt:]]
        elif last:
            nb, npb = x.shape[0] // TM, n_prompt // TM
            outs = _moe(x, routing, g_ffn, moe_w_gate, moe_w_up, moe_w_down, i, g_final,
                        final_norm=True, out_blocks=[(0, npb), (npb, nb - npb)])
        else:
            pending_moe = _moe(x, routing, g_ffn, moe_w_gate, moe_w_up, moe_w_down, i, g_final,
                               final_norm=False, out_blocks=None)
        xs = (x,)
    return (outs[0].reshape(x_prompt.shape), outs[1].reshape(x_sample.shape))
```

```python
import functools

import numpy as np
import jax
import jax.numpy as jnp
from jax import lax
from jax.experimental import pallas as pl
from jax.experimental.pallas import tpu as pltpu

F32 = jnp.float32
BF16 = jnp.bfloat16

GRID_W = 64
WIN_ROWS = 8
WIN_COLS = 16
N_HEADS = 8
HEAD_DIM = 64
HALF = N_HEADS * HEAD_DIM
N_PAIRS = N_HEADS // 2
LANES = 128
SG_CHUNK = 128
N_EXPERTS = 8
RMS_EPS = 1e-6
MASK_VALUE = -1e30

TM = 512
ROWS_PER_BLOCK = TM // GRID_W
WIN_BLOCKS = 3
LOG2_E = 1.4426950408889634
EXPERT_TILE = 512
INVERT_CHUNK = 2048
VMEM_LIMIT = 56 * 1024 * 1024


def _params(sem, vmem=VMEM_LIMIT):
    return pltpu.CompilerParams(dimension_semantics=sem, vmem_limit_bytes=vmem)


def _const_spec(shape):
    nd = len(shape)
    return pl.BlockSpec(shape, lambda *_: (0,) * nd, pipeline_mode=pl.Buffered(1))


def _layer_spec(w, layer):
    nd = w.ndim - 1
    return pl.BlockSpec((None,) + w.shape[1:], lambda *_: (layer,) + (0,) * nd,
                        pipeline_mode=pl.Buffered(1))


def _rms_scale(x):
    return lax.rsqrt(jnp.mean(x * x, axis=-1, keepdims=True) + RMS_EPS)


def _stream_specs(parts):
    D = parts[0].shape[1]
    if len(parts) == 1:
        return [pl.BlockSpec((TM, D), lambda i: (i, 0))]
    first = parts[0].shape[0] // TM
    return [pl.BlockSpec((TM, D), lambda i: (jnp.minimum(i, first - 1), 0)),
            pl.BlockSpec((TM, D), lambda i: (jnp.maximum(i - first, 0), 0))]


def _stream_tile(refs, first):
    if len(refs) == 1:
        return refs[0][...]
    return jnp.where(pl.program_id(0) < first, refs[0][...], refs[1][...])


def _gelu_tanh(x):
    return x * (0.5 * (1.0 + jnp.tanh(0.7978845608028654 * (x + 0.044715 * (x * x * x)))))


def _inproj_kernel(*refs, n_x, first, pending_moe):
    x = _stream_tile(refs[:n_x], first)
    if pending_moe:
        y0_ref, y1_ref, gate_ref = refs[n_x:n_x + 3]
        g_ref, w_ref, sgg_ref, bd_ref, q_ref, k_ref, v_ref, u_ref, sv_ref, x_out_ref = refs[n_x + 3:]
        gates = gate_ref[...]
        x = x + (y0_ref[...] * gates[:, 0:1] + y1_ref[...] * gates[:, 1:2])
        x_out_ref[...] = x
    else:
        g_ref, w_ref, sgg_ref, bd_ref, q_ref, k_ref, v_ref, u_ref, sv_ref = refs[n_x:]
    h = (x * _rms_scale(x) * g_ref[...]).astype(BF16)

    def proj(n):
        return jnp.dot(h, w_ref[:, n * HALF:(n + 1) * HALF], preferred_element_type=F32)

    u_ref[...] = _gelu_tanh(proj(3)).astype(BF16)
    sv = _gelu_tanh(proj(4))
    ssq = jnp.dot((sv * sv).astype(BF16), bd_ref[...], preferred_element_type=F32)
    sv_ref[...] = (sv * lax.rsqrt(ssq * (1.0 / HEAD_DIM) + RMS_EPS) * sgg_ref[...]).astype(BF16)

    q = proj(0) * (HEAD_DIM ** -0.5 * LOG2_E)
    is_even = (lax.broadcasted_iota(jnp.int32, q.shape, 1) % LANES) < HEAD_DIM
    qe = jnp.where(is_even, q, 0.0).astype(BF16)
    qo = jnp.where(is_even, 0.0, q).astype(BF16)
    for r in range(ROWS_PER_BLOCK):
        rows = slice(r * GRID_W, (r + 1) * GRID_W)
        q_ref[2 * r * GRID_W:(2 * r + 1) * GRID_W, :] = qe[rows]
        q_ref[(2 * r + 1) * GRID_W:(2 * r + 2) * GRID_W, :] = qo[rows]
    k_ref[...] = proj(1).astype(BF16)
    v_ref[...] = proj(2).astype(BF16)


def _inproj(xs, g, w_in, layer, sgg, bd, pending_moe=None):
    T = sum(x.shape[0] for x in xs)
    D = xs[0].shape[1]
    out = jax.ShapeDtypeStruct((T, HALF), BF16)
    tile = pl.BlockSpec((TM, HALF), lambda i: (i, 0))
    full = pl.BlockSpec((TM, D), lambda i: (i, 0))
    moe_args, moe_specs, moe_out_specs, moe_out_shape = [], [], [], []
    if pending_moe is not None:
        ys, gates = pending_moe
        moe_args = [ys, ys, gates]
        moe_specs = [full, pl.BlockSpec((TM, D), lambda i: (i + T // TM, 0)),
                     pl.BlockSpec((TM, LANES), lambda i: (i, 0))]
        moe_out_specs = [full]
        moe_out_shape = [jax.ShapeDtypeStruct((T, D), F32)]
    return pl.pallas_call(
        functools.partial(_inproj_kernel, n_x=len(xs), first=xs[0].shape[0] // TM,
                          pending_moe=pending_moe is not None),
        grid=(T // TM,),
        in_specs=_stream_specs(xs) + moe_specs + [
            _const_spec((1, D)), _layer_spec(w_in, layer),
            _const_spec((1, HALF)), _const_spec((HALF, HALF))],
        out_specs=[pl.BlockSpec((2 * TM, HALF), lambda i: (i, 0))] + [tile] * 4 + moe_out_specs,
        out_shape=[jax.ShapeDtypeStruct((2 * T, HALF), BF16)] + [out] * 4 + moe_out_shape,
        compiler_params=_params(("parallel",)),
        name="inproj",
    )(*xs, *moe_args, g, w_in, sgg, bd)


def _row_plan(blk, n_blk):
    win_row0 = min(max(blk - 1, 0), n_blk - WIN_BLOCKS) * ROWS_PER_BLOCK
    plan = []
    for i in range(ROWS_PER_BLOCK):
        r = blk * ROWS_PER_BLOCK + i
        rs = min(max(r - WIN_ROWS // 2, 0), n_blk * ROWS_PER_BLOCK - WIN_ROWS)
        plan.append((rs - win_row0, rs - r + WIN_ROWS - 1))
    return plan


def _natten_kernel(lo_ref, hi_ref, q_ref, kwin, vwin, bias_ref, o_ref):
    b = pl.program_id(0)
    blk = b - lo_ref[b]
    n_blk = hi_ref[b] - lo_ref[b] + 1
    out_even = lax.broadcasted_iota(jnp.int32, (GRID_W, LANES), 1) < HEAD_DIM
    n_keys = WIN_ROWS * GRID_W

    def rows(plan):
        for i, (ws, variant) in enumerate(plan):
            scores = []
            for p in range(N_PAIRS):
                cols = slice(p * LANES, (p + 1) * LANES)
                s = lax.dot_general(q_ref[2 * i * GRID_W:2 * (i + 1) * GRID_W, cols],
                                    kwin[ws * GRID_W:ws * GRID_W + n_keys, cols],
                                    (((1,), (1,)), ((), ())), preferred_element_type=F32)
                scores.append(s + bias_ref[variant, p])
            for p in range(N_PAIRS):
                cols = slice(p * LANES, (p + 1) * LANES)
                s = scores[p]
                e = jnp.exp2(s - jnp.max(s, axis=-1, keepdims=True))
                denom = jnp.sum(e, axis=-1, keepdims=True)
                o2 = jnp.dot(e.astype(BF16), vwin[ws * GRID_W:ws * GRID_W + n_keys, cols],
                             preferred_element_type=F32) * (1.0 / denom)
                out = jnp.where(out_even, o2[:GRID_W], o2[GRID_W:])
                o_ref[i * GRID_W:(i + 1) * GRID_W, cols] = out.astype(BF16)

    model_blocks = WIN_BLOCKS + 2
    inner = _row_plan(1, model_blocks)
    assert all(_row_plan(k, model_blocks) == inner for k in range(1, model_blocks - 1))
    pl.when(blk == 0)(lambda: rows(_row_plan(0, model_blocks)))
    pl.when(blk == n_blk - 1)(lambda: rows(_row_plan(model_blocks - 1, model_blocks)))
    pl.when((blk > 0) & (blk < n_blk - 1))(lambda: rows(inner))


def _natten(q2, k, v, bias, blk_lo, blk_hi):
    T = v.shape[0]

    def win_map(b, lo, hi):
        return (jnp.clip(b - 1, lo[b], hi[b] - (WIN_BLOCKS - 1)) * TM, 0)

    window = pl.BlockSpec((pl.Element(WIN_BLOCKS * TM), pl.Element(HALF)), win_map)
    grid_spec = pltpu.PrefetchScalarGridSpec(
        num_scalar_prefetch=2,
        grid=(T // TM,),
        in_specs=[pl.BlockSpec((2 * TM, HALF), lambda b, lo, hi: (b, 0)), window, window,
                  pl.BlockSpec(bias.shape, lambda b, lo, hi: (0, 0, 0, 0),
                               pipeline_mode=pl.Buffered(1))],
        out_specs=pl.BlockSpec((TM, HALF), lambda b, lo, hi: (b, 0)),
    )
    return pl.pallas_call(
        _natten_kernel,
        grid_spec=grid_spec,
        out_shape=jax.ShapeDtypeStruct((T, HALF), BF16),
        compiler_params=_params(("parallel",)),
        name="natten",
    )(blk_lo, blk_hi, q2, k, v, bias)


def _na_bias_tables(rpb):
    n_layers = rpb.shape[0]
    qc = np.arange(GRID_W)
    kc = np.arange(GRID_W)
    cs = np.clip(qc - WIN_COLS // 2, 0, GRID_W - WIN_COLS)
    col_in = (kc[None, :] >= cs[:, None]) & (kc[None, :] < cs[:, None] + WIN_COLS)
    dc = np.clip(kc[None, :] - qc[:, None], -(WIN_COLS - 1), WIN_COLS - 1) + (WIN_COLS - 1)
    onehot = (dc.reshape(-1)[None, :] == np.arange(2 * WIN_COLS - 1)[:, None]).astype(np.float32)
    band = jnp.dot(rpb.astype(F32).reshape(-1, 2 * WIN_COLS - 1), onehot,
                   precision=lax.Precision.HIGHEST)
    band = band.reshape(n_layers, N_HEADS, 2 * WIN_ROWS - 1, GRID_W, GRID_W)
    band = jnp.where(col_in[None, None, None], band * LOG2_E, MASK_VALUE)
    band = jnp.transpose(band, (0, 1, 3, 2, 4))
    t = jnp.stack([band[:, :, :, o:o + WIN_ROWS, :] for o in range(WIN_ROWS)], axis=1)
    return t.reshape(n_layers, WIN_ROWS, N_PAIRS, 2 * GRID_W, WIN_ROWS * GRID_W)


def _mixout_kernel(*refs, n_x, first, with_route):
    if with_route:
        (a_ref, u_ref, sv_ref, sgw_ref, sgb_ref, g_ref, w_ref, gffn_ref, rw_ref, tri_ref,
         o_ref, idx_ref, gate_ref, cnt_ref, sg_s, carry) = refs[n_x:]
    else:
        a_ref, u_ref, sv_ref, sgw_ref, sgb_ref, g_ref, w_ref, o_ref, sg_s = refs[n_x:]
    even = lax.broadcasted_iota(jnp.int32, (SG_CHUNK, LANES), 1) < HEAD_DIM
    for c in range(TM // SG_CHUNK):
        rows = slice(c * SG_CHUNK, (c + 1) * SG_CHUNK)
        for p in range(N_PAIRS):
            cols = slice(p * LANES, (p + 1) * LANES)
            m2 = jnp.dot(sgw_ref[p], sv_ref[rows, cols], preferred_element_type=F32)
            mixed = jnp.where(even, m2[:SG_CHUNK], m2[SG_CHUNK:])
            sg_s[rows, cols] = u_ref[rows, cols].astype(F32) * (mixed + sgb_ref[:, cols])
    sg = sg_s[...]
    sgn = (sg * _rms_scale(sg) * g_ref[:, HALF:]).astype(BF16)
    a = a_ref[...].astype(F32)
    an = (a * _rms_scale(a) * g_ref[:, :HALF]).astype(BF16)
    y = jnp.dot(an, w_ref[:HALF, :], preferred_element_type=F32)
    y = y + jnp.dot(sgn, w_ref[HALF:, :], preferred_element_type=F32)
    x = _stream_tile(refs[:n_x], first) + y
    o_ref[...] = x
    if with_route:
        @pl.when(pl.program_id(0) == 0)
        def _():
            carry[...] = jnp.zeros_like(carry)

        _route_tile(x, gffn_ref, rw_ref, tri_ref, carry, idx_ref, gate_ref, cnt_ref)


def _route_tile(x, g_ref, rw_ref, tri_ref, carry, idx_ref, gate_ref, cnt_ref):
    h = x * _rms_scale(x) * g_ref[...]
    h_hi = h.astype(BF16)
    h_lo = (h - h_hi.astype(F32)).astype(BF16)
    prod = jnp.dot(jnp.concatenate([h_hi, h_lo], axis=0), rw_ref[...], preferred_element_type=F32)
    logits = (prod[:TM, :LANES] + prod[:TM, LANES:]) + (prod[TM:, :LANES] + prod[TM:, LANES:])
    col = lax.broadcasted_iota(jnp.int32, logits.shape, 1)
    colf = col.astype(F32)
    neg = -jnp.inf
    lg = jnp.where(col < N_EXPERTS, logits, neg)
    m1 = jnp.max(lg, axis=-1, keepdims=True)
    i1 = jnp.min(jnp.where(lg == m1, colf, float(LANES)), axis=-1, keepdims=True)
    sel1 = colf == i1
    lg2 = jnp.where(sel1, neg, lg)
    m2 = jnp.max(lg2, axis=-1, keepdims=True)
    i2 = jnp.min(jnp.where(lg2 == m2, colf, float(LANES)), axis=-1, keepdims=True)
    sel2 = colf == i2
    e2 = jnp.exp(m2 - m1)
    g1 = 1.0 / (1.0 + e2)
    g2 = e2 / (1.0 + e2)

    cnt = jnp.where(sel1 | sel2, 1.0, 0.0)
    before = jnp.dot(tri_ref[...], cnt.astype(BF16), preferred_element_type=F32) + carry[...]
    r1 = jnp.sum(jnp.where(sel1, before, 0.0), axis=-1, keepdims=True)
    r2 = jnp.sum(jnp.where(sel2, before, 0.0), axis=-1, keepdims=True)
    total = carry[...] + jnp.sum(cnt, axis=0, keepdims=True)
    carry[...] = total
    cnt_ref[...] = jnp.broadcast_to(total, cnt_ref.shape)

    meta = jnp.where(col == 0, i1, jnp.where(col == 1, i2, jnp.where(col == 2, r1, r2)))
    idx_ref[...] = meta.astype(jnp.int32)
    gate_ref[...] = jnp.where(col == 0, g1, g2)


def _mixout(attn, u, svn, xs, sgw2, sgb_t, g_out, w_out, layer, route=None):
    T = sum(x.shape[0] for x in xs)
    D = xs[0].shape[1]
    half = pl.BlockSpec((TM, HALF), lambda i: (i, 0))
    full = pl.BlockSpec((TM, D), lambda i: (i, 0))
    meta = pl.BlockSpec((TM, LANES), lambda i: (i, 0))
    route_args = list(route) if route else []
    return pl.pallas_call(
        functools.partial(_mixout_kernel, n_x=len(xs), first=xs[0].shape[0] // TM,
                          with_route=bool(route)),
        grid=(T // TM,),
        in_specs=_stream_specs(xs) + [
            half, half, half, _const_spec(sgw2.shape), _const_spec(sgb_t.shape),
            _const_spec((1, 2 * HALF)), _layer_spec(w_out, layer)
        ] + [_const_spec(a.shape) for a in route_args],
        out_specs=[full] + ([meta, meta, pl.BlockSpec((8, LANES), lambda i: (0, 0))] if route else []),
        out_shape=[jax.ShapeDtypeStruct((T, D), F32)] + (
            [jax.ShapeDtypeStruct((T, LANES), jnp.int32), jax.ShapeDtypeStruct((T, LANES), F32),
             jax.ShapeDtypeStruct((8, LANES), F32)] if route else []),
        scratch_shapes=[pltpu.VMEM((TM, HALF), F32)] + (
            [pltpu.VMEM((1, LANES), F32)] if route else []),
        compiler_params=_params(("arbitrary",) if route else ("parallel",)),
        name="mixout",
    )(*xs, attn, u, svn, sgw2, sgb_t, g_out, w_out, *route_args)


def _ff_chunks(d_ff):
    n = 1
    while d_ff // n > 2048 or d_ff % n or (d_ff // n) % LANES:
        n += 1
    return n


def _swiglu(h, wg_ref, wu_ref, wd_ref, n_chunks):
    d_ff = wd_ref.shape[-2]
    fc = d_ff // n_chunks
    y = None
    for c in range(n_chunks):
        cols = slice(c * fc, (c + 1) * fc)
        gate = jnp.dot(h, wg_ref[:, cols], preferred_element_type=F32)
        up = jnp.dot(h, wu_ref[:, cols], preferred_element_type=F32)
        act = (gate / (1.0 + jnp.exp(-gate)) * up).astype(BF16)
        part = jnp.dot(act, wd_ref[cols, :], preferred_element_type=F32)
        y = part if y is None else y + part
    return y


def _ffn_kernel(x_ref, g_ref, wg_ref, wu_ref, wd_ref, o_ref, *, n_chunks):
    x = x_ref[...]
    h = (x * _rms_scale(x) * g_ref[...]).astype(BF16)
    o_ref[...] = x + _swiglu(h, wg_ref, wu_ref, wd_ref, n_chunks)


def _ffn(x, g, wg, wu, wd, layer):
    T, D = x.shape
    full = pl.BlockSpec((TM, D), lambda i: (i, 0))
    return pl.pallas_call(
        functools.partial(_ffn_kernel, n_chunks=_ff_chunks(wg.shape[-1])),
        grid=(T // TM,),
        in_specs=[full, _const_spec((1, D)), _layer_spec(wg, layer), _layer_spec(wu, layer),
                  _layer_spec(wd, layer)],
        out_specs=full,
        out_shape=jax.ShapeDtypeStruct((T, D), F32),
        compiler_params=_params(("parallel",)),
        name="ffn",
    )(x, g, wg, wu, wd)


def _invert_kernel(lo_ref, hi_ref, dest_ref, out_ref):
    i = pl.program_id(0)

    @pl.when(i == 0)
    def _():
        def clear(j, carry):
            out_ref[j] = 0
            return carry

        for r in range(N_EXPERTS + 1):
            lax.fori_loop(lo_ref[r], hi_ref[r], clear, 0)

    first = i * INVERT_CHUNK + 1

    def place(a, carry):
        out_ref[dest_ref[a]] = first + a
        return carry

    lax.fori_loop(0, INVERT_CHUNK, place, 0, unroll=8)


def _invert(dest, empty_lo, empty_hi, n_rows):
    n = dest.shape[0]
    assert n % INVERT_CHUNK == 0
    grid_spec = pltpu.PrefetchScalarGridSpec(
        num_scalar_prefetch=2,
        grid=(n // INVERT_CHUNK,),
        in_specs=[pl.BlockSpec((INVERT_CHUNK,), lambda i, lo, hi: (i,), memory_space=pltpu.SMEM)],
        out_specs=pl.BlockSpec((n_rows,), lambda i, lo, hi: (0,), memory_space=pltpu.SMEM),
    )
    return pl.pallas_call(
        _invert_kernel,
        grid_spec=grid_spec,
        out_shape=jax.ShapeDtypeStruct((n_rows,), jnp.int32),
        compiler_params=_params(("arbitrary",)),
        name="invert",
    )(empty_lo, empty_hi, dest)


def _experts_kernel(te_ref, src_ref, dst_ref, x_hbm, g_ref, wg_ref, wu_ref, wd_ref, ys_hbm,
                    xbuf, obuf, gsem, ssem, *, n_chunks):
    del te_ref
    i = pl.program_id(0)
    slot = i % 2

    def gather_wait(s):
        pltpu.make_async_copy(x_hbm.at[pl.ds(0, EXPERT_TILE)], xbuf.at[s], gsem.at[s]).wait()

    @pl.when(i == 0)
    def _():
        xbuf[...] = jnp.zeros_like(xbuf)
        obuf[...] = jnp.zeros_like(obuf)

    @pl.when(i >= 1)
    def _():
        gather_wait(1 - slot)

    x = xbuf[1 - slot]
    h = (x * _rms_scale(x) * g_ref[...]).astype(BF16)
    for t in range(EXPERT_TILE):
        pltpu.make_async_copy(x_hbm.at[pl.ds(src_ref[t], 1)], xbuf.at[slot, pl.ds(t, 1)],
                              gsem.at[slot]).start()
        pltpu.make_async_copy(obuf.at[slot, pl.ds(t, 1)], ys_hbm.at[pl.ds(dst_ref[t], 1)],
                              ssem).start()
    y = _swiglu(h, wg_ref.at[0], wu_ref.at[0], wd_ref.at[0], n_chunks)
    pltpu.make_async_copy(obuf.at[slot], ys_hbm.at[pl.ds(0, EXPERT_TILE)], ssem).wait()
    obuf[1 - slot] = y

    @pl.when(i == pl.num_programs(0) - 1)
    def _():
        gather_wait(slot)


def _experts(x, g, src_tok, dst_row, tile_expert, wg, wu, wd, layer):
    T, D = x.shape
    d_ff = wg.shape[-1]
    n_tiles = src_tok.shape[0] // EXPERT_TILE
    n_rows = dst_row.shape[0]

    def w_spec(w):
        return pl.BlockSpec((None, 1) + w.shape[2:],
                            lambda i, te: (layer, te[jnp.clip(i - 1, 0, n_tiles - 1)], 0, 0),
                            pipeline_mode=pl.Buffered(1))

    grid_spec = pltpu.PrefetchScalarGridSpec(
        num_scalar_prefetch=1,
        grid=(n_tiles + 2,),
        in_specs=[pl.BlockSpec((EXPERT_TILE,), lambda i, te: (jnp.minimum(i, n_tiles - 1),),
                               memory_space=pltpu.SMEM),
                  pl.BlockSpec((EXPERT_TILE,), lambda i, te: (jnp.maximum(i - 1, 0),),
                               memory_space=pltpu.SMEM),
                  pl.BlockSpec(memory_space=pl.ANY),
                  pl.BlockSpec((1, D), lambda i, te: (0, 0), pipeline_mode=pl.Buffered(1)),
                  w_spec(wg), w_spec(wu), w_spec(wd)],
        out_specs=pl.BlockSpec(memory_space=pl.ANY),
        scratch_shapes=[pltpu.VMEM((2, EXPERT_TILE, D), F32), pltpu.VMEM((2, EXPERT_TILE, D), F32),
                        pltpu.SemaphoreType.DMA((2,)), pltpu.SemaphoreType.DMA(())],
    )
    return pl.pallas_call(
        functools.partial(_experts_kernel, n_chunks=_ff_chunks(d_ff)),
        grid_spec=grid_spec,
        out_shape=jax.ShapeDtypeStruct((n_rows, D), F32),
        compiler_params=_params(("arbitrary",)),
        name="experts",
    )(tile_expert, src_tok, dst_row, x, g, wg, wu, wd)


def _combine_kernel(x_ref, gate_ref, g_ref, y0_ref, y1_ref, o_ref, *, final_norm):
    gates = gate_ref[...]
    y = x_ref[...] + (y0_ref[...] * gates[:, 0:1] + y1_ref[...] * gates[:, 1:2])
    if final_norm:
        y = y * _rms_scale(y) * g_ref[...]
    o_ref[...] = y


def _combine(x, gates, ys, g_final, final_norm, first_block, n_blocks):
    T, D = x.shape
    rows = lambda i: (i + first_block, 0)
    return pl.pallas_call(
        functools.partial(_combine_kernel, final_norm=final_norm),
        grid=(n_blocks,),
        in_specs=[pl.BlockSpec((TM, D), rows), pl.BlockSpec((TM, LANES), rows), _const_spec((1, D)),
                  pl.BlockSpec((TM, D), rows),
                  pl.BlockSpec((TM, D), lambda i: (i + first_block + T // TM, 0))],
        out_specs=pl.BlockSpec((TM, D), lambda i: (i, 0)),
        out_shape=jax.ShapeDtypeStruct((n_blocks * TM, D), F32),
        compiler_params=_params(("parallel",)),
        name="combine",
    )(x, gates, g_final, ys, ys)


def _final_norm_kernel(x_ref, g_ref, o_ref):
    x = x_ref[...]
    o_ref[...] = x * _rms_scale(x) * g_ref[...]


def _final_norm(x, g):
    T, D = x.shape
    full = pl.BlockSpec((TM, D), lambda i: (i, 0))
    return pl.pallas_call(
        _final_norm_kernel,
        grid=(T // TM,),
        in_specs=[full, _const_spec((1, D))],
        out_specs=full,
        out_shape=jax.ShapeDtypeStruct((T, D), F32),
        compiler_params=_params(("parallel",)),
        name="final_norm",
    )(x, g)


def _route_operands(g_ffn, router_w):
    D = router_w.shape[0]
    rw_pad = jnp.zeros((D, LANES), F32).at[:, :N_EXPERTS].set(router_w)
    rw_hi = rw_pad.astype(BF16)
    rw2 = jnp.concatenate([rw_hi, (rw_pad - rw_hi.astype(F32)).astype(BF16)], axis=1)
    tri = jnp.asarray(np.tril(np.ones((TM, TM), np.float32), -1), BF16)
    return g_ffn, rw2, tri


def _moe(x, routing, g_ffn, wg, wu, wd, layer, g_final, final_norm, out_blocks):
    T, D = x.shape
    idx, gates, cnt = routing

    counts = cnt[0, :N_EXPERTS].astype(jnp.int32)
    padded = (counts + EXPERT_TILE - 1) // EXPERT_TILE * EXPERT_TILE
    pad_end = jnp.cumsum(padded)
    pad_start = pad_end - padded
    dest = (pad_start[idx[:, 0:2]] + idx[:, 2:4]).reshape(2 * T)
    n_rows = 2 * T + N_EXPERTS * EXPERT_TILE
    n_tiles = n_rows // EXPERT_TILE
    tile_start = jnp.arange(n_tiles, dtype=jnp.int32) * EXPERT_TILE
    tile_expert = jnp.minimum(
        jnp.sum(pad_end[None, :] <= tile_start[:, None], axis=1), N_EXPERTS - 1).astype(jnp.int32)

    empty_lo = jnp.concatenate([pad_start + counts, pad_end[-1:]]).astype(jnp.int32)
    empty_hi = jnp.concatenate([pad_end, jnp.full((1,), n_rows)]).astype(jnp.int32)
    filled = _invert(dest, empty_lo, empty_hi, n_rows)
    is_pad = filled == 0
    pad_rank = jnp.cumsum(is_pad.astype(jnp.int32)) - 1
    assign = filled - 1
    src_tok = jnp.where(is_pad, 0, assign // 2)
    dst_row = jnp.where(is_pad, 2 * T + pad_rank, (assign % 2) * T + assign // 2)
    spare = n_rows + jnp.arange(EXPERT_TILE, dtype=jnp.int32)
    dst_row = jnp.concatenate([spare, dst_row])

    ys = _experts(x, g_ffn, src_tok, dst_row, tile_expert, wg, wu, wd, layer)
    if out_blocks is None:
        return ys, gates
    return [_combine(x, gates, ys, g_final, final_norm, b0, nb) for b0, nb in out_blocks]


def _image_blocks(image_rows):
    lo, hi, start = [], [], 0
    for rows in image_rows:
        n = rows // ROWS_PER_BLOCK
        lo += [start] * n
        hi += [start + n - 1] * n
        start += n
    return jnp.asarray(lo, jnp.int32), jnp.asarray(hi, jnp.int32)


def kernel(x_prompt, x_sample, norm_mix_g, w_in, na_rpb, sg_norm_g, sg_w, sg_b, out_norm_g, w_out,
           norm_ffn_g, dense_w_gate, dense_w_up, dense_w_down,
           router_w, moe_w_gate, moe_w_up, moe_w_down, final_norm_g):
    D = x_prompt.shape[-1]
    depth = w_in.shape[0]
    assert D == 2 * HALF
    image_rows = []
    for xs in (x_prompt, x_sample):
        assert xs.shape[1] % TM == 0 and xs.shape[1] >= WIN_BLOCKS * TM
        image_rows += [xs.shape[1] // GRID_W] * xs.shape[0]
    blk_lo, blk_hi = _image_blocks(image_rows)
    n_prompt = x_prompt.shape[0] * x_prompt.shape[1]

    xs = (x_prompt.reshape(-1, D), x_sample.reshape(-1, D))
    bd = jnp.asarray(np.kron(np.eye(N_HEADS, dtype=np.float32),
                             np.ones((HEAD_DIM, HEAD_DIM), np.float32)), BF16)
    g_final = final_norm_g.reshape(1, D)
    na_bias = _na_bias_tables(na_rpb)

    w_in, w_out, dense_w_gate, dense_w_up, dense_w_down, moe_w_gate, moe_w_up, moe_w_down = (
        w.astype(BF16) for w in (w_in, w_out, dense_w_gate, dense_w_up, dense_w_down,
                                 moe_w_gate, moe_w_up, moe_w_down))

    pending_moe = None
    for l in range(depth):
        q, k, v, u, svn, *x_new = _inproj(xs, norm_mix_g[l].reshape(1, D), w_in, l,
                                          sg_norm_g[l].reshape(1, HALF), bd, pending_moe)
        if pending_moe is not None:
            xs, pending_moe = (x_new[0],), None
        attn = _natten(q, k, v, na_bias[l], blk_lo, blk_hi)
        sgw2 = sg_w[l].astype(BF16).reshape(N_PAIRS, 2 * SG_CHUNK, SG_CHUNK)
        sgb_t = jnp.repeat(jnp.transpose(sg_b[l]), HEAD_DIM, axis=1)
        g_ffn = norm_ffn_g[l].reshape(1, D)
        i = l // 2
        last = l == depth - 1
        dense = l % 2 == 0
        x, *routing = _mixout(attn, u, svn, xs, sgw2, sgb_t, out_norm_g[l].reshape(1, D), w_out, l,
                              route=None if dense else _route_operands(g_ffn, router_w[i]))
        if dense:
            x = _ffn(x, g_ffn, dense_w_gate, dense_w_up, dense_w_down, i)
            if last:
                x = _final_norm(x, g_final)
                outs = [x[:n_prompt], x[n_prompt:]]
        elif last:
            nb, npb = x.shape[0] // TM, n_prompt // TM
            outs = _moe(x, routing, g_ffn, moe_w_gate, moe_w_up, moe_w_down, i, g_final,
                        final_norm=True, out_blocks=[(0, npb), (npb, nb - npb)])
        else:
            pending_moe = _moe(x, routing, g_ffn, moe_w_gate, moe_w_up, moe_w_down, i, g_final,
                               final_norm=False, out_blocks=None)
        xs = (x,)
    return (outs[0].reshape(x_prompt.shape), outs[1].reshape(x_sample.shape))
```

```python
import functools

import numpy as np
import jax
import jax.numpy as jnp
from jax import lax
from jax.experimental import pallas as pl
from jax.experimental.pallas import tpu as pltpu

F32 = jnp.float32
BF16 = jnp.bfloat16

GRID_W = 64
WIN_ROWS = 8
WIN_COLS = 16
N_HEADS = 8
HEAD_DIM = 64
HALF = N_HEADS * HEAD_DIM
N_PAIRS = N_HEADS // 2
LANES = 128
SG_CHUNK = 128
N_EXPERTS = 8
RMS_EPS = 1e-6
MASK_VALUE = -1e30

TM = 512
ROWS_PER_BLOCK = TM // GRID_W
WIN_BLOCKS = 3
LOG2_E = 1.4426950408889634
EXPERT_TILE = 512
INVERT_CHUNK = 2048
VMEM_LIMIT = 56 * 1024 * 1024


def _params(sem, vmem=VMEM_LIMIT):
    return pltpu.CompilerParams(dimension_semantics=sem, vmem_limit_bytes=vmem)


def _const_spec(shape):
    nd = len(shape)
    return pl.BlockSpec(shape, lambda *_: (0,) * nd, pipeline_mode=pl.Buffered(1))


def _layer_spec(w, layer):
    nd = w.ndim - 1
    return pl.BlockSpec((None,) + w.shape[1:], lambda *_: (layer,) + (0,) * nd,
                        pipeline_mode=pl.Buffered(1))


def _rms_scale(x):
    return lax.rsqrt(jnp.mean(x * x, axis=-1, keepdims=True) + RMS_EPS)


def _stream_specs(parts):
    D = parts[0].shape[1]
    if len(parts) == 1:
        return [pl.BlockSpec((TM, D), lambda i: (i, 0))]
    first = parts[0].shape[0] // TM
    return [pl.BlockSpec((TM, D), lambda i: (jnp.minimum(i, first - 1), 0)),
            pl.BlockSpec((TM, D), lambda i: (jnp.maximum(i - first, 0), 0))]


def _stream_tile(refs, first):
    if len(refs) == 1:
        return refs[0][...]
    return jnp.where(pl.program_id(0) < first, refs[0][...], refs[1][...])


def _gelu_tanh(x):
    return x * (0.5 * (1.0 + jnp.tanh(0.7978845608028654 * (x + 0.044715 * (x * x * x)))))


def _inproj_kernel(*refs, n_x, first, pending_moe):
    x = _stream_tile(refs[:n_x], first)
    if pending_moe:
        y0_ref, y1_ref, gate_ref = refs[n_x:n_x + 3]
        g_ref, w_ref, sgg_ref, bd_ref, q_ref, k_ref, v_ref, u_ref, sv_ref, x_out_ref = refs[n_x + 3:]
        gates = gate_ref[...]
        x = x + (y0_ref[...] * gates[:, 0:1] + y1_ref[...] * gates[:, 1:2])
        x_out_ref[...] = x
    else:
        g_ref, w_ref, sgg_ref, bd_ref, q_ref, k_ref, v_ref, u_ref, sv_ref = refs[n_x:]
    h = (x * _rms_scale(x) * g_ref[...]).astype(BF16)

    def proj(n):
        return jnp.dot(h, w_ref[:, n * HALF:(n + 1) * HALF], preferred_element_type=F32)

    u_ref[...] = _gelu_tanh(proj(3)).astype(BF16)
    sv = _gelu_tanh(proj(4))
    ssq = jnp.dot((sv * sv).astype(BF16), bd_ref[...], preferred_element_type=F32)
    sv_ref[...] = (sv * lax.rsqrt(ssq * (1.0 / HEAD_DIM) + RMS_EPS) * sgg_ref[...]).astype(BF16)

    q = proj(0) * (HEAD_DIM ** -0.5 * LOG2_E)
    is_even = (lax.broadcasted_iota(jnp.int32, q.shape, 1) % LANES) < HEAD_DIM
    qe = jnp.where(is_even, q, 0.0).astype(BF16)
    qo = jnp.where(is_even, 0.0, q).astype(BF16)
    for r in range(ROWS_PER_BLOCK):
        rows = slice(r * GRID_W, (r + 1) * GRID_W)
        q_ref[2 * r * GRID_W:(2 * r + 1) * GRID_W, :] = qe[rows]
        q_ref[(2 * r + 1) * GRID_W:(2 * r + 2) * GRID_W, :] = qo[rows]
    k_ref[...] = proj(1).astype(BF16)
    v_ref[...] = proj(2).astype(BF16)


def _inproj(xs, g, w_in, layer, sgg, bd, pending_moe=None):
    T = sum(x.shape[0] for x in xs)
    D = xs[0].shape[1]
    out = jax.ShapeDtypeStruct((T, HALF), BF16)
    tile = pl.BlockSpec((TM, HALF), lambda i: (i, 0))
    full = pl.BlockSpec((TM, D), lambda i: (i, 0))
    moe_args, moe_specs, moe_out_specs, moe_out_shape = [], [], [], []
    if pending_moe is not None:
        ys, gates = pending_moe
        moe_args = [ys, ys, gates]
        moe_specs = [full, pl.BlockSpec((TM, D), lambda i: (i + T // TM, 0)),
                     pl.BlockSpec((TM, LANES), lambda i: (i, 0))]
        moe_out_specs = [full]
        moe_out_shape = [jax.ShapeDtypeStruct((T, D), F32)]
    return pl.pallas_call(
        functools.partial(_inproj_kernel, n_x=len(xs), first=xs[0].shape[0] // TM,
                          pending_moe=pending_moe is not None),
        grid=(T // TM,),
        in_specs=_stream_specs(xs) + moe_specs + [
            _const_spec((1, D)), _layer_spec(w_in, layer),
            _const_spec((1, HALF)), _const_spec((HALF, HALF))],
        out_specs=[pl.BlockSpec((2 * TM, HALF), lambda i: (i, 0))] + [tile] * 4 + moe_out_specs,
        out_shape=[jax.ShapeDtypeStruct((2 * T, HALF), BF16)] + [out] * 4 + moe_out_shape,
        compiler_params=_params(("parallel",)),
        name="inproj",
    )(*xs, *moe_args, g, w_in, sgg, bd)


def _row_plan(blk, n_blk):
    win_row0 = min(max(blk - 1, 0), n_blk - WIN_BLOCKS) * ROWS_PER_BLOCK
    plan = []
    for i in range(ROWS_PER_BLOCK):
        r = blk * ROWS_PER_BLOCK + i
        rs = min(max(r - WIN_ROWS // 2, 0), n_blk * ROWS_PER_BLOCK - WIN_ROWS)
        plan.append((rs - win_row0, rs - r + WIN_ROWS - 1))
    return plan


def _natten_kernel(lo_ref, hi_ref, q_ref, kwin, vwin, bias_ref, o_ref):
    b = pl.program_id(0)
    blk = b - lo_ref[b]
    n_blk = hi_ref[b] - lo_ref[b] + 1
    out_even = lax.broadcasted_iota(jnp.int32, (GRID_W, LANES), 1) < HEAD_DIM
    n_keys = WIN_ROWS * GRID_W

    def rows(plan):
        for i, (ws, variant) in enumerate(plan):
            scores = []
            for p in range(N_PAIRS):
                cols = slice(p * LANES, (p + 1) * LANES)
                s = lax.dot_general(q_ref[2 * i * GRID_W:2 * (i + 1) * GRID_W, cols],
                                    kwin[ws * GRID_W:ws * GRID_W + n_keys, cols],
                                    (((1,), (1,)), ((), ())), preferred_element_type=F32)
                scores.append(s + bias_ref[variant, p])
            for p in range(N_PAIRS):
                cols = slice(p * LANES, (p + 1) * LANES)
                s = scores[p]
                e = jnp.exp2(s - jnp.max(s, axis=-1, keepdims=True))
                denom = jnp.sum(e, axis=-1, keepdims=True)
                o2 = jnp.dot(e.astype(BF16), vwin[ws * GRID_W:ws * GRID_W + n_keys, cols],
                             preferred_element_type=F32) * (1.0 / denom)
                out = jnp.where(out_even, o2[:GRID_W], o2[GRID_W:])
                o_ref[i * GRID_W:(i + 1) * GRID_W, cols] = out.astype(BF16)

    model_blocks = WIN_BLOCKS + 2
    inner = _row_plan(1, model_blocks)
    assert all(_row_plan(k, model_blocks) == inner for k in range(1, model_blocks - 1))
    pl.when(blk == 0)(lambda: rows(_row_plan(0, model_blocks)))
    pl.when(blk == n_blk - 1)(lambda: rows(_row_plan(model_blocks - 1, model_blocks)))
    pl.when((blk > 0) & (blk < n_blk - 1))(lambda: rows(inner))


def _natten(q2, k, v, bias, blk_lo, blk_hi):
    T = v.shape[0]

    def win_map(b, lo, hi):
        return (jnp.clip(b - 1, lo[b], hi[b] - (WIN_BLOCKS - 1)) * TM, 0)

    window = pl.BlockSpec((pl.Element(WIN_BLOCKS * TM), pl.Element(HALF)), win_map)
    grid_spec = pltpu.PrefetchScalarGridSpec(
        num_scalar_prefetch=2,
        grid=(T // TM,),
        in_specs=[pl.BlockSpec((2 * TM, HALF), lambda b, lo, hi: (b, 0)), window, window,
                  pl.BlockSpec(bias.shape, lambda b, lo, hi: (0, 0, 0, 0),
                               pipeline_mode=pl.Buffered(1))],
        out_specs=pl.BlockSpec((TM, HALF), lambda b, lo, hi: (b, 0)),
    )
    return pl.pallas_call(
        _natten_kernel,
        grid_spec=grid_spec,
        out_shape=jax.ShapeDtypeStruct((T, HALF), BF16),
        compiler_params=_params(("parallel",)),
        name="natten",
    )(blk_lo, blk_hi, q2, k, v, bias)


def _na_bias_tables(rpb):
    n_layers = rpb.shape[0]
    qc = np.arange(GRID_W)
    kc = np.arange(GRID_W)
    cs = np.clip(qc - WIN_COLS // 2, 0, GRID_W - WIN_COLS)
    col_in = (kc[None, :] >= cs[:, None]) & (kc[None, :] < cs[:, None] + WIN_COLS)
    dc = np.clip(kc[None, :] - qc[:, None], -(WIN_COLS - 1), WIN_COLS - 1) + (WIN_COLS - 1)
    onehot = (dc.reshape(-1)[None, :] == np.arange(2 * WIN_COLS - 1)[:, None]).astype(np.float32)
    band = jnp.dot(rpb.astype(F32).reshape(-1, 2 * WIN_COLS - 1), onehot,
                   precision=lax.Precision.HIGHEST)
    band = band.reshape(n_layers, N_HEADS, 2 * WIN_ROWS - 1, GRID_W, GRID_W)
    band = jnp.where(col_in[None, None, None], band * LOG2_E, MASK_VALUE)
    band = jnp.transpose(band, (0, 1, 3, 2, 4))
    t = jnp.stack([band[:, :, :, o:o + WIN_ROWS, :] for o in range(WIN_ROWS)], axis=1)
    return t.reshape(n_layers, WIN_ROWS, N_PAIRS, 2 * GRID_W, WIN_ROWS * GRID_W)


def _mixout_kernel(*refs, n_x, first, tail, n_chunks):
    a_ref, u_ref, sv_ref, sgw_ref, sgb_ref, g_ref, w_ref, gffn_ref = refs[n_x:n_x + 8]
    if tail == "ffn":
        wg_ref, wu_ref, wd_ref, o_ref, sg_s = refs[n_x + 8:]
    else:
        rw_ref, tri_ref, o_ref, idx_ref, gate_ref, cnt_ref, sg_s, carry = refs[n_x + 8:]
    even = lax.broadcasted_iota(jnp.int32, (SG_CHUNK, LANES), 1) < HEAD_DIM
    for c in range(TM // SG_CHUNK):
        rows = slice(c * SG_CHUNK, (c + 1) * SG_CHUNK)
        for p in range(N_PAIRS):
            cols = slice(p * LANES, (p + 1) * LANES)
            m2 = jnp.dot(sgw_ref[p], sv_ref[rows, cols], preferred_element_type=F32)
            mixed = jnp.where(even, m2[:SG_CHUNK], m2[SG_CHUNK:])
            sg_s[rows, cols] = u_ref[rows, cols].astype(F32) * (mixed + sgb_ref[:, cols])
    sg = sg_s[...]
    sgn = (sg * _rms_scale(sg) * g_ref[:, HALF:]).astype(BF16)
    a = a_ref[...].astype(F32)
    an = (a * _rms_scale(a) * g_ref[:, :HALF]).astype(BF16)
    y = jnp.dot(an, w_ref[:HALF, :], preferred_element_type=F32)
    y = y + jnp.dot(sgn, w_ref[HALF:, :], preferred_element_type=F32)
    x = _stream_tile(refs[:n_x], first) + y
    if tail == "ffn":
        h = (x * _rms_scale(x) * gffn_ref[...]).astype(BF16)
        o_ref[...] = x + _swiglu(h, wg_ref, wu_ref, wd_ref, n_chunks)
    else:
        o_ref[...] = x

        @pl.when(pl.program_id(0) == 0)
        def _():
            carry[...] = jnp.zeros_like(carry)

        _route_tile(x, gffn_ref, rw_ref, tri_ref, carry, idx_ref, gate_ref, cnt_ref)


def _route_tile(x, g_ref, rw_ref, tri_ref, carry, idx_ref, gate_ref, cnt_ref):
    h = x * _rms_scale(x) * g_ref[...]
    h_hi = h.astype(BF16)
    h_lo = (h - h_hi.astype(F32)).astype(BF16)
    prod = jnp.dot(jnp.concatenate([h_hi, h_lo], axis=0), rw_ref[...], preferred_element_type=F32)
    logits = (prod[:TM, :LANES] + prod[:TM, LANES:]) + (prod[TM:, :LANES] + prod[TM:, LANES:])
    col = lax.broadcasted_iota(jnp.int32, logits.shape, 1)
    colf = col.astype(F32)
    neg = -jnp.inf
    lg = jnp.where(col < N_EXPERTS, logits, neg)
    m1 = jnp.max(lg, axis=-1, keepdims=True)
    i1 = jnp.min(jnp.where(lg == m1, colf, float(LANES)), axis=-1, keepdims=True)
    sel1 = colf == i1
    lg2 = jnp.where(sel1, neg, lg)
    m2 = jnp.max(lg2, axis=-1, keepdims=True)
    i2 = jnp.min(jnp.where(lg2 == m2, colf, float(LANES)), axis=-1, keepdims=True)
    sel2 = colf == i2
    e2 = jnp.exp(m2 - m1)
    g1 = 1.0 / (1.0 + e2)
    g2 = e2 / (1.0 + e2)

    cnt = jnp.where(sel1 | sel2, 1.0, 0.0)
    before = jnp.dot(tri_ref[...], cnt.astype(BF16), preferred_element_type=F32) + carry[...]
    r1 = jnp.sum(jnp.where(sel1, before, 0.0), axis=-1, keepdims=True)
    r2 = jnp.sum(jnp.where(sel2, before, 0.0), axis=-1, keepdims=True)
    total = carry[...] + jnp.sum(cnt, axis=0, keepdims=True)
    carry[...] = total
    cnt_ref[...] = jnp.broadcast_to(total, cnt_ref.shape)

    meta = jnp.where(col == 0, i1, jnp.where(col == 1, i2, jnp.where(col == 2, r1, r2)))
    idx_ref[...] = meta.astype(jnp.int32)
    gate_ref[...] = jnp.where(col == 0, g1, g2)


def _mixout(attn, u, svn, xs, sgw2, sgb_t, g_out, w_out, layer, g_ffn, ffn=None, route=None):
    T = sum(x.shape[0] for x in xs)
    D = xs[0].shape[1]
    half = pl.BlockSpec((TM, HALF), lambda i: (i, 0))
    full = pl.BlockSpec((TM, D), lambda i: (i, 0))
    meta = pl.BlockSpec((TM, LANES), lambda i: (i, 0))
    if ffn is not None:
        *tail_args, ffn_layer = ffn
        tail_specs = [_layer_spec(w, ffn_layer) for w in tail_args]
        n_chunks = _ff_chunks(tail_args[0].shape[-1])
    else:
        tail_args = list(route)
        tail_specs = [_const_spec(a.shape) for a in tail_args]
        n_chunks = None
    return pl.pallas_call(
        functools.partial(_mixout_kernel, n_x=len(xs), first=xs[0].shape[0] // TM,
                          tail="ffn" if ffn is not None else "route", n_chunks=n_chunks),
        grid=(T // TM,),
        in_specs=_stream_specs(xs) + [
            half, half, half, _const_spec(sgw2.shape), _const_spec(sgb_t.shape),
            _const_spec((1, 2 * HALF)), _layer_spec(w_out, layer), _const_spec((1, D))
        ] + tail_specs,
        out_specs=[full] + ([meta, meta, pl.BlockSpec((8, LANES), lambda i: (0, 0))] if route else []),
        out_shape=[jax.ShapeDtypeStruct((T, D), F32)] + (
            [jax.ShapeDtypeStruct((T, LANES), jnp.int32), jax.ShapeDtypeStruct((T, LANES), F32),
             jax.ShapeDtypeStruct((8, LANES), F32)] if route else []),
        scratch_shapes=[pltpu.VMEM((TM, HALF), F32)] + (
            [pltpu.VMEM((1, LANES), F32)] if route else []),
        compiler_params=_params(("arbitrary",) if route else ("parallel",)),
        name="mixout",
    )(*xs, attn, u, svn, sgw2, sgb_t, g_out, w_out, g_ffn, *tail_args)


def _ff_chunks(d_ff):
    n = 1
    while d_ff // n > 2048 or d_ff % n or (d_ff // n) % LANES:
        n += 1
    return n


def _swiglu(h, wg_ref, wu_ref, wd_ref, n_chunks):
    d_ff = wd_ref.shape[-2]
    fc = d_ff // n_chunks
    y = None
    for c in range(n_chunks):
        cols = slice(c * fc, (c + 1) * fc)
        gate = jnp.dot(h, wg_ref[:, cols], preferred_element_type=F32)
        up = jnp.dot(h, wu_ref[:, cols], preferred_element_type=F32)
        act = (gate / (1.0 + jnp.exp(-gate)) * up).astype(BF16)
        part = jnp.dot(act, wd_ref[cols, :], preferred_element_type=F32)
        y = part if y is None else y + part
    return y


def _invert_kernel(lo_ref, hi_ref, dest_ref, out_ref):
    i = pl.program_id(0)

    @pl.when(i == 0)
    def _():
        def clear(j, carry):
            out_ref[j] = 0
            return carry

        for r in range(N_EXPERTS + 1):
            lax.fori_loop(lo_ref[r], hi_ref[r], clear, 0)

    first = i * INVERT_CHUNK + 1

    def place(a, carry):
        out_ref[dest_ref[a]] = first + a
        return carry

    lax.fori_loop(0, INVERT_CHUNK, place, 0, unroll=8)


def _invert(dest, empty_lo, empty_hi, n_rows):
    n = dest.shape[0]
    assert n % INVERT_CHUNK == 0
    grid_spec = pltpu.PrefetchScalarGridSpec(
        num_scalar_prefetch=2,
        grid=(n // INVERT_CHUNK,),
        in_specs=[pl.BlockSpec((INVERT_CHUNK,), lambda i, lo, hi: (i,), memory_space=pltpu.SMEM)],
        out_specs=pl.BlockSpec((n_rows,), lambda i, lo, hi: (0,), memory_space=pltpu.SMEM),
    )
    return pl.pallas_call(
        _invert_kernel,
        grid_spec=grid_spec,
        out_shape=jax.ShapeDtypeStruct((n_rows,), jnp.int32),
        compiler_params=_params(("arbitrary",)),
        name="invert",
    )(empty_lo, empty_hi, dest)


def _experts_kernel(te_ref, src_ref, dst_ref, x_hbm, g_ref, wg_ref, wu_ref, wd_ref, ys_hbm,
                    xbuf, obuf, gsem, ssem, *, n_chunks):
    del te_ref
    i = pl.program_id(0)
    slot = i % 2

    def gather_wait(s):
        pltpu.make_async_copy(x_hbm.at[pl.ds(0, EXPERT_TILE)], xbuf.at[s], gsem.at[s]).wait()

    @pl.when(i == 0)
    def _():
        xbuf[...] = jnp.zeros_like(xbuf)
        obuf[...] = jnp.zeros_like(obuf)

    @pl.when(i >= 1)
    def _():
        gather_wait(1 - slot)

    x = xbuf[1 - slot]
    h = (x * _rms_scale(x) * g_ref[...]).astype(BF16)
    for t in range(EXPERT_TILE):
        pltpu.make_async_copy(x_hbm.at[pl.ds(src_ref[t], 1)], xbuf.at[slot, pl.ds(t, 1)],
                              gsem.at[slot]).start()
        pltpu.make_async_copy(obuf.at[slot, pl.ds(t, 1)], ys_hbm.at[pl.ds(dst_ref[t], 1)],
                              ssem).start()
    y = _swiglu(h, wg_ref.at[0], wu_ref.at[0], wd_ref.at[0], n_chunks)
    pltpu.make_async_copy(obuf.at[slot], ys_hbm.at[pl.ds(0, EXPERT_TILE)], ssem).wait()
    obuf[1 - slot] = y

    @pl.when(i == pl.num_programs(0) - 1)
    def _():
        gather_wait(slot)


def _experts(x, g, src_tok, dst_row, tile_expert, wg, wu, wd, layer):
    T, D = x.shape
    d_ff = wg.shape[-1]
    n_tiles = src_tok.shape[0] // EXPERT_TILE
    n_rows = dst_row.shape[0]

    def w_spec(w):
        return pl.BlockSpec((None, 1) + w.shape[2:],
                            lambda i, te: (layer, te[jnp.clip(i - 1, 0, n_tiles - 1)], 0, 0),
                            pipeline_mode=pl.Buffered(1))

    grid_spec = pltpu.PrefetchScalarGridSpec(
        num_scalar_prefetch=1,
        grid=(n_tiles + 2,),
        in_specs=[pl.BlockSpec((EXPERT_TILE,), lambda i, te: (jnp.minimum(i, n_tiles - 1),),
                               memory_space=pltpu.SMEM),
                  pl.BlockSpec((EXPERT_TILE,), lambda i, te: (jnp.maximum(i - 1, 0),),
                               memory_space=pltpu.SMEM),
                  pl.BlockSpec(memory_space=pl.ANY),
                  pl.BlockSpec((1, D), lambda i, te: (0, 0), pipeline_mode=pl.Buffered(1)),
                  w_spec(wg), w_spec(wu), w_spec(wd)],
        out_specs=pl.BlockSpec(memory_space=pl.ANY),
        scratch_shapes=[pltpu.VMEM((2, EXPERT_TILE, D), F32), pltpu.VMEM((2, EXPERT_TILE, D), F32),
                        pltpu.SemaphoreType.DMA((2,)), pltpu.SemaphoreType.DMA(())],
    )
    return pl.pallas_call(
        functools.partial(_experts_kernel, n_chunks=_ff_chunks(d_ff)),
        grid_spec=grid_spec,
        out_shape=jax.ShapeDtypeStruct((n_rows, D), F32),
        compiler_params=_params(("arbitrary",)),
        name="experts",
    )(tile_expert, src_tok, dst_row, x, g, wg, wu, wd)


def _combine_kernel(x_ref, gate_ref, g_ref, y0_ref, y1_ref, o_ref, *, final_norm):
    gates = gate_ref[...]
    y = x_ref[...] + (y0_ref[...] * gates[:, 0:1] + y1_ref[...] * gates[:, 1:2])
    if final_norm:
        y = y * _rms_scale(y) * g_ref[...]
    o_ref[...] = y


def _combine(x, gates, ys, g_final, final_norm, first_block, n_blocks):
    T, D = x.shape
    rows = lambda i: (i + first_block, 0)
    return pl.pallas_call(
        functools.partial(_combine_kernel, final_norm=final_norm),
        grid=(n_blocks,),
        in_specs=[pl.BlockSpec((TM, D), rows), pl.BlockSpec((TM, LANES), rows), _const_spec((1, D)),
                  pl.BlockSpec((TM, D), rows),
                  pl.BlockSpec((TM, D), lambda i: (i + first_block + T // TM, 0))],
        out_specs=pl.BlockSpec((TM, D), lambda i: (i, 0)),
        out_shape=jax.ShapeDtypeStruct((n_blocks * TM, D), F32),
        compiler_params=_params(("parallel",)),
        name="combine",
    )(x, gates, g_final, ys, ys)


def _final_norm_kernel(x_ref, g_ref, o_ref):
    x = x_ref[...]
    o_ref[...] = x * _rms_scale(x) * g_ref[...]


def _final_norm(x, g):
    T, D = x.shape
    full = pl.BlockSpec((TM, D), lambda i: (i, 0))
    return pl.pallas_call(
        _final_norm_kernel,
        grid=(T // TM,),
        in_specs=[full, _const_spec((1, D))],
        out_specs=full,
        out_shape=jax.ShapeDtypeStruct((T, D), F32),
        compiler_params=_params(("parallel",)),
        name="final_norm",
    )(x, g)


def _route_operands(router_w):
    D = router_w.shape[0]
    rw_pad = jnp.zeros((D, LANES), F32).at[:, :N_EXPERTS].set(router_w)
    rw_hi = rw_pad.astype(BF16)
    rw2 = jnp.concatenate([rw_hi, (rw_pad - rw_hi.astype(F32)).astype(BF16)], axis=1)
    tri = jnp.asarray(np.tril(np.ones((TM, TM), np.float32), -1), BF16)
    return rw2, tri


def _moe(x, routing, g_ffn, wg, wu, wd, layer, g_final, final_norm, out_blocks):
    T, D = x.shape
    idx, gates, cnt = routing

    counts = cnt[0, :N_EXPERTS].astype(jnp.int32)
    padded = (counts + EXPERT_TILE - 1) // EXPERT_TILE * EXPERT_TILE
    pad_end = jnp.cumsum(padded)
    pad_start = pad_end - padded
    dest = (pad_start[idx[:, 0:2]] + idx[:, 2:4]).reshape(2 * T)
    n_rows = 2 * T + N_EXPERTS * EXPERT_TILE
    n_tiles = n_rows // EXPERT_TILE
    tile_start = jnp.arange(n_tiles, dtype=jnp.int32) * EXPERT_TILE
    tile_expert = jnp.minimum(
        jnp.sum(pad_end[None, :] <= tile_start[:, None], axis=1), N_EXPERTS - 1).astype(jnp.int32)

    empty_lo = jnp.concatenate([pad_start + counts, pad_end[-1:]]).astype(jnp.int32)
    empty_hi = jnp.concatenate([pad_end, jnp.full((1,), n_rows)]).astype(jnp.int32)
    filled = _invert(dest, empty_lo, empty_hi, n_rows)
    is_pad = filled == 0
    pad_rank = jnp.cumsum(is_pad.astype(jnp.int32)) - 1
    assign = filled - 1
    src_tok = jnp.where(is_pad, 0, assign // 2)
    dst_row = jnp.where(is_pad, 2 * T + pad_rank, (assign % 2) * T + assign // 2)
    spare = n_rows + jnp.arange(EXPERT_TILE, dtype=jnp.int32)
    dst_row = jnp.concatenate([spare, dst_row])

    ys = _experts(x, g_ffn, src_tok, dst_row, tile_expert, wg, wu, wd, layer)
    if out_blocks is None:
        return ys, gates
    return [_combine(x, gates, ys, g_final, final_norm, b0, nb) for b0, nb in out_blocks]


def _image_blocks(image_rows):
    lo, hi, start = [], [], 0
    for rows in image_rows:
        n = rows // ROWS_PER_BLOCK
        lo += [start] * n
        hi += [start + n - 1] * n
        start += n
    return jnp.asarray(lo, jnp.int32), jnp.asarray(hi, jnp.int32)


def kernel(x_prompt, x_sample, norm_mix_g, w_in, na_rpb, sg_norm_g, sg_w, sg_b, out_norm_g, w_out,
           norm_ffn_g, dense_w_gate, dense_w_up, dense_w_down,
           router_w, moe_w_gate, moe_w_up, moe_w_down, final_norm_g):
    D = x_prompt.shape[-1]
    depth = w_in.shape[0]
    assert D == 2 * HALF
    image_rows = []
    for xs in (x_prompt, x_sample):
        assert xs.shape[1] % TM == 0 and xs.shape[1] >= WIN_BLOCKS * TM
        image_rows += [xs.shape[1] // GRID_W] * xs.shape[0]
    blk_lo, blk_hi = _image_blocks(image_rows)
    n_prompt = x_prompt.shape[0] * x_prompt.shape[1]

    xs = (x_prompt.reshape(-1, D), x_sample.reshape(-1, D))
    bd = jnp.asarray(np.kron(np.eye(N_HEADS, dtype=np.float32),
                             np.ones((HEAD_DIM, HEAD_DIM), np.float32)), BF16)
    g_final = final_norm_g.reshape(1, D)
    na_bias = _na_bias_tables(na_rpb)

    w_in, w_out, dense_w_gate, dense_w_up, dense_w_down, moe_w_gate, moe_w_up, moe_w_down = (
        w.astype(BF16) for w in (w_in, w_out, dense_w_gate, dense_w_up, dense_w_down,
                                 moe_w_gate, moe_w_up, moe_w_down))

    pending_moe = None
    for l in range(depth):
        q, k, v, u, svn, *x_new = _inproj(xs, norm_mix_g[l].reshape(1, D), w_in, l,
                                          sg_norm_g[l].reshape(1, HALF), bd, pending_moe)
        if pending_moe is not None:
            xs, pending_moe = (x_new[0],), None
        attn = _natten(q, k, v, na_bias[l], blk_lo, blk_hi)
        sgw2 = sg_w[l].astype(BF16).reshape(N_PAIRS, 2 * SG_CHUNK, SG_CHUNK)
        sgb_t = jnp.repeat(jnp.transpose(sg_b[l]), HEAD_DIM, axis=1)
        g_ffn = norm_ffn_g[l].reshape(1, D)
        i = l // 2
        last = l == depth - 1
        dense = l % 2 == 0
        x, *routing = _mixout(attn, u, svn, xs, sgw2, sgb_t, out_norm_g[l].reshape(1, D), w_out, l,
                              g_ffn,
                              ffn=(dense_w_gate, dense_w_up, dense_w_down, i) if dense else None,
                              route=None if dense else _route_operands(router_w[i]))
        if dense:
            if last:
                x = _final_norm(x, g_final)
                outs = [x[:n_prompt], x[n_prompt:]]
        elif last:
            nb, npb = x.shape[0] // TM, n_prompt // TM
            outs = _moe(x, routing, g_ffn, moe_w_gate, moe_w_up, moe_w_down, i, g_final,
                        final_norm=True, out_blocks=[(0, npb), (npb, nb - npb)])
        else:
            pending_moe = _moe(x, routing, g_ffn, moe_w_gate, moe_w_up, moe_w_down, i, g_final,
                               final_norm=False, out_blocks=None)
        xs = (x,)
    return (outs[0].reshape(x_prompt.shape), outs[1].reshape(x_sample.shape))
```

```python
import functools

import numpy as np
import jax
import jax.numpy as jnp
from jax import lax
from jax.experimental import pallas as pl
from jax.experimental.pallas import tpu as pltpu

F32 = jnp.float32
BF16 = jnp.bfloat16

GRID_W = 64
WIN_ROWS = 8
WIN_COLS = 16
N_HEADS = 8
HEAD_DIM = 64
HALF = N_HEADS * HEAD_DIM
N_PAIRS = N_HEADS // 2
LANES = 128
SG_CHUNK = 128
N_EXPERTS = 8
RMS_EPS = 1e-6
MASK_VALUE = -1e30

TM = 512
ROWS_PER_BLOCK = TM // GRID_W
WIN_BLOCKS = 3
LOG2_E = 1.4426950408889634
EXPERT_TILE = 512
INVERT_CHUNK = 1024
META_LANES = 8
VMEM_LIMIT = 56 * 1024 * 1024


def _params(sem, vmem=VMEM_LIMIT):
    return pltpu.CompilerParams(dimension_semantics=sem, vmem_limit_bytes=vmem)


def _const_spec(shape):
    nd = len(shape)
    return pl.BlockSpec(shape, lambda *_: (0,) * nd, pipeline_mode=pl.Buffered(1))


def _layer_spec(w, layer):
    nd = w.ndim - 1
    return pl.BlockSpec((None,) + w.shape[1:], lambda *_: (layer,) + (0,) * nd,
                        pipeline_mode=pl.Buffered(1))


def _rms_scale(x):
    return lax.rsqrt(jnp.mean(x * x, axis=-1, keepdims=True) + RMS_EPS)


def _stream_specs(parts):
    D = parts[0].shape[1]
    if len(parts) == 1:
        return [pl.BlockSpec((TM, D), lambda i: (i, 0))]
    first = parts[0].shape[0] // TM
    return [pl.BlockSpec((TM, D), lambda i: (jnp.minimum(i, first - 1), 0)),
            pl.BlockSpec((TM, D), lambda i: (jnp.maximum(i - first, 0), 0))]


def _stream_tile(refs, first):
    if len(refs) == 1:
        return refs[0][...]
    return jnp.where(pl.program_id(0) < first, refs[0][...], refs[1][...])


def _gelu_tanh(x):
    return x * (0.5 * (1.0 + jnp.tanh(0.7978845608028654 * (x + 0.044715 * (x * x * x)))))


def _inproj_kernel(*refs, n_x, first, pending_moe):
    x = _stream_tile(refs[:n_x], first)
    if pending_moe:
        y0_ref, y1_ref, gate_ref = refs[n_x:n_x + 3]
        g_ref, w_ref, sgg_ref, bd_ref, q_ref, k_ref, v_ref, u_ref, sv_ref, x_out_ref = refs[n_x + 3:]
        gates = gate_ref[...]
        x = x + (y0_ref[...] * gates[:, 0:1] + y1_ref[...] * gates[:, 1:2])
        x_out_ref[...] = x
    else:
        g_ref, w_ref, sgg_ref, bd_ref, q_ref, k_ref, v_ref, u_ref, sv_ref = refs[n_x:]
    h = (x * _rms_scale(x) * g_ref[...]).astype(BF16)

    def proj(n):
        return jnp.dot(h, w_ref[:, n * HALF:(n + 1) * HALF], preferred_element_type=F32)

    u_ref[...] = _gelu_tanh(proj(3)).astype(BF16)
    sv = _gelu_tanh(proj(4))
    ssq = jnp.dot((sv * sv).astype(BF16), bd_ref[...], preferred_element_type=F32)
    sv_ref[...] = (sv * lax.rsqrt(ssq * (1.0 / HEAD_DIM) + RMS_EPS) * sgg_ref[...]).astype(BF16)

    q = proj(0) * (HEAD_DIM ** -0.5 * LOG2_E)
    is_even = (lax.broadcasted_iota(jnp.int32, q.shape, 1) % LANES) < HEAD_DIM
    qe = jnp.where(is_even, q, 0.0).astype(BF16)
    qo = jnp.where(is_even, 0.0, q).astype(BF16)
    for r in range(ROWS_PER_BLOCK):
        rows = slice(r * GRID_W, (r + 1) * GRID_W)
        q_ref[2 * r * GRID_W:(2 * r + 1) * GRID_W, :] = qe[rows]
        q_ref[(2 * r + 1) * GRID_W:(2 * r + 2) * GRID_W, :] = qo[rows]
    k_ref[...] = proj(1).astype(BF16)
    v_ref[...] = proj(2).astype(BF16)


def _inproj(xs, g, w_in, layer, sgg, bd, pending_moe=None):
    T = sum(x.shape[0] for x in xs)
    D = xs[0].shape[1]
    out = jax.ShapeDtypeStruct((T, HALF), BF16)
    tile = pl.BlockSpec((TM, HALF), lambda i: (i, 0))
    full = pl.BlockSpec((TM, D), lambda i: (i, 0))
    moe_args, moe_specs, moe_out_specs, moe_out_shape = [], [], [], []
    if pending_moe is not None:
        ys, gates = pending_moe
        moe_args = [ys, ys, gates]
        moe_specs = [full, pl.BlockSpec((TM, D), lambda i: (i + T // TM, 0)),
                     pl.BlockSpec((TM, META_LANES), lambda i: (i, 0))]
        moe_out_specs = [full]
        moe_out_shape = [jax.ShapeDtypeStruct((T, D), F32)]
    return pl.pallas_call(
        functools.partial(_inproj_kernel, n_x=len(xs), first=xs[0].shape[0] // TM,
                          pending_moe=pending_moe is not None),
        grid=(T // TM,),
        in_specs=_stream_specs(xs) + moe_specs + [
            _const_spec((1, D)), _layer_spec(w_in, layer),
            _const_spec((1, HALF)), _const_spec((HALF, HALF))],
        out_specs=[pl.BlockSpec((2 * TM, HALF), lambda i: (i, 0))] + [tile] * 4 + moe_out_specs,
        out_shape=[jax.ShapeDtypeStruct((2 * T, HALF), BF16)] + [out] * 4 + moe_out_shape,
        compiler_params=_params(("parallel",)),
        name="inproj",
    )(*xs, *moe_args, g, w_in, sgg, bd)


def _row_plan(blk, n_blk):
    win_row0 = min(max(blk - 1, 0), n_blk - WIN_BLOCKS) * ROWS_PER_BLOCK
    plan = []
    for i in range(ROWS_PER_BLOCK):
        r = blk * ROWS_PER_BLOCK + i
        rs = min(max(r - WIN_ROWS // 2, 0), n_blk * ROWS_PER_BLOCK - WIN_ROWS)
        plan.append((rs - win_row0, rs - r + WIN_ROWS - 1))
    return plan


def _natten_kernel(lo_ref, hi_ref, q_ref, kwin, vwin, bias_ref, o_ref):
    b = pl.program_id(0)
    blk = b - lo_ref[b]
    n_blk = hi_ref[b] - lo_ref[b] + 1
    out_even = lax.broadcasted_iota(jnp.int32, (GRID_W, LANES), 1) < HEAD_DIM
    n_keys = WIN_ROWS * GRID_W

    def rows(plan):
        for i, (ws, variant) in enumerate(plan):
            scores = []
            for p in range(N_PAIRS):
                cols = slice(p * LANES, (p + 1) * LANES)
                s = lax.dot_general(q_ref[2 * i * GRID_W:2 * (i + 1) * GRID_W, cols],
                                    kwin[ws * GRID_W:ws * GRID_W + n_keys, cols],
                                    (((1,), (1,)), ((), ())), preferred_element_type=F32)
                scores.append(s + bias_ref[variant, p])
            for p in range(N_PAIRS):
                cols = slice(p * LANES, (p + 1) * LANES)
                s = scores[p]
                e = jnp.exp2(s - jnp.max(s, axis=-1, keepdims=True))
                denom = jnp.sum(e, axis=-1, keepdims=True)
                o2 = jnp.dot(e.astype(BF16), vwin[ws * GRID_W:ws * GRID_W + n_keys, cols],
                             preferred_element_type=F32) * (1.0 / denom)
                out = jnp.where(out_even, o2[:GRID_W], o2[GRID_W:])
                o_ref[i * GRID_W:(i + 1) * GRID_W, cols] = out.astype(BF16)

    model_blocks = WIN_BLOCKS + 2
    inner = _row_plan(1, model_blocks)
    assert all(_row_plan(k, model_blocks) == inner for k in range(1, model_blocks - 1))
    pl.when(blk == 0)(lambda: rows(_row_plan(0, model_blocks)))
    pl.when(blk == n_blk - 1)(lambda: rows(_row_plan(model_blocks - 1, model_blocks)))
    pl.when((blk > 0) & (blk < n_blk - 1))(lambda: rows(inner))


def _natten(q2, k, v, bias, blk_lo, blk_hi):
    T = v.shape[0]

    def win_map(b, lo, hi):
        return (jnp.clip(b - 1, lo[b], hi[b] - (WIN_BLOCKS - 1)) * TM, 0)

    window = pl.BlockSpec((pl.Element(WIN_BLOCKS * TM), pl.Element(HALF)), win_map)
    grid_spec = pltpu.PrefetchScalarGridSpec(
        num_scalar_prefetch=2,
        grid=(T // TM,),
        in_specs=[pl.BlockSpec((2 * TM, HALF), lambda b, lo, hi: (b, 0)), window, window,
                  pl.BlockSpec(bias.shape, lambda b, lo, hi: (0, 0, 0, 0),
                               pipeline_mode=pl.Buffered(1))],
        out_specs=pl.BlockSpec((TM, HALF), lambda b, lo, hi: (b, 0)),
    )
    return pl.pallas_call(
        _natten_kernel,
        grid_spec=grid_spec,
        out_shape=jax.ShapeDtypeStruct((T, HALF), BF16),
        compiler_params=_params(("parallel",)),
        name="natten",
    )(blk_lo, blk_hi, q2, k, v, bias)


def _na_bias_tables(rpb):
    n_layers = rpb.shape[0]
    qc = np.arange(GRID_W)
    kc = np.arange(GRID_W)
    cs = np.clip(qc - WIN_COLS // 2, 0, GRID_W - WIN_COLS)
    col_in = (kc[None, :] >= cs[:, None]) & (kc[None, :] < cs[:, None] + WIN_COLS)
    dc = np.clip(kc[None, :] - qc[:, None], -(WIN_COLS - 1), WIN_COLS - 1) + (WIN_COLS - 1)
    onehot = (dc.reshape(-1)[None, :] == np.arange(2 * WIN_COLS - 1)[:, None]).astype(np.float32)
    band = jnp.dot(rpb.astype(F32).reshape(-1, 2 * WIN_COLS - 1), onehot,
                   precision=lax.Precision.HIGHEST)
    band = band.reshape(n_layers * N_HEADS, 2 * WIN_ROWS - 1, GRID_W * GRID_W)
    oj = np.add.outer(np.arange(WIN_ROWS), np.arange(WIN_ROWS)).reshape(-1)
    row_onehot = (oj[:, None] == np.arange(2 * WIN_ROWS - 1)[None, :]).astype(np.float32)
    t = jnp.einsum("ad,bdc->bac", row_onehot, band, precision=lax.Precision.HIGHEST)
    t = t.reshape(n_layers, N_HEADS, WIN_ROWS, WIN_ROWS, GRID_W, GRID_W)
    t = jnp.where(col_in[None, None, None, None], t * LOG2_E, MASK_VALUE)
    t = jnp.transpose(t, (0, 2, 1, 4, 3, 5))
    return t.reshape(n_layers, WIN_ROWS, N_PAIRS, 2 * GRID_W, WIN_ROWS * GRID_W)


def _mixout_kernel(*refs, n_x, first, tail, n_chunks):
    a_ref, u_ref, sv_ref, sgw_ref, sgb_ref, g_ref, w_ref, gffn_ref = refs[n_x:n_x + 8]
    if tail == "ffn":
        wg_ref, wu_ref, wd_ref, o_ref, sg_s = refs[n_x + 8:]
    else:
        rw_ref, tri_ref, o_ref, idx_ref, gate_ref, cnt_ref, sg_s, carry = refs[n_x + 8:]
    even = lax.broadcasted_iota(jnp.int32, (SG_CHUNK, LANES), 1) < HEAD_DIM
    for c in range(TM // SG_CHUNK):
        rows = slice(c * SG_CHUNK, (c + 1) * SG_CHUNK)
        for p in range(N_PAIRS):
            cols = slice(p * LANES, (p + 1) * LANES)
            m2 = jnp.dot(sgw_ref[p], sv_ref[rows, cols], preferred_element_type=F32)
            mixed = jnp.where(even, m2[:SG_CHUNK], m2[SG_CHUNK:])
            sg_s[rows, cols] = u_ref[rows, cols].astype(F32) * (mixed + sgb_ref[:, cols])
    sg = sg_s[...]
    sgn = (sg * _rms_scale(sg) * g_ref[:, HALF:]).astype(BF16)
    a = a_ref[...].astype(F32)
    an = (a * _rms_scale(a) * g_ref[:, :HALF]).astype(BF16)
    y = jnp.dot(an, w_ref[:HALF, :], preferred_element_type=F32)
    y = y + jnp.dot(sgn, w_ref[HALF:, :], preferred_element_type=F32)
    x = _stream_tile(refs[:n_x], first) + y
    if tail == "ffn":
        h = (x * _rms_scale(x) * gffn_ref[...]).astype(BF16)
        o_ref[...] = x + _swiglu(h, wg_ref, wu_ref, wd_ref, n_chunks)
    else:
        o_ref[...] = x

        @pl.when(pl.program_id(0) == 0)
        def _():
            carry[...] = jnp.zeros_like(carry)

        _route_tile(x, gffn_ref, rw_ref, tri_ref, carry, idx_ref, gate_ref, cnt_ref)


def _route_tile(x, g_ref, rw_ref, tri_ref, carry, idx_ref, gate_ref, cnt_ref):
    h = x * _rms_scale(x) * g_ref[...]
    h_hi = h.astype(BF16)
    h_lo = (h - h_hi.astype(F32)).astype(BF16)
    prod = jnp.dot(jnp.concatenate([h_hi, h_lo], axis=0), rw_ref[...], preferred_element_type=F32)
    logits = (prod[:TM, :LANES] + prod[:TM, LANES:]) + (prod[TM:, :LANES] + prod[TM:, LANES:])
    col = lax.broadcasted_iota(jnp.int32, logits.shape, 1)
    colf = col.astype(F32)
    neg = -jnp.inf
    lg = jnp.where(col < N_EXPERTS, logits, neg)
    m1 = jnp.max(lg, axis=-1, keepdims=True)
    i1 = jnp.min(jnp.where(lg == m1, colf, float(LANES)), axis=-1, keepdims=True)
    sel1 = colf == i1
    lg2 = jnp.where(sel1, neg, lg)
    m2 = jnp.max(lg2, axis=-1, keepdims=True)
    i2 = jnp.min(jnp.where(lg2 == m2, colf, float(LANES)), axis=-1, keepdims=True)
    sel2 = colf == i2
    e2 = jnp.exp(m2 - m1)
    g1 = 1.0 / (1.0 + e2)
    g2 = e2 / (1.0 + e2)

    cnt = jnp.where(sel1 | sel2, 1.0, 0.0)
    before = jnp.dot(tri_ref[...], cnt.astype(BF16), preferred_element_type=F32) + carry[...]
    r1 = jnp.sum(jnp.where(sel1, before, 0.0), axis=-1, keepdims=True)
    r2 = jnp.sum(jnp.where(sel2, before, 0.0), axis=-1, keepdims=True)
    total = carry[...] + jnp.sum(cnt, axis=0, keepdims=True)
    carry[...] = total
    cnt_ref[...] = jnp.broadcast_to(total, cnt_ref.shape)

    meta = jnp.where(col == 0, i1, jnp.where(col == 1, i2, jnp.where(col == 2, r1, r2)))
    idx_ref[...] = meta[:, :META_LANES].astype(jnp.int32)
    gate_ref[...] = jnp.where(col == 0, g1, g2)[:, :META_LANES]


def _mixout(attn, u, svn, xs, sgw2, sgb_t, g_out, w_out, layer, g_ffn, ffn=None, route=None):
    T = sum(x.shape[0] for x in xs)
    D = xs[0].shape[1]
    half = pl.BlockSpec((TM, HALF), lambda i: (i, 0))
    full = pl.BlockSpec((TM, D), lambda i: (i, 0))
    meta = pl.BlockSpec((TM, META_LANES), lambda i: (i, 0))
    if ffn is not None:
        *tail_args, ffn_layer = ffn
        tail_specs = [_layer_spec(w, ffn_layer) for w in tail_args]
        n_chunks = _ff_chunks(tail_args[0].shape[-1])
    else:
        tail_args = list(route)
        tail_specs = [_const_spec(a.shape) for a in tail_args]
        n_chunks = None
    return pl.pallas_call(
        functools.partial(_mixout_kernel, n_x=len(xs), first=xs[0].shape[0] // TM,
                          tail="ffn" if ffn is not None else "route", n_chunks=n_chunks),
        grid=(T // TM,),
        in_specs=_stream_specs(xs) + [
            half, half, half, _const_spec(sgw2.shape), _const_spec(sgb_t.shape),
            _const_spec((1, 2 * HALF)), _layer_spec(w_out, layer), _const_spec((1, D))
        ] + tail_specs,
        out_specs=[full] + ([meta, meta, pl.BlockSpec((8, LANES), lambda i: (0, 0))] if route else []),
        out_shape=[jax.ShapeDtypeStruct((T, D), F32)] + (
            [jax.ShapeDtypeStruct((T, META_LANES), jnp.int32),
             jax.ShapeDtypeStruct((T, META_LANES), F32),
             jax.ShapeDtypeStruct((8, LANES), F32)] if route else []),
        scratch_shapes=[pltpu.VMEM((TM, HALF), F32)] + (
            [pltpu.VMEM((1, LANES), F32)] if route else []),
        compiler_params=_params(("arbitrary",) if route else ("parallel",)),
        name="mixout",
    )(*xs, attn, u, svn, sgw2, sgb_t, g_out, w_out, g_ffn, *tail_args)


def _ff_chunks(d_ff):
    n = 1
    while d_ff // n > 2048 or d_ff % n or (d_ff // n) % LANES:
        n += 1
    return n


def _swiglu(h, wg_ref, wu_ref, wd_ref, n_chunks):
    d_ff = wd_ref.shape[-2]
    fc = d_ff // n_chunks
    y = None
    for c in range(n_chunks):
        cols = slice(c * fc, (c + 1) * fc)
        gate = jnp.dot(h, wg_ref[:, cols], preferred_element_type=F32)
        up = jnp.dot(h, wu_ref[:, cols], preferred_element_type=F32)
        act = (gate / (1.0 + jnp.exp(-gate)) * up).astype(BF16)
        part = jnp.dot(act, wd_ref[cols, :], preferred_element_type=F32)
        y = part if y is None else y + part
    return y


def _invert_kernel(lo_ref, hi_ref, dest0_ref, dest1_ref, out_ref):
    i = pl.program_id(0)

    @pl.when(i == 0)
    def _():
        def clear(j, carry):
            out_ref[j] = 0
            return carry

        for r in range(N_EXPERTS + 1):
            lax.fori_loop(lo_ref[r], hi_ref[r], clear, 0)

    first = 2 * i * INVERT_CHUNK + 1

    def place(t, carry):
        out_ref[dest0_ref[t]] = first + 2 * t
        out_ref[dest1_ref[t]] = first + 2 * t + 1
        return carry

    lax.fori_loop(0, INVERT_CHUNK, place, 0, unroll=8)


def _invert(dest0, dest1, empty_lo, empty_hi, n_rows):
    n = dest0.shape[0]
    assert n % INVERT_CHUNK == 0
    chunk = pl.BlockSpec((INVERT_CHUNK,), lambda i, lo, hi: (i,), memory_space=pltpu.SMEM)
    grid_spec = pltpu.PrefetchScalarGridSpec(
        num_scalar_prefetch=2,
        grid=(n // INVERT_CHUNK,),
        in_specs=[chunk, chunk],
        out_specs=pl.BlockSpec((n_rows,), lambda i, lo, hi: (0,), memory_space=pltpu.SMEM),
    )
    return pl.pallas_call(
        _invert_kernel,
        grid_spec=grid_spec,
        out_shape=jax.ShapeDtypeStruct((n_rows,), jnp.int32),
        compiler_params=_params(("arbitrary",)),
        name="invert",
    )(empty_lo, empty_hi, dest0, dest1)


def _experts_kernel(te_ref, src_ref, dst_ref, x_hbm, g_ref, wg_ref, wu_ref, wd_ref, ys_hbm,
                    xbuf, obuf, gsem, ssem, *, n_chunks):
    del te_ref
    i = pl.program_id(0)
    slot = i % 2

    def gather_wait(s):
        pltpu.make_async_copy(x_hbm.at[pl.ds(0, EXPERT_TILE)], xbuf.at[s], gsem.at[s]).wait()

    @pl.when(i == 0)
    def _():
        xbuf[...] = jnp.zeros_like(xbuf)
        obuf[...] = jnp.zeros_like(obuf)

    @pl.when(i >= 1)
    def _():
        gather_wait(1 - slot)

    x = xbuf[1 - slot]
    h = (x * _rms_scale(x) * g_ref[...]).astype(BF16)
    for t in range(EXPERT_TILE):
        pltpu.make_async_copy(x_hbm.at[pl.ds(src_ref[t], 1)], xbuf.at[slot, pl.ds(t, 1)],
                              gsem.at[slot]).start()
        pltpu.make_async_copy(obuf.at[slot, pl.ds(t, 1)], ys_hbm.at[pl.ds(dst_ref[t], 1)],
                              ssem).start()
    y = _swiglu(h, wg_ref.at[0], wu_ref.at[0], wd_ref.at[0], n_chunks)
    pltpu.make_async_copy(obuf.at[slot], ys_hbm.at[pl.ds(0, EXPERT_TILE)], ssem).wait()
    obuf[1 - slot] = y

    @pl.when(i == pl.num_programs(0) - 1)
    def _():
        gather_wait(slot)


def _experts(x, g, src_tok, dst_row, tile_expert, wg, wu, wd, layer):
    T, D = x.shape
    d_ff = wg.shape[-1]
    n_tiles = src_tok.shape[0] // EXPERT_TILE
    n_rows = dst_row.shape[0]

    def w_spec(w):
        return pl.BlockSpec((None, 1) + w.shape[2:],
                            lambda i, te: (layer, te[jnp.clip(i - 1, 0, n_tiles - 1)], 0, 0),
                            pipeline_mode=pl.Buffered(1))

    grid_spec = pltpu.PrefetchScalarGridSpec(
        num_scalar_prefetch=1,
        grid=(n_tiles + 2,),
        in_specs=[pl.BlockSpec((EXPERT_TILE,), lambda i, te: (jnp.minimum(i, n_tiles - 1),),
                               memory_space=pltpu.SMEM),
                  pl.BlockSpec((EXPERT_TILE,), lambda i, te: (jnp.maximum(i - 1, 0),),
                               memory_space=pltpu.SMEM),
                  pl.BlockSpec(memory_space=pl.ANY),
                  pl.BlockSpec((1, D), lambda i, te: (0, 0), pipeline_mode=pl.Buffered(1)),
                  w_spec(wg), w_spec(wu), w_spec(wd)],
        out_specs=pl.BlockSpec(memory_space=pl.ANY),
        scratch_shapes=[pltpu.VMEM((2, EXPERT_TILE, D), F32), pltpu.VMEM((2, EXPERT_TILE, D), F32),
                        pltpu.SemaphoreType.DMA((2,)), pltpu.SemaphoreType.DMA(())],
    )
    return pl.pallas_call(
        functools.partial(_experts_kernel, n_chunks=_ff_chunks(d_ff)),
        grid_spec=grid_spec,
        out_shape=jax.ShapeDtypeStruct((n_rows, D), F32),
        compiler_params=_params(("arbitrary",)),
        name="experts",
    )(tile_expert, src_tok, dst_row, x, g, wg, wu, wd)


def _combine_kernel(x_ref, gate_ref, g_ref, y0_ref, y1_ref, o_ref, *, final_norm):
    gates = gate_ref[...]
    y = x_ref[...] + (y0_ref[...] * gates[:, 0:1] + y1_ref[...] * gates[:, 1:2])
    if final_norm:
        y = y * _rms_scale(y) * g_ref[...]
    o_ref[...] = y


def _combine(x, gates, ys, g_final, final_norm, first_block, n_blocks):
    T, D = x.shape
    rows = lambda i: (i + first_block, 0)
    return pl.pallas_call(
        functools.partial(_combine_kernel, final_norm=final_norm),
        grid=(n_blocks,),
        in_specs=[pl.BlockSpec((TM, D), rows), pl.BlockSpec((TM, META_LANES), rows), _const_spec((1, D)),
                  pl.BlockSpec((TM, D), rows),
                  pl.BlockSpec((TM, D), lambda i: (i + first_block + T // TM, 0))],
        out_specs=pl.BlockSpec((TM, D), lambda i: (i, 0)),
        out_shape=jax.ShapeDtypeStruct((n_blocks * TM, D), F32),
        compiler_params=_params(("parallel",)),
        name="combine",
    )(x, gates, g_final, ys, ys)


def _final_norm_kernel(x_ref, g_ref, o_ref):
    x = x_ref[...]
    o_ref[...] = x * _rms_scale(x) * g_ref[...]


def _final_norm(x, g):
    T, D = x.shape
    full = pl.BlockSpec((TM, D), lambda i: (i, 0))
    return pl.pallas_call(
        _final_norm_kernel,
        grid=(T // TM,),
        in_specs=[full, _const_spec((1, D))],
        out_specs=full,
        out_shape=jax.ShapeDtypeStruct((T, D), F32),
        compiler_params=_params(("parallel",)),
        name="final_norm",
    )(x, g)


def _route_operands(router_w):
    D = router_w.shape[0]
    rw_pad = jnp.zeros((D, LANES), F32).at[:, :N_EXPERTS].set(router_w)
    rw_hi = rw_pad.astype(BF16)
    rw2 = jnp.concatenate([rw_hi, (rw_pad - rw_hi.astype(F32)).astype(BF16)], axis=1)
    tri = jnp.asarray(np.tril(np.ones((TM, TM), np.float32), -1), BF16)
    return rw2, tri


def _moe(x, routing, g_ffn, wg, wu, wd, layer, g_final, final_norm, out_blocks):
    T, D = x.shape
    idx, gates, cnt = routing

    counts = cnt[0, :N_EXPERTS].astype(jnp.int32)
    padded = (counts + EXPERT_TILE - 1) // EXPERT_TILE * EXPERT_TILE
    pad_end = jnp.cumsum(padded)
    pad_start = pad_end - padded
    dest0 = pad_start[idx[:, 0]] + idx[:, 2]
    dest1 = pad_start[idx[:, 1]] + idx[:, 3]
    n_rows = 2 * T + N_EXPERTS * EXPERT_TILE
    n_tiles = n_rows // EXPERT_TILE
    tile_start = jnp.arange(n_tiles, dtype=jnp.int32) * EXPERT_TILE
    tile_expert = jnp.minimum(
        jnp.sum(pad_end[None, :] <= tile_start[:, None], axis=1), N_EXPERTS - 1).astype(jnp.int32)

    empty_lo = jnp.concatenate([pad_start + counts, pad_end[-1:]]).astype(jnp.int32)
    empty_hi = jnp.concatenate([pad_end, jnp.full((1,), n_rows)]).astype(jnp.int32)
    filled = _invert(dest0, dest1, empty_lo, empty_hi, n_rows)
    is_pad = filled == 0
    pad_rank = jnp.cumsum(is_pad.astype(jnp.int32)) - 1
    assign = filled - 1
    src_tok = jnp.where(is_pad, 0, assign // 2)
    dst_row = jnp.where(is_pad, 2 * T + pad_rank, (assign % 2) * T + assign // 2)
    spare = n_rows + jnp.arange(EXPERT_TILE, dtype=jnp.int32)
    dst_row = jnp.concatenate([spare, dst_row])

    ys = _experts(x, g_ffn, src_tok, dst_row, tile_expert, wg, wu, wd, layer)
    if out_blocks is None:
        return ys, gates
    return [_combine(x, gates, ys, g_final, final_norm, b0, nb) for b0, nb in out_blocks]


def _image_blocks(image_rows):
    lo, hi, start = [], [], 0
    for rows in image_rows:
        n = rows // ROWS_PER_BLOCK
        lo += [start] * n
        hi += [start + n - 1] * n
        start += n
    return jnp.asarray(lo, jnp.int32), jnp.asarray(hi, jnp.int32)


def kernel(x_prompt, x_sample, norm_mix_g, w_in, na_rpb, sg_norm_g, sg_w, sg_b, out_norm_g, w_out,
           norm_ffn_g, dense_w_gate, dense_w_up, dense_w_down,
           router_w, moe_w_gate, moe_w_up, moe_w_down, final_norm_g):
    D = x_prompt.shape[-1]
    depth = w_in.shape[0]
    assert D == 2 * HALF
    image_rows = []
    for xs in (x_prompt, x_sample):
        assert xs.shape[1] % TM == 0 and xs.shape[1] >= WIN_BLOCKS * TM
        image_rows += [xs.shape[1] // GRID_W] * xs.shape[0]
    blk_lo, blk_hi = _image_blocks(image_rows)
    n_prompt = x_prompt.shape[0] * x_prompt.shape[1]

    xs = (x_prompt.reshape(-1, D), x_sample.reshape(-1, D))
    bd = jnp.asarray(np.kron(np.eye(N_HEADS, dtype=np.float32),
                             np.ones((HEAD_DIM, HEAD_DIM), np.float32)), BF16)
    g_final = final_norm_g.reshape(1, D)
    na_bias = _na_bias_tables(na_rpb)

    w_in, w_out, dense_w_gate, dense_w_up, dense_w_down, moe_w_gate, moe_w_up, moe_w_down = (
        w.astype(BF16) for w in (w_in, w_out, dense_w_gate, dense_w_up, dense_w_down,
                                 moe_w_gate, moe_w_up, moe_w_down))

    pending_moe = None
    for l in range(depth):
        q, k, v, u, svn, *x_new = _inproj(xs, norm_mix_g[l].reshape(1, D), w_in, l,
                                          sg_norm_g[l].reshape(1, HALF), bd, pending_moe)
        if pending_moe is not None:
            xs, pending_moe = (x_new[0],), None
        attn = _natten(q, k, v, na_bias[l], blk_lo, blk_hi)
        sgw2 = sg_w[l].astype(BF16).reshape(N_PAIRS, 2 * SG_CHUNK, SG_CHUNK)
        sgb_t = jnp.repeat(jnp.transpose(sg_b[l]), HEAD_DIM, axis=1)
        g_ffn = norm_ffn_g[l].reshape(1, D)
        i = l // 2
        last = l == depth - 1
        dense = l % 2 == 0
        x, *routing = _mixout(attn, u, svn, xs, sgw2, sgb_t, out_norm_g[l].reshape(1, D), w_out, l,
                              g_ffn,
                              ffn=(dense_w_gate, dense_w_up, dense_w_down, i) if dense else None,
                              route=None if dense else _route_operands(router_w[i]))
        if dense:
            if last:
                x = _final_norm(x, g_final)
                outs = [x[:n_prompt], x[n_prompt:]]
        elif last:
            nb, npb = x.shape[0] // TM, n_prompt // TM
            outs = _moe(x, routing, g_ffn, moe_w_gate, moe_w_up, moe_w_down, i, g_final,
                        final_norm=True, out_blocks=[(0, npb), (npb, nb - npb)])
        else:
            pending_moe = _moe(x, routing, g_ffn, moe_w_gate, moe_w_up, moe_w_down, i, g_final,
                               final_norm=False, out_blocks=None)
        xs = (x,)
    return (outs[0].reshape(x_prompt.shape), outs[1].reshape(x_sample.shape))
```

```python
import functools

import numpy as np
import jax
import jax.numpy as jnp
from jax import lax
from jax.experimental import pallas as pl
from jax.experimental.pallas import tpu as pltpu

F32 = jnp.float32
BF16 = jnp.bfloat16

GRID_W = 64
WIN_ROWS = 8
WIN_COLS = 16
N_HEADS = 8
HEAD_DIM = 64
HALF = N_HEADS * HEAD_DIM
N_PAIRS = N_HEADS // 2
LANES = 128
SG_CHUNK = 128
N_EXPERTS = 8
RMS_EPS = 1e-6
MASK_VALUE = -1e30

TM = 512
ROWS_PER_BLOCK = TM // GRID_W
WIN_BLOCKS = 3
LOG2_E = 1.4426950408889634
EXPERT_TILE = 512
INVERT_CHUNK = 1024
VMEM_LIMIT = 56 * 1024 * 1024


def _params(sem, vmem=VMEM_LIMIT):
    return pltpu.CompilerParams(dimension_semantics=sem, vmem_limit_bytes=vmem)


def _const_spec(shape):
    nd = len(shape)
    return pl.BlockSpec(shape, lambda *_: (0,) * nd, pipeline_mode=pl.Buffered(1))


def _layer_spec(w, layer):
    nd = w.ndim - 1
    return pl.BlockSpec((None,) + w.shape[1:], lambda *_: (layer,) + (0,) * nd,
                        pipeline_mode=pl.Buffered(1))


def _rms_scale(x):
    return lax.rsqrt(jnp.mean(x * x, axis=-1, keepdims=True) + RMS_EPS)


def _stream_specs(parts):
    D = parts[0].shape[1]
    if len(parts) == 1:
        return [pl.BlockSpec((TM, D), lambda i: (i, 0))]
    first = parts[0].shape[0] // TM
    return [pl.BlockSpec((TM, D), lambda i: (jnp.minimum(i, first - 1), 0)),
            pl.BlockSpec((TM, D), lambda i: (jnp.maximum(i - first, 0), 0))]


def _stream_tile(refs, first):
    if len(refs) == 1:
        return refs[0][...]
    return jnp.where(pl.program_id(0) < first, refs[0][...], refs[1][...])


def _gelu_tanh(x):
    return x * (0.5 * (1.0 + jnp.tanh(0.7978845608028654 * (x + 0.044715 * (x * x * x)))))


def _inproj_kernel(*refs, n_x, first, pending_moe):
    x = _stream_tile(refs[:n_x], first)
    if pending_moe:
        y0_ref, y1_ref, gate_ref = refs[n_x:n_x + 3]
        g_ref, w_ref, sgg_ref, bd_ref, q_ref, k_ref, v_ref, u_ref, sv_ref, x_out_ref = refs[n_x + 3:]
        gates = gate_ref[...]
        x = x + (y0_ref[...] * gates[:, 0:1] + y1_ref[...] * gates[:, 1:2])
        x_out_ref[...] = x
    else:
        g_ref, w_ref, sgg_ref, bd_ref, q_ref, k_ref, v_ref, u_ref, sv_ref = refs[n_x:]
    h = (x * _rms_scale(x) * g_ref[...]).astype(BF16)

    def proj(n):
        return jnp.dot(h, w_ref[:, n * HALF:(n + 1) * HALF], preferred_element_type=F32)

    u_ref[...] = _gelu_tanh(proj(3)).astype(BF16)
    sv = _gelu_tanh(proj(4))
    ssq = jnp.dot((sv * sv).astype(BF16), bd_ref[...], preferred_element_type=F32)
    sv_ref[...] = (sv * lax.rsqrt(ssq * (1.0 / HEAD_DIM) + RMS_EPS) * sgg_ref[...]).astype(BF16)

    q = proj(0) * (HEAD_DIM ** -0.5 * LOG2_E)
    is_even = (lax.broadcasted_iota(jnp.int32, q.shape, 1) % LANES) < HEAD_DIM
    qe = jnp.where(is_even, q, 0.0).astype(BF16)
    qo = jnp.where(is_even, 0.0, q).astype(BF16)
    for r in range(ROWS_PER_BLOCK):
        rows = slice(r * GRID_W, (r + 1) * GRID_W)
        q_ref[2 * r * GRID_W:(2 * r + 1) * GRID_W, :] = qe[rows]
        q_ref[(2 * r + 1) * GRID_W:(2 * r + 2) * GRID_W, :] = qo[rows]
    k_ref[...] = proj(1).astype(BF16)
    v_ref[...] = proj(2).astype(BF16)


def _inproj(xs, g, w_in, layer, sgg, bd, pending_moe=None):
    T = sum(x.shape[0] for x in xs)
    D = xs[0].shape[1]
    out = jax.ShapeDtypeStruct((T, HALF), BF16)
    tile = pl.BlockSpec((TM, HALF), lambda i: (i, 0))
    full = pl.BlockSpec((TM, D), lambda i: (i, 0))
    moe_args, moe_specs, moe_out_specs, moe_out_shape = [], [], [], []
    if pending_moe is not None:
        ys, gates = pending_moe
        moe_args = [ys, ys, gates]
        moe_specs = [full, pl.BlockSpec((TM, D), lambda i: (i + T // TM, 0)),
                     pl.BlockSpec((TM, LANES), lambda i: (i, 0))]
        moe_out_specs = [full]
        moe_out_shape = [jax.ShapeDtypeStruct((T, D), F32)]
    return pl.pallas_call(
        functools.partial(_inproj_kernel, n_x=len(xs), first=xs[0].shape[0] // TM,
                          pending_moe=pending_moe is not None),
        grid=(T // TM,),
        in_specs=_stream_specs(xs) + moe_specs + [
            _const_spec((1, D)), _layer_spec(w_in, layer),
            _const_spec((1, HALF)), _const_spec((HALF, HALF))],
        out_specs=[pl.BlockSpec((2 * TM, HALF), lambda i: (i, 0))] + [tile] * 4 + moe_out_specs,
        out_shape=[jax.ShapeDtypeStruct((2 * T, HALF), BF16)] + [out] * 4 + moe_out_shape,
        compiler_params=_params(("parallel",)),
        name="inproj",
    )(*xs, *moe_args, g, w_in, sgg, bd)


def _row_plan(blk, n_blk):
    win_row0 = min(max(blk - 1, 0), n_blk - WIN_BLOCKS) * ROWS_PER_BLOCK
    plan = []
    for i in range(ROWS_PER_BLOCK):
        r = blk * ROWS_PER_BLOCK + i
        rs = min(max(r - WIN_ROWS // 2, 0), n_blk * ROWS_PER_BLOCK - WIN_ROWS)
        plan.append((rs - win_row0, rs - r + WIN_ROWS - 1))
    return plan


def _natten_kernel(lo_ref, hi_ref, q_ref, kwin, vwin, bias_ref, o_ref):
    b = pl.program_id(0)
    blk = b - lo_ref[b]
    n_blk = hi_ref[b] - lo_ref[b] + 1
    out_even = lax.broadcasted_iota(jnp.int32, (GRID_W, LANES), 1) < HEAD_DIM
    n_keys = WIN_ROWS * GRID_W

    def rows(plan):
        for i, (ws, variant) in enumerate(plan):
            scores = []
            for p in range(N_PAIRS):
                cols = slice(p * LANES, (p + 1) * LANES)
                s = lax.dot_general(q_ref[2 * i * GRID_W:2 * (i + 1) * GRID_W, cols],
                                    kwin[ws * GRID_W:ws * GRID_W + n_keys, cols],
                                    (((1,), (1,)), ((), ())), preferred_element_type=F32)
                scores.append(s + bias_ref[variant, p])
            for p in range(N_PAIRS):
                cols = slice(p * LANES, (p + 1) * LANES)
                s = scores[p]
                e = jnp.exp2(s - jnp.max(s, axis=-1, keepdims=True))
                denom = jnp.sum(e, axis=-1, keepdims=True)
                o2 = jnp.dot(e.astype(BF16), vwin[ws * GRID_W:ws * GRID_W + n_keys, cols],
                             preferred_element_type=F32) * (1.0 / denom)
                out = jnp.where(out_even, o2[:GRID_W], o2[GRID_W:])
                o_ref[i * GRID_W:(i + 1) * GRID_W, cols] = out.astype(BF16)

    model_blocks = WIN_BLOCKS + 2
    inner = _row_plan(1, model_blocks)
    assert all(_row_plan(k, model_blocks) == inner for k in range(1, model_blocks - 1))
    pl.when(blk == 0)(lambda: rows(_row_plan(0, model_blocks)))
    pl.when(blk == n_blk - 1)(lambda: rows(_row_plan(model_blocks - 1, model_blocks)))
    pl.when((blk > 0) & (blk < n_blk - 1))(lambda: rows(inner))


def _natten(q2, k, v, bias, blk_lo, blk_hi):
    T = v.shape[0]

    def win_map(b, lo, hi):
        return (jnp.clip(b - 1, lo[b], hi[b] - (WIN_BLOCKS - 1)) * TM, 0)

    window = pl.BlockSpec((pl.Element(WIN_BLOCKS * TM), pl.Element(HALF)), win_map)
    grid_spec = pltpu.PrefetchScalarGridSpec(
        num_scalar_prefetch=2,
        grid=(T // TM,),
        in_specs=[pl.BlockSpec((2 * TM, HALF), lambda b, lo, hi: (b, 0)), window, window,
                  pl.BlockSpec(bias.shape, lambda b, lo, hi: (0, 0, 0, 0),
                               pipeline_mode=pl.Buffered(1))],
        out_specs=pl.BlockSpec((TM, HALF), lambda b, lo, hi: (b, 0)),
    )
    return pl.pallas_call(
        _natten_kernel,
        grid_spec=grid_spec,
        out_shape=jax.ShapeDtypeStruct((T, HALF), BF16),
        compiler_params=_params(("parallel",)),
        name="natten",
    )(blk_lo, blk_hi, q2, k, v, bias)


def _na_bias_tables(rpb):
    n_layers = rpb.shape[0]
    qc = np.arange(GRID_W)
    kc = np.arange(GRID_W)
    cs = np.clip(qc - WIN_COLS // 2, 0, GRID_W - WIN_COLS)
    col_in = (kc[None, :] >= cs[:, None]) & (kc[None, :] < cs[:, None] + WIN_COLS)
    dc = np.clip(kc[None, :] - qc[:, None], -(WIN_COLS - 1), WIN_COLS - 1) + (WIN_COLS - 1)
    onehot = (dc.reshape(-1)[None, :] == np.arange(2 * WIN_COLS - 1)[:, None]).astype(np.float32)
    band = jnp.dot(rpb.astype(F32).reshape(-1, 2 * WIN_COLS - 1), onehot,
                   precision=lax.Precision.HIGHEST)
    band = band.reshape(n_layers, N_HEADS, 2 * WIN_ROWS - 1, GRID_W, GRID_W)
    band = jnp.where(col_in[None, None, None], band * LOG2_E, MASK_VALUE)
    band = jnp.transpose(band, (0, 1, 3, 2, 4))
    t = jnp.stack([band[:, :, :, o:o + WIN_ROWS, :] for o in range(WIN_ROWS)], axis=1)
    return t.reshape(n_layers, WIN_ROWS, N_PAIRS, 2 * GRID_W, WIN_ROWS * GRID_W)


def _mixout_kernel(*refs, n_x, first, tail, n_chunks):
    a_ref, u_ref, sv_ref, sgw_ref, sgb_ref, g_ref, w_ref, gffn_ref = refs[n_x:n_x + 8]
    if tail == "ffn":
        wg_ref, wu_ref, wd_ref, o_ref, sg_s = refs[n_x + 8:]
    else:
        rw_ref, tri_ref, o_ref, idx_ref, gate_ref, cnt_ref, sg_s, carry = refs[n_x + 8:]
    even = lax.broadcasted_iota(jnp.int32, (SG_CHUNK, LANES), 1) < HEAD_DIM
    for c in range(TM // SG_CHUNK):
        rows = slice(c * SG_CHUNK, (c + 1) * SG_CHUNK)
        for p in range(N_PAIRS):
            cols = slice(p * LANES, (p + 1) * LANES)
            m2 = jnp.dot(sgw_ref[p], sv_ref[rows, cols], preferred_element_type=F32)
            mixed = jnp.where(even, m2[:SG_CHUNK], m2[SG_CHUNK:])
            sg_s[rows, cols] = u_ref[rows, cols].astype(F32) * (mixed + sgb_ref[:, cols])
    sg = sg_s[...]
    sgn = (sg * _rms_scale(sg) * g_ref[:, HALF:]).astype(BF16)
    a = a_ref[...].astype(F32)
    an = (a * _rms_scale(a) * g_ref[:, :HALF]).astype(BF16)
    y = jnp.dot(an, w_ref[:HALF, :], preferred_element_type=F32)
    y = y + jnp.dot(sgn, w_ref[HALF:, :], preferred_element_type=F32)
    x = _stream_tile(refs[:n_x], first) + y
    if tail == "ffn":
        h = (x * _rms_scale(x) * gffn_ref[...]).astype(BF16)
        o_ref[...] = x + _swiglu(h, wg_ref, wu_ref, wd_ref, n_chunks)
    else:
        o_ref[...] = x

        @pl.when(pl.program_id(0) == 0)
        def _():
            carry[...] = jnp.zeros_like(carry)

        _route_tile(x, gffn_ref, rw_ref, tri_ref, carry, idx_ref, gate_ref, cnt_ref)


def _route_tile(x, g_ref, rw_ref, tri_ref, carry, idx_ref, gate_ref, cnt_ref):
    h = x * _rms_scale(x) * g_ref[...]
    h_hi = h.astype(BF16)
    h_lo = (h - h_hi.astype(F32)).astype(BF16)
    prod = jnp.dot(jnp.concatenate([h_hi, h_lo], axis=0), rw_ref[...], preferred_element_type=F32)
    logits = (prod[:TM, :LANES] + prod[:TM, LANES:]) + (prod[TM:, :LANES] + prod[TM:, LANES:])
    col = lax.broadcasted_iota(jnp.int32, logits.shape, 1)
    colf = col.astype(F32)
    neg = -jnp.inf
    lg = jnp.where(col < N_EXPERTS, logits, neg)
    m1 = jnp.max(lg, axis=-1, keepdims=True)
    i1 = jnp.min(jnp.where(lg == m1, colf, float(LANES)), axis=-1, keepdims=True)
    sel1 = colf == i1
    lg2 = jnp.where(sel1, neg, lg)
    m2 = jnp.max(lg2, axis=-1, keepdims=True)
    i2 = jnp.min(jnp.where(lg2 == m2, colf, float(LANES)), axis=-1, keepdims=True)
    sel2 = colf == i2
    e2 = jnp.exp(m2 - m1)
    g1 = 1.0 / (1.0 + e2)
    g2 = e2 / (1.0 + e2)

    cnt = jnp.where(sel1 | sel2, 1.0, 0.0)
    before = jnp.dot(tri_ref[...], cnt.astype(BF16), preferred_element_type=F32) + carry[...]
    r1 = jnp.sum(jnp.where(sel1, before, 0.0), axis=-1, keepdims=True)
    r2 = jnp.sum(jnp.where(sel2, before, 0.0), axis=-1, keepdims=True)
    total = carry[...] + jnp.sum(cnt, axis=0, keepdims=True)
    carry[...] = total
    cnt_ref[...] = jnp.broadcast_to(total, cnt_ref.shape)

    meta = jnp.where(col == 0, i1, jnp.where(col == 1, i2, jnp.where(col == 2, r1, r2)))
    idx_ref[...] = meta.astype(jnp.int32)
    gate_ref[...] = jnp.where(col == 0, g1, g2)


def _mixout(attn, u, svn, xs, sgw2, sgb_t, g_out, w_out, layer, g_ffn, ffn=None, route=None):
    T = sum(x.shape[0] for x in xs)
    D = xs[0].shape[1]
    half = pl.BlockSpec((TM, HALF), lambda i: (i, 0))
    full = pl.BlockSpec((TM, D), lambda i: (i, 0))
    meta = pl.BlockSpec((TM, LANES), lambda i: (i, 0))
    if ffn is not None:
        *tail_args, ffn_layer = ffn
        tail_specs = [_layer_spec(w, ffn_layer) for w in tail_args]
        n_chunks = _ff_chunks(tail_args[0].shape[-1])
    else:
        tail_args = list(route)
        tail_specs = [_const_spec(a.shape) for a in tail_args]
        n_chunks = None
    return pl.pallas_call(
        functools.partial(_mixout_kernel, n_x=len(xs), first=xs[0].shape[0] // TM,
                          tail="ffn" if ffn is not None else "route", n_chunks=n_chunks),
        grid=(T // TM,),
        in_specs=_stream_specs(xs) + [
            half, half, half, _const_spec(sgw2.shape), _const_spec(sgb_t.shape),
            _const_spec((1, 2 * HALF)), _layer_spec(w_out, layer), _const_spec((1, D))
        ] + tail_specs,
        out_specs=[full] + ([meta, meta, pl.BlockSpec((8, LANES), lambda i: (0, 0))] if route else []),
        out_shape=[jax.ShapeDtypeStruct((T, D), F32)] + (
            [jax.ShapeDtypeStruct((T, LANES), jnp.int32), jax.ShapeDtypeStruct((T, LANES), F32),
             jax.ShapeDtypeStruct((8, LANES), F32)] if route else []),
        scratch_shapes=[pltpu.VMEM((TM, HALF), F32)] + (
            [pltpu.VMEM((1, LANES), F32)] if route else []),
        compiler_params=_params(("arbitrary",) if route else ("parallel",)),
        name="mixout",
    )(*xs, attn, u, svn, sgw2, sgb_t, g_out, w_out, g_ffn, *tail_args)


def _ff_chunks(d_ff):
    n = 1
    while d_ff // n > 2048 or d_ff % n or (d_ff // n) % LANES:
        n += 1
    return n


def _swiglu(h, wg_ref, wu_ref, wd_ref, n_chunks):
    d_ff = wd_ref.shape[-2]
    fc = d_ff // n_chunks
    y = None
    for c in range(n_chunks):
        cols = slice(c * fc, (c + 1) * fc)
        gate = jnp.dot(h, wg_ref[:, cols], preferred_element_type=F32)
        up = jnp.dot(h, wu_ref[:, cols], preferred_element_type=F32)
        act = (gate / (1.0 + jnp.exp(-gate)) * up).astype(BF16)
        part = jnp.dot(act, wd_ref[cols, :], preferred_element_type=F32)
        y = part if y is None else y + part
    return y


def _invert_kernel(lo_ref, hi_ref, dest0_ref, dest1_ref, out_ref):
    i = pl.program_id(0)

    @pl.when(i == 0)
    def _():
        def clear(j, carry):
            out_ref[j] = 0
            return carry

        for r in range(N_EXPERTS + 1):
            lax.fori_loop(lo_ref[r], hi_ref[r], clear, 0)

    first = 2 * i * INVERT_CHUNK + 1

    def place(t, carry):
        out_ref[dest0_ref[t]] = first + 2 * t
        out_ref[dest1_ref[t]] = first + 2 * t + 1
        return carry

    lax.fori_loop(0, INVERT_CHUNK, place, 0, unroll=8)


def _invert(dest0, dest1, empty_lo, empty_hi, n_rows):
    n = dest0.shape[0]
    assert n % INVERT_CHUNK == 0
    chunk = pl.BlockSpec((INVERT_CHUNK,), lambda i, lo, hi: (i,), memory_space=pltpu.SMEM)
    grid_spec = pltpu.PrefetchScalarGridSpec(
        num_scalar_prefetch=2,
        grid=(n // INVERT_CHUNK,),
        in_specs=[chunk, chunk],
        out_specs=pl.BlockSpec((n_rows,), lambda i, lo, hi: (0,), memory_space=pltpu.SMEM),
    )
    return pl.pallas_call(
        _invert_kernel,
        grid_spec=grid_spec,
        out_shape=jax.ShapeDtypeStruct((n_rows,), jnp.int32),
        compiler_params=_params(("arbitrary",)),
        name="invert",
    )(empty_lo, empty_hi, dest0, dest1)


def _experts_kernel(te_ref, src_ref, dst_ref, x_hbm, g_ref, wg_ref, wu_ref, wd_ref, ys_hbm,
                    xbuf, obuf, gsem, ssem, *, n_chunks):
    del te_ref
    i = pl.program_id(0)
    slot = i % 2

    def gather_wait(s):
        pltpu.make_async_copy(x_hbm.at[pl.ds(0, EXPERT_TILE)], xbuf.at[s], gsem.at[s]).wait()

    @pl.when(i == 0)
    def _():
        xbuf[...] = jnp.zeros_like(xbuf)
        obuf[...] = jnp.zeros_like(obuf)

    @pl.when(i >= 1)
    def _():
        gather_wait(1 - slot)

    x = xbuf[1 - slot]
    h = (x * _rms_scale(x) * g_ref[...]).astype(BF16)
    for t in range(EXPERT_TILE):
        pltpu.make_async_copy(x_hbm.at[pl.ds(src_ref[t], 1)], xbuf.at[slot, pl.ds(t, 1)],
                              gsem.at[slot]).start()
        pltpu.make_async_copy(obuf.at[slot, pl.ds(t, 1)], ys_hbm.at[pl.ds(dst_ref[t], 1)],
                              ssem).start()
    y = _swiglu(h, wg_ref.at[0], wu_ref.at[0], wd_ref.at[0], n_chunks)
    pltpu.make_async_copy(obuf.at[slot], ys_hbm.at[pl.ds(0, EXPERT_TILE)], ssem).wait()
    obuf[1 - slot] = y

    @pl.when(i == pl.num_programs(0) - 1)
    def _():
        gather_wait(slot)


def _experts(x, g, src_tok, dst_row, tile_expert, wg, wu, wd, layer):
    T, D = x.shape
    d_ff = wg.shape[-1]
    n_tiles = src_tok.shape[0] // EXPERT_TILE
    n_rows = dst_row.shape[0]

    def w_spec(w):
        return pl.BlockSpec((None, 1) + w.shape[2:],
                            lambda i, te: (layer, te[jnp.clip(i - 1, 0, n_tiles - 1)], 0, 0),
                            pipeline_mode=pl.Buffered(1))

    grid_spec = pltpu.PrefetchScalarGridSpec(
        num_scalar_prefetch=1,
        grid=(n_tiles + 2,),
        in_specs=[pl.BlockSpec((EXPERT_TILE,), lambda i, te: (jnp.minimum(i, n_tiles - 1),),
                               memory_space=pltpu.SMEM),
                  pl.BlockSpec((EXPERT_TILE,), lambda i, te: (jnp.maximum(i - 1, 0),),
                               memory_space=pltpu.SMEM),
                  pl.BlockSpec(memory_space=pl.ANY),
                  pl.BlockSpec((1, D), lambda i, te: (0, 0), pipeline_mode=pl.Buffered(1)),
                  w_spec(wg), w_spec(wu), w_spec(wd)],
        out_specs=pl.BlockSpec(memory_space=pl.ANY),
        scratch_shapes=[pltpu.VMEM((2, EXPERT_TILE, D), F32), pltpu.VMEM((2, EXPERT_TILE, D), F32),
                        pltpu.SemaphoreType.DMA((2,)), pltpu.SemaphoreType.DMA(())],
    )
    return pl.pallas_call(
        functools.partial(_experts_kernel, n_chunks=_ff_chunks(d_ff)),
        grid_spec=grid_spec,
        out_shape=jax.ShapeDtypeStruct((n_rows, D), F32),
        compiler_params=_params(("arbitrary",)),
        name="experts",
    )(tile_expert, src_tok, dst_row, x, g, wg, wu, wd)


def _combine_kernel(x_ref, gate_ref, g_ref, y0_ref, y1_ref, o_ref, *, final_norm):
    gates = gate_ref[...]
    y = x_ref[...] + (y0_ref[...] * gates[:, 0:1] + y1_ref[...] * gates[:, 1:2])
    if final_norm:
        y = y * _rms_scale(y) * g_ref[...]
    o_ref[...] = y


def _combine(x, gates, ys, g_final, final_norm, first_block, n_blocks):
    T, D = x.shape
    rows = lambda i: (i + first_block, 0)
    return pl.pallas_call(
        functools.partial(_combine_kernel, final_norm=final_norm),
        grid=(n_blocks,),
        in_specs=[pl.BlockSpec((TM, D), rows), pl.BlockSpec((TM, LANES), rows), _const_spec((1, D)),
                  pl.BlockSpec((TM, D), rows),
                  pl.BlockSpec((TM, D), lambda i: (i + first_block + T // TM, 0))],
        out_specs=pl.BlockSpec((TM, D), lambda i: (i, 0)),
        out_shape=jax.ShapeDtypeStruct((n_blocks * TM, D), F32),
        compiler_params=_params(("parallel",)),
        name="combine",
    )(x, gates, g_final, ys, ys)


def _final_norm_kernel(x_ref, g_ref, o_ref):
    x = x_ref[...]
    o_ref[...] = x * _rms_scale(x) * g_ref[...]


def _final_norm(x, g):
    T, D = x.shape
    full = pl.BlockSpec((TM, D), lambda i: (i, 0))
    return pl.pallas_call(
        _final_norm_kernel,
        grid=(T // TM,),
        in_specs=[full, _const_spec((1, D))],
        out_specs=full,
        out_shape=jax.ShapeDtypeStruct((T, D), F32),
        compiler_params=_params(("parallel",)),
        name="final_norm",
    )(x, g)


def _route_operands(router_w):
    D = router_w.shape[0]
    rw_pad = jnp.zeros((D, LANES), F32).at[:, :N_EXPERTS].set(router_w)
    rw_hi = rw_pad.astype(BF16)
    rw2 = jnp.concatenate([rw_hi, (rw_pad - rw_hi.astype(F32)).astype(BF16)], axis=1)
    tri = jnp.asarray(np.tril(np.ones((TM, TM), np.float32), -1), BF16)
    return rw2, tri


def _moe(x, routing, g_ffn, wg, wu, wd, layer, g_final, final_norm, out_blocks):
    T, D = x.shape
    idx, gates, cnt = routing

    counts = cnt[0, :N_EXPERTS].astype(jnp.int32)
    padded = (counts + EXPERT_TILE - 1) // EXPERT_TILE * EXPERT_TILE
    pad_end = jnp.cumsum(padded)
    pad_start = pad_end - padded
    dest0 = pad_start[idx[:, 0]] + idx[:, 2]
    dest1 = pad_start[idx[:, 1]] + idx[:, 3]
    n_rows = 2 * T + N_EXPERTS * EXPERT_TILE
    n_tiles = n_rows // EXPERT_TILE
    tile_start = jnp.arange(n_tiles, dtype=jnp.int32) * EXPERT_TILE
    tile_seg = jnp.sum(pad_end[None, :] <= tile_start[:, None], axis=1).astype(jnp.int32)
    tile_expert = jnp.minimum(tile_seg, N_EXPERTS - 1)

    empty_lo = jnp.concatenate([pad_start + counts, pad_end[-1:]]).astype(jnp.int32)
    empty_hi = jnp.concatenate([pad_end, jnp.full((1,), n_rows)]).astype(jnp.int32)
    filled = _invert(dest0, dest1, empty_lo, empty_hi, n_rows)
    is_pad = filled == 0
    empty_len = empty_hi - empty_lo
    tile_rank0 = (jnp.cumsum(empty_len) - empty_len - empty_lo)[tile_seg]
    pad_rank = jnp.arange(n_rows, dtype=jnp.int32) + jnp.repeat(tile_rank0, EXPERT_TILE)
    assign = filled - 1
    src_tok = jnp.where(is_pad, 0, assign // 2)
    dst_row = jnp.where(is_pad, 2 * T + pad_rank, (assign % 2) * T + assign // 2)
    spare = n_rows + jnp.arange(EXPERT_TILE, dtype=jnp.int32)
    dst_row = jnp.concatenate([spare, dst_row])

    ys = _experts(x, g_ffn, src_tok, dst_row, tile_expert, wg, wu, wd, layer)
    if out_blocks is None:
        return ys, gates
    return [_combine(x, gates, ys, g_final, final_norm, b0, nb) for b0, nb in out_blocks]


def _image_blocks(image_rows):
    lo, hi, start = [], [], 0
    for rows in image_rows:
        n = rows // ROWS_PER_BLOCK
        lo += [start] * n
        hi += [start + n - 1] * n
        start += n
    return jnp.asarray(lo, jnp.int32), jnp.asarray(hi, jnp.int32)


def kernel(x_prompt, x_sample, norm_mix_g, w_in, na_rpb, sg_norm_g, sg_w, sg_b, out_norm_g, w_out,
           norm_ffn_g, dense_w_gate, dense_w_up, dense_w_down,
           router_w, moe_w_gate, moe_w_up, moe_w_down, final_norm_g):
    D = x_prompt.shape[-1]
    depth = w_in.shape[0]
    assert D == 2 * HALF
    image_rows = []
    for xs in (x_prompt, x_sample):
        assert xs.shape[1] % TM == 0 and xs.shape[1] >= WIN_BLOCKS * TM
        image_rows += [xs.shape[1] // GRID_W] * xs.shape[0]
    blk_lo, blk_hi = _image_blocks(image_rows)
    n_prompt = x_prompt.shape[0] * x_prompt.shape[1]

    xs = (x_prompt.reshape(-1, D), x_sample.reshape(-1, D))
    bd = jnp.asarray(np.kron(np.eye(N_HEADS, dtype=np.float32),
                             np.ones((HEAD_DIM, HEAD_DIM), np.float32)), BF16)
    g_final = final_norm_g.reshape(1, D)
    na_bias = _na_bias_tables(na_rpb)

    w_in, w_out, dense_w_gate, dense_w_up, dense_w_down, moe_w_gate, moe_w_up, moe_w_down = (
        w.astype(BF16) for w in (w_in, w_out, dense_w_gate, dense_w_up, dense_w_down,
                                 moe_w_gate, moe_w_up, moe_w_down))

    pending_moe = None
    for l in range(depth):
        q, k, v, u, svn, *x_new = _inproj(xs, norm_mix_g[l].reshape(1, D), w_in, l,
                                          sg_norm_g[l].reshape(1, HALF), bd, pending_moe)
        if pending_moe is not None:
            xs, pending_moe = (x_new[0],), None
        attn = _natten(q, k, v, na_bias[l], blk_lo, blk_hi)
        sgw2 = sg_w[l].astype(BF16).reshape(N_PAIRS, 2 * SG_CHUNK, SG_CHUNK)
        sgb_t = jnp.repeat(jnp.transpose(sg_b[l]), HEAD_DIM, axis=1)
        g_ffn = norm_ffn_g[l].reshape(1, D)
        i = l // 2
        last = l == depth - 1
        dense = l % 2 == 0
        x, *routing = _mixout(attn, u, svn, xs, sgw2, sgb_t, out_norm_g[l].reshape(1, D), w_out, l,
                              g_ffn,
                              ffn=(dense_w_gate, dense_w_up, dense_w_down, i) if dense else None,
                              route=None if dense else _route_operands(router_w[i]))
        if dense:
            if last:
                x = _final_norm(x, g_final)
                outs = [x[:n_prompt], x[n_prompt:]]
        elif last:
            nb, npb = x.shape[0] // TM, n_prompt // TM
            outs = _moe(x, routing, g_ffn, moe_w_gate, moe_w_up, moe_w_down, i, g_final,
                        final_norm=True, out_blocks=[(0, npb), (npb, nb - npb)])
        else:
            pending_moe = _moe(x, routing, g_ffn, moe_w_gate, moe_w_up, moe_w_down, i, g_final,
                               final_norm=False, out_blocks=None)
        xs = (x,)
    return (outs[0].reshape(x_prompt.shape), outs[1].reshape(x_sample.shape))
```

```python
import functools

import numpy as np
import jax
import jax.numpy as jnp
from jax import lax
from jax.experimental import pallas as pl
from jax.experimental.pallas import tpu as pltpu

F32 = jnp.float32
BF16 = jnp.bfloat16

GRID_W = 64
WIN_ROWS = 8
WIN_COLS = 16
N_HEADS = 8
HEAD_DIM = 64
HALF = N_HEADS * HEAD_DIM
N_PAIRS = N_HEADS // 2
LANES = 128
SG_CHUNK = 128
N_EXPERTS = 8
RMS_EPS = 1e-6
MASK_VALUE = -1e30

TM = 512
ROWS_PER_BLOCK = TM // GRID_W
WIN_BLOCKS = 3
LOG2_E = 1.4426950408889634
EXPERT_TILE = 512
INVERT_CHUNK = 2048
MXU_WIDTH = 256
MAX_FF_CHUNK = 3072
VMEM_LIMIT = 56 * 1024 * 1024


def _params(sem, vmem=VMEM_LIMIT):
    return pltpu.CompilerParams(dimension_semantics=sem, vmem_limit_bytes=vmem)


def _const_spec(shape):
    nd = len(shape)
    return pl.BlockSpec(shape, lambda *_: (0,) * nd, pipeline_mode=pl.Buffered(1))


def _layer_spec(w, layer):
    nd = w.ndim - 1
    return pl.BlockSpec((None,) + w.shape[1:], lambda *_: (layer,) + (0,) * nd,
                        pipeline_mode=pl.Buffered(1))


def _rms_scale(x):
    return lax.rsqrt(jnp.mean(x * x, axis=-1, keepdims=True) + RMS_EPS)


def _stream_specs(parts):
    D = parts[0].shape[1]
    if len(parts) == 1:
        return [pl.BlockSpec((TM, D), lambda i: (i, 0))]
    first = parts[0].shape[0] // TM
    return [pl.BlockSpec((TM, D), lambda i: (jnp.minimum(i, first - 1), 0)),
            pl.BlockSpec((TM, D), lambda i: (jnp.maximum(i - first, 0), 0))]


def _stream_tile(refs, first):
    if len(refs) == 1:
        return refs[0][...]
    return jnp.where(pl.program_id(0) < first, refs[0][...], refs[1][...])


def _gelu_tanh(x):
    return x * (0.5 * (1.0 + jnp.tanh(0.7978845608028654 * (x + 0.044715 * (x * x * x)))))


def _inproj_kernel(*refs, n_x, first, pending_moe):
    x = _stream_tile(refs[:n_x], first)
    if pending_moe:
        y0_ref, y1_ref, gate_ref = refs[n_x:n_x + 3]
        g_ref, w_ref, sgg_ref, bd_ref, q_ref, k_ref, v_ref, u_ref, sv_ref, x_out_ref = refs[n_x + 3:]
        gates = gate_ref[...]
        x = x + (y0_ref[...] * gates[:, 0:1] + y1_ref[...] * gates[:, 1:2])
        x_out_ref[...] = x
    else:
        g_ref, w_ref, sgg_ref, bd_ref, q_ref, k_ref, v_ref, u_ref, sv_ref = refs[n_x:]
    h = (x * _rms_scale(x) * g_ref[...]).astype(BF16)

    def proj(n):
        return jnp.dot(h, w_ref[:, n * HALF:(n + 1) * HALF], preferred_element_type=F32)

    u_ref[...] = _gelu_tanh(proj(3)).astype(BF16)
    sv = _gelu_tanh(proj(4))
    ssq = jnp.dot((sv * sv).astype(BF16), bd_ref[...], preferred_element_type=F32)
    sv_ref[...] = (sv * lax.rsqrt(ssq * (1.0 / HEAD_DIM) + RMS_EPS) * sgg_ref[...]).astype(BF16)

    q = proj(0) * (HEAD_DIM ** -0.5 * LOG2_E)
    is_even = (lax.broadcasted_iota(jnp.int32, q.shape, 1) % LANES) < HEAD_DIM
    qe = jnp.where(is_even, q, 0.0).astype(BF16)
    qo = jnp.where(is_even, 0.0, q).astype(BF16)
    for r in range(ROWS_PER_BLOCK):
        rows = slice(r * GRID_W, (r + 1) * GRID_W)
        q_ref[2 * r * GRID_W:(2 * r + 1) * GRID_W, :] = qe[rows]
        q_ref[(2 * r + 1) * GRID_W:(2 * r + 2) * GRID_W, :] = qo[rows]
    k_ref[...] = proj(1).astype(BF16)
    v_ref[...] = proj(2).astype(BF16)


def _inproj(xs, g, w_in, layer, sgg, bd, pending_moe=None):
    T = sum(x.shape[0] for x in xs)
    D = xs[0].shape[1]
    out = jax.ShapeDtypeStruct((T, HALF), BF16)
    tile = pl.BlockSpec((TM, HALF), lambda i: (i, 0))
    full = pl.BlockSpec((TM, D), lambda i: (i, 0))
    moe_args, moe_specs, moe_out_specs, moe_out_shape = [], [], [], []
    if pending_moe is not None:
        ys, gates = pending_moe
        moe_args = [ys, ys, gates]
        moe_specs = [full, pl.BlockSpec((TM, D), lambda i: (i + T // TM, 0)),
                     pl.BlockSpec((TM, LANES), lambda i: (i, 0))]
        moe_out_specs = [full]
        moe_out_shape = [jax.ShapeDtypeStruct((T, D), F32)]
    return pl.pallas_call(
        functools.partial(_inproj_kernel, n_x=len(xs), first=xs[0].shape[0] // TM,
                          pending_moe=pending_moe is not None),
        grid=(T // TM,),
        in_specs=_stream_specs(xs) + moe_specs + [
            _const_spec((1, D)), _layer_spec(w_in, layer),
            _const_spec((1, HALF)), _const_spec((HALF, HALF))],
        out_specs=[pl.BlockSpec((2 * TM, HALF), lambda i: (i, 0))] + [tile] * 4 + moe_out_specs,
        out_shape=[jax.ShapeDtypeStruct((2 * T, HALF), BF16)] + [out] * 4 + moe_out_shape,
        compiler_params=_params(("parallel",)),
        name="inproj",
    )(*xs, *moe_args, g, w_in, sgg, bd)


def _row_plan(blk, n_blk):
    win_row0 = min(max(blk - 1, 0), n_blk - WIN_BLOCKS) * ROWS_PER_BLOCK
    plan = []
    for i in range(ROWS_PER_BLOCK):
        r = blk * ROWS_PER_BLOCK + i
        rs = min(max(r - WIN_ROWS // 2, 0), n_blk * ROWS_PER_BLOCK - WIN_ROWS)
        plan.append((rs - win_row0, rs - r + WIN_ROWS - 1))
    return plan


def _natten_kernel(lo_ref, hi_ref, q_ref, kwin, vwin, bias_ref, o_ref):
    b = pl.program_id(0)
    blk = b - lo_ref[b]
    n_blk = hi_ref[b] - lo_ref[b] + 1
    out_even = lax.broadcasted_iota(jnp.int32, (GRID_W, LANES), 1) < HEAD_DIM
    n_keys = WIN_ROWS * GRID_W

    def rows(plan):
        for i, (ws, variant) in enumerate(plan):
            scores = []
            for p in range(N_PAIRS):
                cols = slice(p * LANES, (p + 1) * LANES)
                s = lax.dot_general(q_ref[2 * i * GRID_W:2 * (i + 1) * GRID_W, cols],
                                    kwin[ws * GRID_W:ws * GRID_W + n_keys, cols],
                                    (((1,), (1,)), ((), ())), preferred_element_type=F32)
                scores.append(s + bias_ref[variant, p])
            for p in range(N_PAIRS):
                cols = slice(p * LANES, (p + 1) * LANES)
                s = scores[p]
                e = jnp.exp2(s - jnp.max(s, axis=-1, keepdims=True))
                denom = jnp.sum(e, axis=-1, keepdims=True)
                o2 = jnp.dot(e.astype(BF16), vwin[ws * GRID_W:ws * GRID_W + n_keys, cols],
                             preferred_element_type=F32) * (1.0 / denom)
                out = jnp.where(out_even, o2[:GRID_W], o2[GRID_W:])
                o_ref[i * GRID_W:(i + 1) * GRID_W, cols] = out.astype(BF16)

    model_blocks = WIN_BLOCKS + 2
    inner = _row_plan(1, model_blocks)
    assert all(_row_plan(k, model_blocks) == inner for k in range(1, model_blocks - 1))
    pl.when(blk == 0)(lambda: rows(_row_plan(0, model_blocks)))
    pl.when(blk == n_blk - 1)(lambda: rows(_row_plan(model_blocks - 1, model_blocks)))
    pl.when((blk > 0) & (blk < n_blk - 1))(lambda: rows(inner))


def _natten(q2, k, v, bias, blk_lo, blk_hi):
    T = v.shape[0]

    def win_map(b, lo, hi):
        return (jnp.clip(b - 1, lo[b], hi[b] - (WIN_BLOCKS - 1)) * TM, 0)

    window = pl.BlockSpec((pl.Element(WIN_BLOCKS * TM), pl.Element(HALF)), win_map)
    grid_spec = pltpu.PrefetchScalarGridSpec(
        num_scalar_prefetch=2,
        grid=(T // TM,),
        in_specs=[pl.BlockSpec((2 * TM, HALF), lambda b, lo, hi: (b, 0)), window, window,
                  pl.BlockSpec(bias.shape, lambda b, lo, hi: (0, 0, 0, 0),
                               pipeline_mode=pl.Buffered(1))],
        out_specs=pl.BlockSpec((TM, HALF), lambda b, lo, hi: (b, 0)),
    )
    return pl.pallas_call(
        _natten_kernel,
        grid_spec=grid_spec,
        out_shape=jax.ShapeDtypeStruct((T, HALF), BF16),
        compiler_params=_params(("parallel",)),
        name="natten",
    )(blk_lo, blk_hi, q2, k, v, bias)


def _na_bias_tables(rpb):
    n_layers = rpb.shape[0]
    qc = np.arange(GRID_W)
    kc = np.arange(GRID_W)
    cs = np.clip(qc - WIN_COLS // 2, 0, GRID_W - WIN_COLS)
    col_in = (kc[None, :] >= cs[:, None]) & (kc[None, :] < cs[:, None] + WIN_COLS)
    dc = np.clip(kc[None, :] - qc[:, None], -(WIN_COLS - 1), WIN_COLS - 1) + (WIN_COLS - 1)
    onehot = (dc.reshape(-1)[None, :] == np.arange(2 * WIN_COLS - 1)[:, None]).astype(np.float32)
    band = jnp.dot(rpb.astype(F32).reshape(-1, 2 * WIN_COLS - 1), onehot,
                   precision=lax.Precision.HIGHEST)
    band = band.reshape(n_layers, N_HEADS, 2 * WIN_ROWS - 1, GRID_W, GRID_W)
    band = jnp.where(col_in[None, None, None], band * LOG2_E, MASK_VALUE)
    band = jnp.transpose(band, (0, 1, 3, 2, 4))
    t = jnp.stack([band[:, :, :, o:o + WIN_ROWS, :] for o in range(WIN_ROWS)], axis=1)
    return t.reshape(n_layers, WIN_ROWS, N_PAIRS, 2 * GRID_W, WIN_ROWS * GRID_W)


def _mixout_kernel(*refs, n_x, first, tail, n_chunks):
    a_ref, u_ref, sv_ref, sgw_ref, sgb_ref, g_ref, w_ref, gffn_ref = refs[n_x:n_x + 8]
    if tail == "ffn":
        wg_ref, wu_ref, wd_ref, o_ref, sg_s = refs[n_x + 8:]
    else:
        rw_ref, tri_ref, o_ref, idx_ref, gate_ref, cnt_ref, sg_s, carry = refs[n_x + 8:]
    even = lax.broadcasted_iota(jnp.int32, (SG_CHUNK, LANES), 1) < HEAD_DIM
    for c in range(TM // SG_CHUNK):
        rows = slice(c * SG_CHUNK, (c + 1) * SG_CHUNK)
        for p in range(N_PAIRS):
            cols = slice(p * LANES, (p + 1) * LANES)
            m2 = jnp.dot(sgw_ref[p], sv_ref[rows, cols], preferred_element_type=F32)
            mixed = jnp.where(even, m2[:SG_CHUNK], m2[SG_CHUNK:])
            sg_s[rows, cols] = u_ref[rows, cols].astype(F32) * (mixed + sgb_ref[:, cols])
    sg = sg_s[...]
    sgn = (sg * _rms_scale(sg) * g_ref[:, HALF:]).astype(BF16)
    a = a_ref[...].astype(F32)
    an = (a * _rms_scale(a) * g_ref[:, :HALF]).astype(BF16)
    y = jnp.dot(an, w_ref[:HALF, :], preferred_element_type=F32)
    y = y + jnp.dot(sgn, w_ref[HALF:, :], preferred_element_type=F32)
    x = _stream_tile(refs[:n_x], first) + y
    if tail == "ffn":
        h = (x * _rms_scale(x) * gffn_ref[...]).astype(BF16)
        o_ref[...] = x + _swiglu(h, wg_ref, wu_ref, wd_ref, n_chunks)
    else:
        o_ref[...] = x

        @pl.when(pl.program_id(0) == 0)
        def _():
            carry[...] = jnp.zeros_like(carry)

        _route_tile(x, gffn_ref, rw_ref, tri_ref, carry, idx_ref, gate_ref, cnt_ref)


def _route_tile(x, g_ref, rw_ref, tri_ref, carry, idx_ref, gate_ref, cnt_ref):
    h = x * _rms_scale(x) * g_ref[...]
    h_hi = h.astype(BF16)
    h_lo = (h - h_hi.astype(F32)).astype(BF16)
    prod = jnp.dot(jnp.concatenate([h_hi, h_lo], axis=0), rw_ref[...], preferred_element_type=F32)
    logits = (prod[:TM, :LANES] + prod[:TM, LANES:]) + (prod[TM:, :LANES] + prod[TM:, LANES:])
    col = lax.broadcasted_iota(jnp.int32, logits.shape, 1)
    colf = col.astype(F32)
    neg = -jnp.inf
    lg = jnp.where(col < N_EXPERTS, logits, neg)
    m1 = jnp.max(lg, axis=-1, keepdims=True)
    i1 = jnp.min(jnp.where(lg == m1, colf, float(LANES)), axis=-1, keepdims=True)
    sel1 = colf == i1
    lg2 = jnp.where(sel1, neg, lg)
    m2 = jnp.max(lg2, axis=-1, keepdims=True)
    i2 = jnp.min(jnp.where(lg2 == m2, colf, float(LANES)), axis=-1, keepdims=True)
    sel2 = colf == i2
    e2 = jnp.exp(m2 - m1)
    g1 = 1.0 / (1.0 + e2)
    g2 = e2 / (1.0 + e2)

    cnt = jnp.where(sel1 | sel2, 1.0, 0.0)
    before = jnp.dot(tri_ref[...], cnt.astype(BF16), preferred_element_type=F32) + carry[...]
    r1 = jnp.sum(jnp.where(sel1, before, 0.0), axis=-1, keepdims=True)
    r2 = jnp.sum(jnp.where(sel2, before, 0.0), axis=-1, keepdims=True)
    total = carry[...] + jnp.sum(cnt, axis=0, keepdims=True)
    carry[...] = total
    cnt_ref[...] = jnp.broadcast_to(total, cnt_ref.shape)

    meta = jnp.where(col == 0, i1, jnp.where(col == 1, i2, jnp.where(col == 2, r1, r2)))
    idx_ref[...] = meta.astype(jnp.int32)
    gate_ref[...] = jnp.where(col == 0, g1, g2)


def _mixout(attn, u, svn, xs, sgw2, sgb_t, g_out, w_out, layer, g_ffn, ffn=None, route=None):
    T = sum(x.shape[0] for x in xs)
    D = xs[0].shape[1]
    half = pl.BlockSpec((TM, HALF), lambda i: (i, 0))
    full = pl.BlockSpec((TM, D), lambda i: (i, 0))
    meta = pl.BlockSpec((TM, LANES), lambda i: (i, 0))
    if ffn is not None:
        *tail_args, ffn_layer = ffn
        tail_specs = [_layer_spec(w, ffn_layer) for w in tail_args]
        n_chunks = _ff_chunks(tail_args[0].shape[-1])
    else:
        tail_args = list(route)
        tail_specs = [_const_spec(a.shape) for a in tail_args]
        n_chunks = None
    return pl.pallas_call(
        functools.partial(_mixout_kernel, n_x=len(xs), first=xs[0].shape[0] // TM,
                          tail="ffn" if ffn is not None else "route", n_chunks=n_chunks),
        grid=(T // TM,),
        in_specs=_stream_specs(xs) + [
            half, half, half, _const_spec(sgw2.shape), _const_spec(sgb_t.shape),
            _const_spec((1, 2 * HALF)), _layer_spec(w_out, layer), _const_spec((1, D))
        ] + tail_specs,
        out_specs=[full] + ([meta, meta, pl.BlockSpec((8, LANES), lambda i: (0, 0))] if route else []),
        out_shape=[jax.ShapeDtypeStruct((T, D), F32)] + (
            [jax.ShapeDtypeStruct((T, LANES), jnp.int32), jax.ShapeDtypeStruct((T, LANES), F32),
             jax.ShapeDtypeStruct((8, LANES), F32)] if route else []),
        scratch_shapes=[pltpu.VMEM((TM, HALF), F32)] + (
            [pltpu.VMEM((1, LANES), F32)] if route else []),
        compiler_params=_params(("arbitrary",) if route else ("parallel",)),
        name="mixout",
    )(*xs, attn, u, svn, sgw2, sgb_t, g_out, w_out, g_ffn, *tail_args)


def _ff_chunks(d_ff):
    for width in (MXU_WIDTH, LANES):
        for n in range(1, d_ff // width + 1):
            if d_ff % n == 0 and d_ff // n <= MAX_FF_CHUNK and (d_ff // n) % width == 0:
                return n
    raise ValueError(f"d_ff={d_ff} is not a multiple of {LANES}")


def _swiglu(h, wg_ref, wu_ref, wd_ref, n_chunks):
    d_ff = wd_ref.shape[-2]
    fc = d_ff // n_chunks
    y = None
    for c in range(n_chunks):
        cols = slice(c * fc, (c + 1) * fc)
        gate = jnp.dot(h, wg_ref[:, cols], preferred_element_type=F32)
        up = jnp.dot(h, wu_ref[:, cols], preferred_element_type=F32)
        act = (gate / (1.0 + jnp.exp(-gate)) * up).astype(BF16)
        part = jnp.dot(act, wd_ref[cols, :], preferred_element_type=F32)
        y = part if y is None else y + part
    return y


def _invert_kernel(lo_ref, hi_ref, dest_ref, out_ref):
    i = pl.program_id(0)

    @pl.when(i == 0)
    def _():
        def clear(j, carry):
            out_ref[j] = 0
            return carry

        for r in range(N_EXPERTS + 1):
            lax.fori_loop(lo_ref[r], hi_ref[r], clear, 0)

    first = i * INVERT_CHUNK + 1

    def place(a, carry):
        out_ref[dest_ref[a]] = first + a
        return carry

    lax.fori_loop(0, INVERT_CHUNK, place, 0, unroll=8)


def _invert(dest, empty_lo, empty_hi, n_rows):
    n = dest.shape[0]
    assert n % INVERT_CHUNK == 0
    grid_spec = pltpu.PrefetchScalarGridSpec(
        num_scalar_prefetch=2,
        grid=(n // INVERT_CHUNK,),
        in_specs=[pl.BlockSpec((INVERT_CHUNK,), lambda i, lo, hi: (i,), memory_space=pltpu.SMEM)],
        out_specs=pl.BlockSpec((n_rows,), lambda i, lo, hi: (0,), memory_space=pltpu.SMEM),
    )
    return pl.pallas_call(
        _invert_kernel,
        grid_spec=grid_spec,
        out_shape=jax.ShapeDtypeStruct((n_rows,), jnp.int32),
        compiler_params=_params(("arbitrary",)),
        name="invert",
    )(empty_lo, empty_hi, dest)


def _experts_kernel(te_ref, src_ref, dst_ref, x_hbm, g_ref, wg_ref, wu_ref, wd_ref, ys_hbm,
                    xbuf, obuf, gsem, ssem, *, n_chunks):
    del te_ref
    i = pl.program_id(0)
    slot = i % 2

    def gather_wait(s):
        pltpu.make_async_copy(x_hbm.at[pl.ds(0, EXPERT_TILE)], xbuf.at[s], gsem.at[s]).wait()

    @pl.when(i == 0)
    def _():
        xbuf[...] = jnp.zeros_like(xbuf)
        obuf[...] = jnp.zeros_like(obuf)

    @pl.when(i >= 1)
    def _():
        gather_wait(1 - slot)

    x = xbuf[1 - slot]
    h = (x * _rms_scale(x) * g_ref[...]).astype(BF16)
    for t in range(EXPERT_TILE):
        pltpu.make_async_copy(x_hbm.at[pl.ds(src_ref[t], 1)], xbuf.at[slot, pl.ds(t, 1)],
                              gsem.at[slot]).start()
        pltpu.make_async_copy(obuf.at[slot, pl.ds(t, 1)], ys_hbm.at[pl.ds(dst_ref[t], 1)],
                              ssem).start()
    y = _swiglu(h, wg_ref.at[0], wu_ref.at[0], wd_ref.at[0], n_chunks)
    pltpu.make_async_copy(obuf.at[slot], ys_hbm.at[pl.ds(0, EXPERT_TILE)], ssem).wait()
    obuf[1 - slot] = y

    @pl.when(i == pl.num_programs(0) - 1)
    def _():
        gather_wait(slot)


def _experts(x, g, src_tok, dst_row, tile_expert, wg, wu, wd, layer):
    T, D = x.shape
    d_ff = wg.shape[-1]
    n_tiles = src_tok.shape[0] // EXPERT_TILE
    n_rows = dst_row.shape[0]

    def w_spec(w):
        return pl.BlockSpec((None, 1) + w.shape[2:],
                            lambda i, te: (layer, te[jnp.clip(i - 1, 0, n_tiles - 1)], 0, 0),
                            pipeline_mode=pl.Buffered(1))

    grid_spec = pltpu.PrefetchScalarGridSpec(
        num_scalar_prefetch=1,
        grid=(n_tiles + 2,),
        in_specs=[pl.BlockSpec((EXPERT_TILE,), lambda i, te: (jnp.minimum(i, n_tiles - 1),),
                               memory_space=pltpu.SMEM),
                  pl.BlockSpec((EXPERT_TILE,), lambda i, te: (jnp.maximum(i - 1, 0),),
                               memory_space=pltpu.SMEM),
                  pl.BlockSpec(memory_space=pl.ANY),
                  pl.BlockSpec((1, D), lambda i, te: (0, 0), pipeline_mode=pl.Buffered(1)),
                  w_spec(wg), w_spec(wu), w_spec(wd)],
        out_specs=pl.BlockSpec(memory_space=pl.ANY),
        scratch_shapes=[pltpu.VMEM((2, EXPERT_TILE, D), F32), pltpu.VMEM((2, EXPERT_TILE, D), F32),
                        pltpu.SemaphoreType.DMA((2,)), pltpu.SemaphoreType.DMA(())],
    )
    return pl.pallas_call(
        functools.partial(_experts_kernel, n_chunks=_ff_chunks(d_ff)),
        grid_spec=grid_spec,
        out_shape=jax.ShapeDtypeStruct((n_rows, D), F32),
        compiler_params=_params(("arbitrary",)),
        name="experts",
    )(tile_expert, src_tok, dst_row, x, g, wg, wu, wd)


def _combine_kernel(x_ref, gate_ref, g_ref, y0_ref, y1_ref, o_ref, *, final_norm):
    gates = gate_ref[...]
    y = x_ref[...] + (y0_ref[...] * gates[:, 0:1] + y1_ref[...] * gates[:, 1:2])
    if final_norm:
        y = y * _rms_scale(y) * g_ref[...]
    o_ref[...] = y


def _combine(x, gates, ys, g_final, final_norm, first_block, n_blocks):
    T, D = x.shape
    rows = lambda i: (i + first_block, 0)
    return pl.pallas_call(
        functools.partial(_combine_kernel, final_norm=final_norm),
        grid=(n_blocks,),
        in_specs=[pl.BlockSpec((TM, D), rows), pl.BlockSpec((TM, LANES), rows), _const_spec((1, D)),
                  pl.BlockSpec((TM, D), rows),
                  pl.BlockSpec((TM, D), lambda i: (i + first_block + T // TM, 0))],
        out_specs=pl.BlockSpec((TM, D), lambda i: (i, 0)),
        out_shape=jax.ShapeDtypeStruct((n_blocks * TM, D), F32),
        compiler_params=_params(("parallel",)),
        name="combine",
    )(x, gates, g_final, ys, ys)


def _final_norm_kernel(x_ref, g_ref, o_ref):
    x = x_ref[...]
    o_ref[...] = x * _rms_scale(x) * g_ref[...]


def _final_norm(x, g):
    T, D = x.shape
    full = pl.BlockSpec((TM, D), lambda i: (i, 0))
    return pl.pallas_call(
        _final_norm_kernel,
        grid=(T // TM,),
        in_specs=[full, _const_spec((1, D))],
        out_specs=full,
        out_shape=jax.ShapeDtypeStruct((T, D), F32),
        compiler_params=_params(("parallel",)),
        name="final_norm",
    )(x, g)


def _route_operands(router_w):
    D = router_w.shape[0]
    rw_pad = jnp.zeros((D, LANES), F32).at[:, :N_EXPERTS].set(router_w)
    rw_hi = rw_pad.astype(BF16)
    rw2 = jnp.concatenate([rw_hi, (rw_pad - rw_hi.astype(F32)).astype(BF16)], axis=1)
    tri = jnp.asarray(np.tril(np.ones((TM, TM), np.float32), -1), BF16)
    return rw2, tri


def _moe(x, routing, g_ffn, wg, wu, wd, layer, g_final, final_norm, out_blocks):
    T, D = x.shape
    idx, gates, cnt = routing

    counts = cnt[0, :N_EXPERTS].astype(jnp.int32)
    padded = (counts + EXPERT_TILE - 1) // EXPERT_TILE * EXPERT_TILE
    pad_end = jnp.cumsum(padded)
    pad_start = pad_end - padded
    dest = (pad_start[idx[:, 0:2]] + idx[:, 2:4]).reshape(2 * T)
    n_rows = 2 * T + N_EXPERTS * EXPERT_TILE
    n_tiles = n_rows // EXPERT_TILE
    tile_start = jnp.arange(n_tiles, dtype=jnp.int32) * EXPERT_TILE
    tile_expert = jnp.minimum(
        jnp.sum(pad_end[None, :] <= tile_start[:, None], axis=1), N_EXPERTS - 1).astype(jnp.int32)

    empty_lo = jnp.concatenate([pad_start + counts, pad_end[-1:]]).astype(jnp.int32)
    empty_hi = jnp.concatenate([pad_end, jnp.full((1,), n_rows)]).astype(jnp.int32)
    filled = _invert(dest, empty_lo, empty_hi, n_rows)
    is_pad = filled == 0
    pad_rank = jnp.cumsum(is_pad.astype(jnp.int32)) - 1
    assign = filled - 1
    src_tok = jnp.where(is_pad, 0, assign // 2)
    dst_row = jnp.where(is_pad, 2 * T + pad_rank, (assign % 2) * T + assign // 2)
    spare = n_rows + jnp.arange(EXPERT_TILE, dtype=jnp.int32)
    dst_row = jnp.concatenate([spare, dst_row])

    ys = _experts(x, g_ffn, src_tok, dst_row, tile_expert, wg, wu, wd, layer)
    if out_blocks is None:
        return ys, gates
    return [_combine(x, gates, ys, g_final, final_norm, b0, nb) for b0, nb in out_blocks]


def _image_blocks(image_rows):
    lo, hi, start = [], [], 0
    for rows in image_rows:
        n = rows // ROWS_PER_BLOCK
        lo += [start] * n
        hi += [start + n - 1] * n
        start += n
    return jnp.asarray(lo, jnp.int32), jnp.asarray(hi, jnp.int32)


def kernel(x_prompt, x_sample, norm_mix_g, w_in, na_rpb, sg_norm_g, sg_w, sg_b, out_norm_g, w_out,
           norm_ffn_g, dense_w_gate, dense_w_up, dense_w_down,
           router_w, moe_w_gate, moe_w_up, moe_w_down, final_norm_g):
    D = x_prompt.shape[-1]
    depth = w_in.shape[0]
    assert D == 2 * HALF
    image_rows = []
    for xs in (x_prompt, x_sample):
        assert xs.shape[1] % TM == 0 and xs.shape[1] >= WIN_BLOCKS * TM
        image_rows += [xs.shape[1] // GRID_W] * xs.shape[0]
    blk_lo, blk_hi = _image_blocks(image_rows)
    n_prompt = x_prompt.shape[0] * x_prompt.shape[1]

    xs = (x_prompt.reshape(-1, D), x_sample.reshape(-1, D))
    bd = jnp.asarray(np.kron(np.eye(N_HEADS, dtype=np.float32),
                             np.ones((HEAD_DIM, HEAD_DIM), np.float32)), BF16)
    g_final = final_norm_g.reshape(1, D)
    na_bias = _na_bias_tables(na_rpb)

    w_in, w_out, dense_w_gate, dense_w_up, dense_w_down, moe_w_gate, moe_w_up, moe_w_down = (
        w.astype(BF16) for w in (w_in, w_out, dense_w_gate, dense_w_up, dense_w_down,
                                 moe_w_gate, moe_w_up, moe_w_down))

    pending_moe = None
    for l in range(depth):
        q, k, v, u, svn, *x_new = _inproj(xs, norm_mix_g[l].reshape(1, D), w_in, l,
                                          sg_norm_g[l].reshape(1, HALF), bd, pending_moe)
        if pending_moe is not None:
            xs, pending_moe = (x_new[0],), None
        attn = _natten(q, k, v, na_bias[l], blk_lo, blk_hi)
        sgw2 = sg_w[l].astype(BF16).reshape(N_PAIRS, 2 * SG_CHUNK, SG_CHUNK)
        sgb_t = jnp.repeat(jnp.transpose(sg_b[l]), HEAD_DIM, axis=1)
        g_ffn = norm_ffn_g[l].reshape(1, D)
        i = l // 2
        last = l == depth - 1
        dense = l % 2 == 0
        x, *routing = _mixout(attn, u, svn, xs, sgw2, sgb_t, out_norm_g[l].reshape(1, D), w_out, l,
                              g_ffn,
                              ffn=(dense_w_gate, dense_w_up, dense_w_down, i) if dense else None,
                              route=None if dense else _route_operands(router_w[i]))
        if dense:
            if last:
                x = _final_norm(x, g_final)
                outs = [x[:n_prompt], x[n_prompt:]]
        elif last:
            nb, npb = x.shape[0] // TM, n_prompt // TM
            outs = _moe(x, routing, g_ffn, moe_w_gate, moe_w_up, moe_w_down, i, g_final,
                        final_norm=True, out_blocks=[(0, npb), (npb, nb - npb)])
        else:
            pending_moe = _moe(x, routing, g_ffn, moe_w_gate, moe_w_up, moe_w_down, i, g_final,
                               final_norm=False, out_blocks=None)
        xs = (x,)
    return (outs[0].reshape(x_prompt.shape), outs[1].reshape(x_sample.shape))
```

```python
import functools

import numpy as np
import jax
import jax.numpy as jnp
from jax import lax
from jax.experimental import pallas as pl
from jax.experimental.pallas import tpu as pltpu

F32 = jnp.float32
BF16 = jnp.bfloat16

GRID_W = 64
WIN_ROWS = 8
WIN_COLS = 16
N_HEADS = 8
HEAD_DIM = 64
HALF = N_HEADS * HEAD_DIM
N_PAIRS = N_HEADS // 2
LANES = 128
SG_CHUNK = 128
N_EXPERTS = 8
RMS_EPS = 1e-6
MASK_VALUE = -1e30

TM = 512
ROWS_PER_BLOCK = TM // GRID_W
WIN_BLOCKS = 3
LOG2_E = 1.4426950408889634
EXPERT_TILE = 512
INVERT_CHUNK = 2048
MXU_WIDTH = 256
MAX_FF_CHUNK = 3072
VMEM_LIMIT = 56 * 1024 * 1024


def _params(sem, vmem=VMEM_LIMIT):
    return pltpu.CompilerParams(dimension_semantics=sem, vmem_limit_bytes=vmem)


def _const_spec(shape):
    nd = len(shape)
    return pl.BlockSpec(shape, lambda *_: (0,) * nd, pipeline_mode=pl.Buffered(1))


def _layer_spec(w, layer):
    nd = w.ndim - 1
    return pl.BlockSpec((None,) + w.shape[1:], lambda *_: (layer,) + (0,) * nd,
                        pipeline_mode=pl.Buffered(1))


def _rms_scale(x):
    return lax.rsqrt(jnp.mean(x * x, axis=-1, keepdims=True) + RMS_EPS)


def _stream_specs(parts):
    D = parts[0].shape[1]
    if len(parts) == 1:
        return [pl.BlockSpec((TM, D), lambda i: (i, 0))]
    first = parts[0].shape[0] // TM
    return [pl.BlockSpec((TM, D), lambda i: (jnp.minimum(i, first - 1), 0)),
            pl.BlockSpec((TM, D), lambda i: (jnp.maximum(i - first, 0), 0))]


def _stream_tile(refs, first):
    if len(refs) == 1:
        return refs[0][...]
    return jnp.where(pl.program_id(0) < first, refs[0][...], refs[1][...])


def _pending_operands(pending_moe, T, D):
    if pending_moe is None:
        return [], []
    ys, gates = pending_moe
    return [ys, ys, gates], [pl.BlockSpec((TM, D), lambda i: (i, 0)),
                             pl.BlockSpec((TM, D), lambda i: (i + T // TM, 0)),
                             pl.BlockSpec((TM, LANES), lambda i: (i, 0))]


def _add_pending(x, y0_ref, y1_ref, gate_ref):
    gates = gate_ref[...]
    return x + (y0_ref[...] * gates[:, 0:1] + y1_ref[...] * gates[:, 1:2])


def _gelu_tanh(x):
    return x * (0.5 * (1.0 + jnp.tanh(0.7978845608028654 * (x + 0.044715 * (x * x * x)))))


def _inproj_kernel(*refs, n_x, first, pending_moe):
    x = _stream_tile(refs[:n_x], first)
    n_in = n_x + (3 if pending_moe else 0)
    if pending_moe:
        x = _add_pending(x, *refs[n_x:n_in])
    g_ref, w_ref, sgg_ref, bd_ref, q_ref, k_ref, v_ref, u_ref, sv_ref = refs[n_in:]
    h = (x * _rms_scale(x) * g_ref[...]).astype(BF16)

    def proj(n):
        return jnp.dot(h, w_ref[:, n * HALF:(n + 1) * HALF], preferred_element_type=F32)

    u_ref[...] = _gelu_tanh(proj(3)).astype(BF16)
    sv = _gelu_tanh(proj(4))
    ssq = jnp.dot((sv * sv).astype(BF16), bd_ref[...], preferred_element_type=F32)
    sv_ref[...] = (sv * lax.rsqrt(ssq * (1.0 / HEAD_DIM) + RMS_EPS) * sgg_ref[...]).astype(BF16)

    q = proj(0) * (HEAD_DIM ** -0.5 * LOG2_E)
    is_even = (lax.broadcasted_iota(jnp.int32, q.shape, 1) % LANES) < HEAD_DIM
    qe = jnp.where(is_even, q, 0.0).astype(BF16)
    qo = jnp.where(is_even, 0.0, q).astype(BF16)
    for r in range(ROWS_PER_BLOCK):
        rows = slice(r * GRID_W, (r + 1) * GRID_W)
        q_ref[2 * r * GRID_W:(2 * r + 1) * GRID_W, :] = qe[rows]
        q_ref[(2 * r + 1) * GRID_W:(2 * r + 2) * GRID_W, :] = qo[rows]
    k_ref[...] = proj(1).astype(BF16)
    v_ref[...] = proj(2).astype(BF16)


def _inproj(xs, g, w_in, layer, sgg, bd, pending_moe=None):
    T = sum(x.shape[0] for x in xs)
    D = xs[0].shape[1]
    out = jax.ShapeDtypeStruct((T, HALF), BF16)
    tile = pl.BlockSpec((TM, HALF), lambda i: (i, 0))
    moe_args, moe_specs = _pending_operands(pending_moe, T, D)
    return pl.pallas_call(
        functools.partial(_inproj_kernel, n_x=len(xs), first=xs[0].shape[0] // TM,
                          pending_moe=pending_moe is not None),
        grid=(T // TM,),
        in_specs=_stream_specs(xs) + moe_specs + [
            _const_spec((1, D)), _layer_spec(w_in, layer),
            _const_spec((1, HALF)), _const_spec((HALF, HALF))],
        out_specs=[pl.BlockSpec((2 * TM, HALF), lambda i: (i, 0))] + [tile] * 4,
        out_shape=[jax.ShapeDtypeStruct((2 * T, HALF), BF16)] + [out] * 4,
        compiler_params=_params(("parallel",)),
        name="inproj",
    )(*xs, *moe_args, g, w_in, sgg, bd)


def _row_plan(blk, n_blk):
    win_row0 = min(max(blk - 1, 0), n_blk - WIN_BLOCKS) * ROWS_PER_BLOCK
    plan = []
    for i in range(ROWS_PER_BLOCK):
        r = blk * ROWS_PER_BLOCK + i
        rs = min(max(r - WIN_ROWS // 2, 0), n_blk * ROWS_PER_BLOCK - WIN_ROWS)
        plan.append((rs - win_row0, rs - r + WIN_ROWS - 1))
    return plan


def _natten_kernel(lo_ref, hi_ref, q_ref, kwin, vwin, bias_ref, o_ref):
    b = pl.program_id(0)
    blk = b - lo_ref[b]
    n_blk = hi_ref[b] - lo_ref[b] + 1
    out_even = lax.broadcasted_iota(jnp.int32, (GRID_W, LANES), 1) < HEAD_DIM
    n_keys = WIN_ROWS * GRID_W

    def rows(plan):
        for i, (ws, variant) in enumerate(plan):
            scores = []
            for p in range(N_PAIRS):
                cols = slice(p * LANES, (p + 1) * LANES)
                s = lax.dot_general(q_ref[2 * i * GRID_W:2 * (i + 1) * GRID_W, cols],
                                    kwin[ws * GRID_W:ws * GRID_W + n_keys, cols],
                                    (((1,), (1,)), ((), ())), preferred_element_type=F32)
                scores.append(s + bias_ref[variant, p])
            for p in range(N_PAIRS):
                cols = slice(p * LANES, (p + 1) * LANES)
                s = scores[p]
                e = jnp.exp2(s - jnp.max(s, axis=-1, keepdims=True))
                denom = jnp.sum(e, axis=-1, keepdims=True)
                o2 = jnp.dot(e.astype(BF16), vwin[ws * GRID_W:ws * GRID_W + n_keys, cols],
                             preferred_element_type=F32) * (1.0 / denom)
                out = jnp.where(out_even, o2[:GRID_W], o2[GRID_W:])
                o_ref[i * GRID_W:(i + 1) * GRID_W, cols] = out.astype(BF16)

    model_blocks = WIN_BLOCKS + 2
    inner = _row_plan(1, model_blocks)
    assert all(_row_plan(k, model_blocks) == inner for k in range(1, model_blocks - 1))
    pl.when(blk == 0)(lambda: rows(_row_plan(0, model_blocks)))
    pl.when(blk == n_blk - 1)(lambda: rows(_row_plan(model_blocks - 1, model_blocks)))
    pl.when((blk > 0) & (blk < n_blk - 1))(lambda: rows(inner))


def _natten(q2, k, v, bias, blk_lo, blk_hi):
    T = v.shape[0]

    def win_map(b, lo, hi):
        return (jnp.clip(b - 1, lo[b], hi[b] - (WIN_BLOCKS - 1)) * TM, 0)

    window = pl.BlockSpec((pl.Element(WIN_BLOCKS * TM), pl.Element(HALF)), win_map)
    grid_spec = pltpu.PrefetchScalarGridSpec(
        num_scalar_prefetch=2,
        grid=(T // TM,),
        in_specs=[pl.BlockSpec((2 * TM, HALF), lambda b, lo, hi: (b, 0)), window, window,
                  pl.BlockSpec(bias.shape, lambda b, lo, hi: (0, 0, 0, 0),
                               pipeline_mode=pl.Buffered(1))],
        out_specs=pl.BlockSpec((TM, HALF), lambda b, lo, hi: (b, 0)),
    )
    return pl.pallas_call(
        _natten_kernel,
        grid_spec=grid_spec,
        out_shape=jax.ShapeDtypeStruct((T, HALF), BF16),
        compiler_params=_params(("parallel",)),
        name="natten",
    )(blk_lo, blk_hi, q2, k, v, bias)


def _na_bias_tables(rpb):
    n_layers = rpb.shape[0]
    qc = np.arange(GRID_W)
    kc = np.arange(GRID_W)
    cs = np.clip(qc - WIN_COLS // 2, 0, GRID_W - WIN_COLS)
    col_in = (kc[None, :] >= cs[:, None]) & (kc[None, :] < cs[:, None] + WIN_COLS)
    dc = np.clip(kc[None, :] - qc[:, None], -(WIN_COLS - 1), WIN_COLS - 1) + (WIN_COLS - 1)
    onehot = (dc.reshape(-1)[None, :] == np.arange(2 * WIN_COLS - 1)[:, None]).astype(np.float32)
    band = jnp.dot(rpb.astype(F32).reshape(-1, 2 * WIN_COLS - 1), onehot,
                   precision=lax.Precision.HIGHEST)
    band = band.reshape(n_layers, N_HEADS, 2 * WIN_ROWS - 1, GRID_W, GRID_W)
    band = jnp.where(col_in[None, None, None], band * LOG2_E, MASK_VALUE)
    band = jnp.transpose(band, (0, 1, 3, 2, 4))
    t = jnp.stack([band[:, :, :, o:o + WIN_ROWS, :] for o in range(WIN_ROWS)], axis=1)
    return t.reshape(n_layers, WIN_ROWS, N_PAIRS, 2 * GRID_W, WIN_ROWS * GRID_W)


def _mixout_kernel(*refs, n_x, first, pending_moe, tail, n_chunks):
    n_in = n_x + (3 if pending_moe else 0)
    a_ref, u_ref, sv_ref, sgw_ref, sgb_ref, g_ref, w_ref, gffn_ref = refs[n_in:n_in + 8]
    if tail == "ffn":
        wg_ref, wu_ref, wd_ref, o_ref, sg_s = refs[n_in + 8:]
    else:
        rw_ref, tri_ref, o_ref, idx_ref, gate_ref, cnt_ref, sg_s, carry = refs[n_in + 8:]
    even = lax.broadcasted_iota(jnp.int32, (SG_CHUNK, LANES), 1) < HEAD_DIM
    for c in range(TM // SG_CHUNK):
        rows = slice(c * SG_CHUNK, (c + 1) * SG_CHUNK)
        for p in range(N_PAIRS):
            cols = slice(p * LANES, (p + 1) * LANES)
            m2 = jnp.dot(sgw_ref[p], sv_ref[rows, cols], preferred_element_type=F32)
            mixed = jnp.where(even, m2[:SG_CHUNK], m2[SG_CHUNK:])
            sg_s[rows, cols] = u_ref[rows, cols].astype(F32) * (mixed + sgb_ref[:, cols])
    sg = sg_s[...]
    sgn = (sg * _rms_scale(sg) * g_ref[:, HALF:]).astype(BF16)
    a = a_ref[...].astype(F32)
    an = (a * _rms_scale(a) * g_ref[:, :HALF]).astype(BF16)
    y = jnp.dot(an, w_ref[:HALF, :], preferred_element_type=F32)
    y = y + jnp.dot(sgn, w_ref[HALF:, :], preferred_element_type=F32)
    x = _stream_tile(refs[:n_x], first)
    if pending_moe:
        x = _add_pending(x, *refs[n_x:n_in])
    x = x + y
    if tail == "ffn":
        h = (x * _rms_scale(x) * gffn_ref[...]).astype(BF16)
        o_ref[...] = x + _swiglu(h, wg_ref, wu_ref, wd_ref, n_chunks)
    else:
        o_ref[...] = x

        @pl.when(pl.program_id(0) == 0)
        def _():
            carry[...] = jnp.zeros_like(carry)

        _route_tile(x, gffn_ref, rw_ref, tri_ref, carry, idx_ref, gate_ref, cnt_ref)


def _route_tile(x, g_ref, rw_ref, tri_ref, carry, idx_ref, gate_ref, cnt_ref):
    h = x * _rms_scale(x) * g_ref[...]
    h_hi = h.astype(BF16)
    h_lo = (h - h_hi.astype(F32)).astype(BF16)
    prod = jnp.dot(jnp.concatenate([h_hi, h_lo], axis=0), rw_ref[...], preferred_element_type=F32)
    logits = (prod[:TM, :LANES] + prod[:TM, LANES:]) + (prod[TM:, :LANES] + prod[TM:, LANES:])
    col = lax.broadcasted_iota(jnp.int32, logits.shape, 1)
    colf = col.astype(F32)
    neg = -jnp.inf
    lg = jnp.where(col < N_EXPERTS, logits, neg)
    m1 = jnp.max(lg, axis=-1, keepdims=True)
    i1 = jnp.min(jnp.where(lg == m1, colf, float(LANES)), axis=-1, keepdims=True)
    sel1 = colf == i1
    lg2 = jnp.where(sel1, neg, lg)
    m2 = jnp.max(lg2, axis=-1, keepdims=True)
    i2 = jnp.min(jnp.where(lg2 == m2, colf, float(LANES)), axis=-1, keepdims=True)
    sel2 = colf == i2
    e2 = jnp.exp(m2 - m1)
    g1 = 1.0 / (1.0 + e2)
    g2 = e2 / (1.0 + e2)

    cnt = jnp.where(sel1 | sel2, 1.0, 0.0)
    before = jnp.dot(tri_ref[...], cnt.astype(BF16), preferred_element_type=F32) + carry[...]
    r1 = jnp.sum(jnp.where(sel1, before, 0.0), axis=-1, keepdims=True)
    r2 = jnp.sum(jnp.where(sel2, before, 0.0), axis=-1, keepdims=True)
    total = carry[...] + jnp.sum(cnt, axis=0, keepdims=True)
    carry[...] = total
    cnt_ref[...] = jnp.broadcast_to(total, cnt_ref.shape)

    meta = jnp.where(col == 0, i1, jnp.where(col == 1, i2, jnp.where(col == 2, r1, r2)))
    idx_ref[...] = meta.astype(jnp.int32)
    gate_ref[...] = jnp.where(col == 0, g1, g2)


def _mixout(attn, u, svn, xs, sgw2, sgb_t, g_out, w_out, layer, g_ffn, ffn=None, route=None,
            pending_moe=None):
    T = sum(x.shape[0] for x in xs)
    D = xs[0].shape[1]
    half = pl.BlockSpec((TM, HALF), lambda i: (i, 0))
    full = pl.BlockSpec((TM, D), lambda i: (i, 0))
    meta = pl.BlockSpec((TM, LANES), lambda i: (i, 0))
    moe_args, moe_specs = _pending_operands(pending_moe, T, D)
    if ffn is not None:
        *tail_args, ffn_layer = ffn
        tail_specs = [_layer_spec(w, ffn_layer) for w in tail_args]
        n_chunks = _ff_chunks(tail_args[0].shape[-1])
    else:
        tail_args = list(route)
        tail_specs = [_const_spec(a.shape) for a in tail_args]
        n_chunks = None
    return pl.pallas_call(
        functools.partial(_mixout_kernel, n_x=len(xs), first=xs[0].shape[0] // TM,
                          pending_moe=pending_moe is not None,
                          tail="ffn" if ffn is not None else "route", n_chunks=n_chunks),
        grid=(T // TM,),
        in_specs=_stream_specs(xs) + moe_specs + [
            half, half, half, _const_spec(sgw2.shape), _const_spec(sgb_t.shape),
            _const_spec((1, 2 * HALF)), _layer_spec(w_out, layer), _const_spec((1, D))
        ] + tail_specs,
        out_specs=[full] + ([meta, meta, pl.BlockSpec((8, LANES), lambda i: (0, 0))] if route else []),
        out_shape=[jax.ShapeDtypeStruct((T, D), F32)] + (
            [jax.ShapeDtypeStruct((T, LANES), jnp.int32), jax.ShapeDtypeStruct((T, LANES), F32),
             jax.ShapeDtypeStruct((8, LANES), F32)] if route else []),
        scratch_shapes=[pltpu.VMEM((TM, HALF), F32)] + (
            [pltpu.VMEM((1, LANES), F32)] if route else []),
        compiler_params=_params(("arbitrary",) if route else ("parallel",)),
        name="mixout",
    )(*xs, *moe_args, attn, u, svn, sgw2, sgb_t, g_out, w_out, g_ffn, *tail_args)


def _ff_chunks(d_ff):
    for width in (MXU_WIDTH, LANES):
        for n in range(1, d_ff // width + 1):
            if d_ff % n == 0 and d_ff // n <= MAX_FF_CHUNK and (d_ff // n) % width == 0:
                return n
    raise ValueError(f"d_ff={d_ff} is not a multiple of {LANES}")


def _swiglu(h, wg_ref, wu_ref, wd_ref, n_chunks):
    d_ff = wd_ref.shape[-2]
    fc = d_ff // n_chunks
    y = None
    for c in range(n_chunks):
        cols = slice(c * fc, (c + 1) * fc)
        gate = jnp.dot(h, wg_ref[:, cols], preferred_element_type=F32)
        up = jnp.dot(h, wu_ref[:, cols], preferred_element_type=F32)
        act = (gate / (1.0 + jnp.exp(-gate)) * up).astype(BF16)
        part = jnp.dot(act, wd_ref[cols, :], preferred_element_type=F32)
        y = part if y is None else y + part
    return y


def _invert_kernel(lo_ref, hi_ref, dest_ref, out_ref):
    i = pl.program_id(0)

    @pl.when(i == 0)
    def _():
        def clear(j, carry):
            out_ref[j] = 0
            return carry

        for r in range(N_EXPERTS + 1):
            lax.fori_loop(lo_ref[r], hi_ref[r], clear, 0)

    first = i * INVERT_CHUNK + 1

    def place(a, carry):
        out_ref[dest_ref[a]] = first + a
        return carry

    lax.fori_loop(0, INVERT_CHUNK, place, 0, unroll=8)


def _invert(dest, empty_lo, empty_hi, n_rows):
    n = dest.shape[0]
    assert n % INVERT_CHUNK == 0
    grid_spec = pltpu.PrefetchScalarGridSpec(
        num_scalar_prefetch=2,
        grid=(n // INVERT_CHUNK,),
        in_specs=[pl.BlockSpec((INVERT_CHUNK,), lambda i, lo, hi: (i,), memory_space=pltpu.SMEM)],
        out_specs=pl.BlockSpec((n_rows,), lambda i, lo, hi: (0,), memory_space=pltpu.SMEM),
    )
    return pl.pallas_call(
        _invert_kernel,
        grid_spec=grid_spec,
        out_shape=jax.ShapeDtypeStruct((n_rows,), jnp.int32),
        compiler_params=_params(("arbitrary",)),
        name="invert",
    )(empty_lo, empty_hi, dest)


def _experts_kernel(te_ref, src_ref, dst_ref, x_hbm, g_ref, wg_ref, wu_ref, wd_ref, ys_hbm,
                    xbuf, obuf, gsem, ssem, *, n_chunks):
    del te_ref
    i = pl.program_id(0)
    slot = i % 2

    def gather_wait(s):
        pltpu.make_async_copy(x_hbm.at[pl.ds(0, EXPERT_TILE)], xbuf.at[s], gsem.at[s]).wait()

    @pl.when(i == 0)
    def _():
        xbuf[...] = jnp.zeros_like(xbuf)
        obuf[...] = jnp.zeros_like(obuf)

    @pl.when(i >= 1)
    def _():
        gather_wait(1 - slot)

    x = xbuf[1 - slot]
    h = (x * _rms_scale(x) * g_ref[...]).astype(BF16)
    for t in range(EXPERT_TILE):
        pltpu.make_async_copy(x_hbm.at[pl.ds(src_ref[t], 1)], xbuf.at[slot, pl.ds(t, 1)],
                              gsem.at[slot]).start()
        pltpu.make_async_copy(obuf.at[slot, pl.ds(t, 1)], ys_hbm.at[pl.ds(dst_ref[t], 1)],
                              ssem).start()
    y = _swiglu(h, wg_ref.at[0], wu_ref.at[0], wd_ref.at[0], n_chunks)
    pltpu.make_async_copy(obuf.at[slot], ys_hbm.at[pl.ds(0, EXPERT_TILE)], ssem).wait()
    obuf[1 - slot] = y

    @pl.when(i == pl.num_programs(0) - 1)
    def _():
        gather_wait(slot)


def _experts(x, g, src_tok, dst_row, tile_expert, wg, wu, wd, layer):
    T, D = x.shape
    d_ff = wg.shape[-1]
    n_tiles = src_tok.shape[0] // EXPERT_TILE
    n_rows = dst_row.shape[0]

    def w_spec(w):
        return pl.BlockSpec((None, 1) + w.shape[2:],
                            lambda i, te: (layer, te[jnp.clip(i - 1, 0, n_tiles - 1)], 0, 0),
                            pipeline_mode=pl.Buffered(1))

    grid_spec = pltpu.PrefetchScalarGridSpec(
        num_scalar_prefetch=1,
        grid=(n_tiles + 2,),
        in_specs=[pl.BlockSpec((EXPERT_TILE,), lambda i, te: (jnp.minimum(i, n_tiles - 1),),
                               memory_space=pltpu.SMEM),
                  pl.BlockSpec((EXPERT_TILE,), lambda i, te: (jnp.maximum(i - 1, 0),),
                               memory_space=pltpu.SMEM),
                  pl.BlockSpec(memory_space=pl.ANY),
                  pl.BlockSpec((1, D), lambda i, te: (0, 0), pipeline_mode=pl.Buffered(1)),
                  w_spec(wg), w_spec(wu), w_spec(wd)],
        out_specs=pl.BlockSpec(memory_space=pl.ANY),
        scratch_shapes=[pltpu.VMEM((2, EXPERT_TILE, D), F32), pltpu.VMEM((2, EXPERT_TILE, D), F32),
                        pltpu.SemaphoreType.DMA((2,)), pltpu.SemaphoreType.DMA(())],
    )
    return pl.pallas_call(
        functools.partial(_experts_kernel, n_chunks=_ff_chunks(d_ff)),
        grid_spec=grid_spec,
        out_shape=jax.ShapeDtypeStruct((n_rows, D), F32),
        compiler_params=_params(("arbitrary",)),
        name="experts",
    )(tile_expert, src_tok, dst_row, x, g, wg, wu, wd)


def _combine_kernel(x_ref, gate_ref, g_ref, y0_ref, y1_ref, o_ref, *, final_norm):
    gates = gate_ref[...]
    y = x_ref[...] + (y0_ref[...] * gates[:, 0:1] + y1_ref[...] * gates[:, 1:2])
    if final_norm:
        y = y * _rms_scale(y) * g_ref[...]
    o_ref[...] = y


def _combine(x, gates, ys, g_final, final_norm, first_block, n_blocks):
    T, D = x.shape
    rows = lambda i: (i + first_block, 0)
    return pl.pallas_call(
        functools.partial(_combine_kernel, final_norm=final_norm),
        grid=(n_blocks,),
        in_specs=[pl.BlockSpec((TM, D), rows), pl.BlockSpec((TM, LANES), rows), _const_spec((1, D)),
                  pl.BlockSpec((TM, D), rows),
                  pl.BlockSpec((TM, D), lambda i: (i + first_block + T // TM, 0))],
        out_specs=pl.BlockSpec((TM, D), lambda i: (i, 0)),
        out_shape=jax.ShapeDtypeStruct((n_blocks * TM, D), F32),
        compiler_params=_params(("parallel",)),
        name="combine",
    )(x, gates, g_final, ys, ys)


def _final_norm_kernel(x_ref, g_ref, o_ref):
    x = x_ref[...]
    o_ref[...] = x * _rms_scale(x) * g_ref[...]


def _final_norm(x, g):
    T, D = x.shape
    full = pl.BlockSpec((TM, D), lambda i: (i, 0))
    return pl.pallas_call(
        _final_norm_kernel,
        grid=(T // TM,),
        in_specs=[full, _const_spec((1, D))],
        out_specs=full,
        out_shape=jax.ShapeDtypeStruct((T, D), F32),
        compiler_params=_params(("parallel",)),
        name="final_norm",
    )(x, g)


def _route_operands(router_w):
    D = router_w.shape[0]
    rw_pad = jnp.zeros((D, LANES), F32).at[:, :N_EXPERTS].set(router_w)
    rw_hi = rw_pad.astype(BF16)
    rw2 = jnp.concatenate([rw_hi, (rw_pad - rw_hi.astype(F32)).astype(BF16)], axis=1)
    tri = jnp.asarray(np.tril(np.ones((TM, TM), np.float32), -1), BF16)
    return rw2, tri


def _moe(x, routing, g_ffn, wg, wu, wd, layer, g_final, final_norm, out_blocks):
    T, D = x.shape
    idx, gates, cnt = routing

    counts = cnt[0, :N_EXPERTS].astype(jnp.int32)
    padded = (counts + EXPERT_TILE - 1) // EXPERT_TILE * EXPERT_TILE
    pad_end = jnp.cumsum(padded)
    pad_start = pad_end - padded
    dest = (pad_start[idx[:, 0:2]] + idx[:, 2:4]).reshape(2 * T)
    n_rows = 2 * T + N_EXPERTS * EXPERT_TILE
    n_tiles = n_rows // EXPERT_TILE
    tile_start = jnp.arange(n_tiles, dtype=jnp.int32) * EXPERT_TILE
    tile_expert = jnp.minimum(
        jnp.sum(pad_end[None, :] <= tile_start[:, None], axis=1), N_EXPERTS - 1).astype(jnp.int32)

    empty_lo = jnp.concatenate([pad_start + counts, pad_end[-1:]]).astype(jnp.int32)
    empty_hi = jnp.concatenate([pad_end, jnp.full((1,), n_rows)]).astype(jnp.int32)
    filled = _invert(dest, empty_lo, empty_hi, n_rows)
    is_pad = filled == 0
    pad_rank = jnp.cumsum(is_pad.astype(jnp.int32)) - 1
    assign = filled - 1
    src_tok = jnp.where(is_pad, 0, assign // 2)
    dst_row = jnp.where(is_pad, 2 * T + pad_rank, (assign % 2) * T + assign // 2)
    spare = n_rows + jnp.arange(EXPERT_TILE, dtype=jnp.int32)
    dst_row = jnp.concatenate([spare, dst_row])

    ys = _experts(x, g_ffn, src_tok, dst_row, tile_expert, wg, wu, wd, layer)
    if out_blocks is None:
        return ys, gates
    return [_combine(x, gates, ys, g_final, final_norm, b0, nb) for b0, nb in out_blocks]


def _image_blocks(image_rows):
    lo, hi, start = [], [], 0
    for rows in image_rows:
        n = rows // ROWS_PER_BLOCK
        lo += [start] * n
        hi += [start + n - 1] * n
        start += n
    return jnp.asarray(lo, jnp.int32), jnp.asarray(hi, jnp.int32)


def kernel(x_prompt, x_sample, norm_mix_g, w_in, na_rpb, sg_norm_g, sg_w, sg_b, out_norm_g, w_out,
           norm_ffn_g, dense_w_gate, dense_w_up, dense_w_down,
           router_w, moe_w_gate, moe_w_up, moe_w_down, final_norm_g):
    D = x_prompt.shape[-1]
    depth = w_in.shape[0]
    assert D == 2 * HALF
    image_rows = []
    for xs in (x_prompt, x_sample):
        assert xs.shape[1] % TM == 0 and xs.shape[1] >= WIN_BLOCKS * TM
        image_rows += [xs.shape[1] // GRID_W] * xs.shape[0]
    blk_lo, blk_hi = _image_blocks(image_rows)
    n_prompt = x_prompt.shape[0] * x_prompt.shape[1]

    xs = (x_prompt.reshape(-1, D), x_sample.reshape(-1, D))
    bd = jnp.asarray(np.kron(np.eye(N_HEADS, dtype=np.float32),
                             np.ones((HEAD_DIM, HEAD_DIM), np.float32)), BF16)
    g_final = final_norm_g.reshape(1, D)
    na_bias = _na_bias_tables(na_rpb)

    w_in, w_out, dense_w_gate, dense_w_up, dense_w_down, moe_w_gate, moe_w_up, moe_w_down = (
        w.astype(BF16) for w in (w_in, w_out, dense_w_gate, dense_w_up, dense_w_down,
                                 moe_w_gate, moe_w_up, moe_w_down))

    pending_moe = None
    for l in range(depth):
        q, k, v, u, svn = _inproj(xs, norm_mix_g[l].reshape(1, D), w_in, l,
                                  sg_norm_g[l].reshape(1, HALF), bd, pending_moe)
        attn = _natten(q, k, v, na_bias[l], blk_lo, blk_hi)
        sgw2 = sg_w[l].astype(BF16).reshape(N_PAIRS, 2 * SG_CHUNK, SG_CHUNK)
        sgb_t = jnp.repeat(jnp.transpose(sg_b[l]), HEAD_DIM, axis=1)
        g_ffn = norm_ffn_g[l].reshape(1, D)
        i = l // 2
        last = l == depth - 1
        dense = l % 2 == 0
        x, *routing = _mixout(attn, u, svn, xs, sgw2, sgb_t, out_norm_g[l].reshape(1, D), w_out, l,
                              g_ffn,
                              ffn=(dense_w_gate, dense_w_up, dense_w_down, i) if dense else None,
                              route=None if dense else _route_operands(router_w[i]),
                              pending_moe=pending_moe)
        pending_moe = None
        if dense:
            if last:
                x = _final_norm(x, g_final)
                outs = [x[:n_prompt], x[n_prompt:]]
        elif last:
            nb, npb = x.shape[0] // TM, n_prompt // TM
            outs = _moe(x, routing, g_ffn, moe_w_gate, moe_w_up, moe_w_down, i, g_final,
                        final_norm=True, out_blocks=[(0, npb), (npb, nb - npb)])
        else:
            pending_moe = _moe(x, routing, g_ffn, moe_w_gate, moe_w_up, moe_w_down, i, g_final,
                               final_norm=False, out_blocks=None)
        xs = (x,)
    return (outs[0].reshape(x_prompt.shape), outs[1].reshape(x_sample.shape))
```

```python
import functools

import numpy as np
import jax
import jax.numpy as jnp
from jax import lax
from jax.experimental import pallas as pl
from jax.experimental.pallas import tpu as pltpu

F32 = jnp.float32
BF16 = jnp.bfloat16

GRID_W = 64
WIN_ROWS = 8
WIN_COLS = 16
N_HEADS = 8
HEAD_DIM = 64
HALF = N_HEADS * HEAD_DIM
N_PAIRS = N_HEADS // 2
LANES = 128
SG_CHUNK = 128
N_EXPERTS = 8
RMS_EPS = 1e-6
MASK_VALUE = -1e30

TM = 512
ROWS_PER_BLOCK = TM // GRID_W
WIN_BLOCKS = 3
LOG2_E = 1.4426950408889634
EXPERT_TILE = 512
INVERT_CHUNK = 2048
MXU_WIDTH = 256
MAX_FF_CHUNK = 3072
VMEM_LIMIT = 56 * 1024 * 1024


def _params(sem, vmem=VMEM_LIMIT):
    return pltpu.CompilerParams(dimension_semantics=sem, vmem_limit_bytes=vmem)


def _const_spec(shape):
    nd = len(shape)
    return pl.BlockSpec(shape, lambda *_: (0,) * nd, pipeline_mode=pl.Buffered(1))


def _layer_spec(w, layer):
    nd = w.ndim - 1
    return pl.BlockSpec((None,) + w.shape[1:], lambda *_: (layer,) + (0,) * nd,
                        pipeline_mode=pl.Buffered(1))


def _rms_scale(x):
    return lax.rsqrt(jnp.mean(x * x, axis=-1, keepdims=True) + RMS_EPS)


def _stream_specs(parts):
    D = parts[0].shape[1]
    if len(parts) == 1:
        return [pl.BlockSpec((TM, D), lambda i: (i, 0))]
    first = parts[0].shape[0] // TM
    return [pl.BlockSpec((TM, D), lambda i: (jnp.minimum(i, first - 1), 0)),
            pl.BlockSpec((TM, D), lambda i: (jnp.maximum(i - first, 0), 0))]


def _stream_tile(refs, first):
    if len(refs) == 1:
        return refs[0][...]
    return jnp.where(pl.program_id(0) < first, refs[0][...], refs[1][...])


def _gelu_tanh(x):
    return x * (0.5 * (1.0 + jnp.tanh(0.7978845608028654 * (x + 0.044715 * (x * x * x)))))


def _inproj_kernel(*refs, n_x, first, pending_moe):
    x = _stream_tile(refs[:n_x], first)
    if pending_moe:
        y0_ref, y1_ref, gate_ref = refs[n_x:n_x + 3]
        g_ref, w_ref, sgg_ref, bd_ref, q_ref, k_ref, v_ref, u_ref, sv_ref, x_out_ref = refs[n_x + 3:]
        gates = gate_ref[...]
        x = x + (y0_ref[...] * gates[:, 0:1] + y1_ref[...] * gates[:, 1:2])
        x_out_ref[...] = x
    else:
        g_ref, w_ref, sgg_ref, bd_ref, q_ref, k_ref, v_ref, u_ref, sv_ref = refs[n_x:]
    h = (x * _rms_scale(x) * g_ref[...]).astype(BF16)

    def proj(n):
        return jnp.dot(h, w_ref[:, n * HALF:(n + 1) * HALF], preferred_element_type=F32)

    u_ref[...] = _gelu_tanh(proj(3)).astype(BF16)
    sv = _gelu_tanh(proj(4))
    ssq = jnp.dot((sv * sv).astype(BF16), bd_ref[...], preferred_element_type=F32)
    sv_ref[...] = (sv * lax.rsqrt(ssq * (1.0 / HEAD_DIM) + RMS_EPS) * sgg_ref[...]).astype(BF16)

    q = proj(0) * (HEAD_DIM ** -0.5 * LOG2_E)
    is_even = (lax.broadcasted_iota(jnp.int32, q.shape, 1) % LANES) < HEAD_DIM
    qe = jnp.where(is_even, q, 0.0).astype(BF16)
    qo = jnp.where(is_even, 0.0, q).astype(BF16)
    for r in range(ROWS_PER_BLOCK):
        rows = slice(r * GRID_W, (r + 1) * GRID_W)
        q_ref[2 * r * GRID_W:(2 * r + 1) * GRID_W, :] = qe[rows]
        q_ref[(2 * r + 1) * GRID_W:(2 * r + 2) * GRID_W, :] = qo[rows]
    k_ref[...] = proj(1).astype(BF16)
    v_ref[...] = proj(2).astype(BF16)


def _inproj(xs, g, w_in, layer, sgg, bd, pending_moe=None):
    T = sum(x.shape[0] for x in xs)
    D = xs[0].shape[1]
    out = jax.ShapeDtypeStruct((T, HALF), BF16)
    tile = pl.BlockSpec((TM, HALF), lambda i: (i, 0))
    full = pl.BlockSpec((TM, D), lambda i: (i, 0))
    moe_args, moe_specs, moe_out_specs, moe_out_shape = [], [], [], []
    if pending_moe is not None:
        ys, gates = pending_moe
        moe_args = [ys, ys, gates]
        moe_specs = [full, pl.BlockSpec((TM, D), lambda i: (i + T // TM, 0)),
                     pl.BlockSpec((TM, LANES), lambda i: (i, 0))]
        moe_out_specs = [full]
        moe_out_shape = [jax.ShapeDtypeStruct((T, D), F32)]
    return pl.pallas_call(
        functools.partial(_inproj_kernel, n_x=len(xs), first=xs[0].shape[0] // TM,
                          pending_moe=pending_moe is not None),
        grid=(T // TM,),
        in_specs=_stream_specs(xs) + moe_specs + [
            _const_spec((1, D)), _layer_spec(w_in, layer),
            _const_spec((1, HALF)), _const_spec((HALF, HALF))],
        out_specs=[pl.BlockSpec((2 * TM, HALF), lambda i: (i, 0))] + [tile] * 4 + moe_out_specs,
        out_shape=[jax.ShapeDtypeStruct((2 * T, HALF), BF16)] + [out] * 4 + moe_out_shape,
        compiler_params=_params(("parallel",)),
        name="inproj",
    )(*xs, *moe_args, g, w_in, sgg, bd)


def _row_plan(blk, n_blk):
    win_row0 = min(max(blk - 1, 0), n_blk - WIN_BLOCKS) * ROWS_PER_BLOCK
    plan = []
    for i in range(ROWS_PER_BLOCK):
        r = blk * ROWS_PER_BLOCK + i
        rs = min(max(r - WIN_ROWS // 2, 0), n_blk * ROWS_PER_BLOCK - WIN_ROWS)
        plan.append((rs - win_row0, rs - r + WIN_ROWS - 1))
    return plan


def _natten_kernel(lo_ref, hi_ref, q_ref, kwin, vwin, bias_ref, o_ref):
    b = pl.program_id(0)
    blk = b - lo_ref[b]
    n_blk = hi_ref[b] - lo_ref[b] + 1
    out_even = lax.broadcasted_iota(jnp.int32, (GRID_W, LANES), 1) < HEAD_DIM
    n_keys = WIN_ROWS * GRID_W

    def rows(plan):
        for i, (ws, variant) in enumerate(plan):
            scores = []
            for p in range(N_PAIRS):
                cols = slice(p * LANES, (p + 1) * LANES)
                s = lax.dot_general(q_ref[2 * i * GRID_W:2 * (i + 1) * GRID_W, cols],
                                    kwin[ws * GRID_W:ws * GRID_W + n_keys, cols],
                                    (((1,), (1,)), ((), ())), preferred_element_type=F32)
                scores.append(s + bias_ref[variant, p])
            for p in range(N_PAIRS):
                cols = slice(p * LANES, (p + 1) * LANES)
                s = scores[p]
                e = jnp.exp2(s - jnp.max(s, axis=-1, keepdims=True))
                denom = jnp.sum(e, axis=-1, keepdims=True)
                o2 = jnp.dot(e.astype(BF16), vwin[ws * GRID_W:ws * GRID_W + n_keys, cols],
                             preferred_element_type=F32) * (1.0 / denom)
                out = jnp.where(out_even, o2[:GRID_W], o2[GRID_W:])
                o_ref[i * GRID_W:(i + 1) * GRID_W, cols] = out.astype(BF16)

    model_blocks = WIN_BLOCKS + 2
    inner = _row_plan(1, model_blocks)
    assert all(_row_plan(k, model_blocks) == inner for k in range(1, model_blocks - 1))
    pl.when(blk == 0)(lambda: rows(_row_plan(0, model_blocks)))
    pl.when(blk == n_blk - 1)(lambda: rows(_row_plan(model_blocks - 1, model_blocks)))
    pl.when((blk > 0) & (blk < n_blk - 1))(lambda: rows(inner))


def _natten(q2, k, v, bias, blk_lo, blk_hi):
    T = v.shape[0]

    def win_map(b, lo, hi):
        return (jnp.clip(b - 1, lo[b], hi[b] - (WIN_BLOCKS - 1)) * TM, 0)

    window = pl.BlockSpec((pl.Element(WIN_BLOCKS * TM), pl.Element(HALF)), win_map)
    grid_spec = pltpu.PrefetchScalarGridSpec(
        num_scalar_prefetch=2,
        grid=(T // TM,),
        in_specs=[pl.BlockSpec((2 * TM, HALF), lambda b, lo, hi: (b, 0)), window, window,
                  pl.BlockSpec(bias.shape, lambda b, lo, hi: (0, 0, 0, 0),
                               pipeline_mode=pl.Buffered(1))],
        out_specs=pl.BlockSpec((TM, HALF), lambda b, lo, hi: (b, 0)),
    )
    return pl.pallas_call(
        _natten_kernel,
        grid_spec=grid_spec,
        out_shape=jax.ShapeDtypeStruct((T, HALF), BF16),
        compiler_params=_params(("parallel",)),
        name="natten",
    )(blk_lo, blk_hi, q2, k, v, bias)


def _na_bias_tables(rpb):
    n_layers = rpb.shape[0]
    qc = np.arange(GRID_W)
    kc = np.arange(GRID_W)
    cs = np.clip(qc - WIN_COLS // 2, 0, GRID_W - WIN_COLS)
    col_in = (kc[None, :] >= cs[:, None]) & (kc[None, :] < cs[:, None] + WIN_COLS)
    dc = np.clip(kc[None, :] - qc[:, None], -(WIN_COLS - 1), WIN_COLS - 1) + (WIN_COLS - 1)
    onehot = (dc.reshape(-1)[None, :] == np.arange(2 * WIN_COLS - 1)[:, None]).astype(np.float32)
    band = jnp.dot(rpb.astype(F32).reshape(-1, 2 * WIN_COLS - 1), onehot,
                   precision=lax.Precision.HIGHEST)
    band = band.reshape(n_layers, N_HEADS, 2 * WIN_ROWS - 1, GRID_W, GRID_W)
    band = jnp.where(col_in[None, None, None], band * LOG2_E, MASK_VALUE)
    band = jnp.transpose(band, (0, 1, 3, 2, 4))
    t = jnp.stack([band[:, :, :, o:o + WIN_ROWS, :] for o in range(WIN_ROWS)], axis=1)
    return t.reshape(n_layers, WIN_ROWS, N_PAIRS, 2 * GRID_W, WIN_ROWS * GRID_W)


def _mixout_kernel(*refs, n_x, first, tail, n_chunks):
    a_ref, u_ref, sv_ref, sgw_ref, sgb_ref, g_ref, w_ref, gffn_ref = refs[n_x:n_x + 8]
    if tail == "ffn":
        wg_ref, wu_ref, wd_ref, o_ref, sg_s = refs[n_x + 8:]
    else:
        rw_ref, tri_ref, o_ref, idx_ref, gate_ref, cnt_ref, sg_s, carry, x_prev = refs[n_x + 8:]

        @pl.when(pl.program_id(0) == 0)
        def _():
            carry[...] = jnp.zeros_like(carry)
            x_prev[...] = jnp.zeros_like(x_prev)

        x_old = x_prev[...]
    even = lax.broadcasted_iota(jnp.int32, (SG_CHUNK, LANES), 1) < HEAD_DIM
    for c in range(TM // SG_CHUNK):
        rows = slice(c * SG_CHUNK, (c + 1) * SG_CHUNK)
        for p in range(N_PAIRS):
            cols = slice(p * LANES, (p + 1) * LANES)
            m2 = jnp.dot(sgw_ref[p], sv_ref[rows, cols], preferred_element_type=F32)
            mixed = jnp.where(even, m2[:SG_CHUNK], m2[SG_CHUNK:])
            sg_s[rows, cols] = u_ref[rows, cols].astype(F32) * (mixed + sgb_ref[:, cols])
    sg = sg_s[...]
    sgn = (sg * _rms_scale(sg) * g_ref[:, HALF:]).astype(BF16)
    a = a_ref[...].astype(F32)
    an = (a * _rms_scale(a) * g_ref[:, :HALF]).astype(BF16)
    y = jnp.dot(an, w_ref[:HALF, :], preferred_element_type=F32)
    y = y + jnp.dot(sgn, w_ref[HALF:, :], preferred_element_type=F32)
    x = _stream_tile(refs[:n_x], first) + y
    if tail == "ffn":
        h = (x * _rms_scale(x) * gffn_ref[...]).astype(BF16)
        o_ref[...] = x + _swiglu(h, wg_ref, wu_ref, wd_ref, n_chunks)
    else:
        o_ref[...] = x
        x_prev[...] = x
        _route_tile(x_old, gffn_ref, rw_ref, tri_ref, carry, idx_ref, gate_ref, cnt_ref,
                    valid=(pl.program_id(0) >= 1).astype(F32))


def _route_tile(x, g_ref, rw_ref, tri_ref, carry, idx_ref, gate_ref, cnt_ref, valid):
    h = x * _rms_scale(x) * g_ref[...]
    h_hi = h.astype(BF16)
    h_lo = (h - h_hi.astype(F32)).astype(BF16)
    prod = jnp.dot(jnp.concatenate([h_hi, h_lo], axis=0), rw_ref[...], preferred_element_type=F32)
    logits = (prod[:TM, :LANES] + prod[:TM, LANES:]) + (prod[TM:, :LANES] + prod[TM:, LANES:])
    col = lax.broadcasted_iota(jnp.int32, logits.shape, 1)
    colf = col.astype(F32)
    neg = -jnp.inf
    lg = jnp.where(col < N_EXPERTS, logits, neg)
    m1 = jnp.max(lg, axis=-1, keepdims=True)
    i1 = jnp.min(jnp.where(lg == m1, colf, float(LANES)), axis=-1, keepdims=True)
    sel1 = colf == i1
    lg2 = jnp.where(sel1, neg, lg)
    m2 = jnp.max(lg2, axis=-1, keepdims=True)
    i2 = jnp.min(jnp.where(lg2 == m2, colf, float(LANES)), axis=-1, keepdims=True)
    sel2 = colf == i2
    e2 = jnp.exp(m2 - m1)
    g1 = 1.0 / (1.0 + e2)
    g2 = e2 / (1.0 + e2)

    cnt = jnp.where(sel1 | sel2, 1.0, 0.0) * valid
    before = jnp.dot(tri_ref[...], cnt.astype(BF16), preferred_element_type=F32) + carry[...]
    r1 = jnp.sum(jnp.where(sel1, before, 0.0), axis=-1, keepdims=True)
    r2 = jnp.sum(jnp.where(sel2, before, 0.0), axis=-1, keepdims=True)
    total = carry[...] + jnp.sum(cnt, axis=0, keepdims=True)
    carry[...] = total
    cnt_ref[...] = jnp.broadcast_to(total, cnt_ref.shape)

    meta = jnp.where(col == 0, i1, jnp.where(col == 1, i2, jnp.where(col == 2, r1, r2)))
    idx_ref[...] = meta.astype(jnp.int32)
    gate_ref[...] = jnp.where(col == 0, g1, g2)


def _mixout(attn, u, svn, xs, sgw2, sgb_t, g_out, w_out, layer, g_ffn, ffn=None, route=None):
    T = sum(x.shape[0] for x in xs)
    D = xs[0].shape[1]
    n_tiles = T // TM
    n_steps = n_tiles + 1 if route else n_tiles
    tile_map = lambda i: (jnp.minimum(i, n_tiles - 1), 0)
    half = pl.BlockSpec((TM, HALF), tile_map)
    full = pl.BlockSpec((TM, D), tile_map)
    meta = pl.BlockSpec((TM, LANES), lambda i: (jnp.maximum(i - 1, 0), 0))
    x_specs = [full] if len(xs) == 1 else _stream_specs(xs)
    assert not (route and len(xs) > 1)
    if ffn is not None:
        *tail_args, ffn_layer = ffn
        tail_specs = [_layer_spec(w, ffn_layer) for w in tail_args]
        n_chunks = _ff_chunks(tail_args[0].shape[-1])
    else:
        tail_args = list(route)
        tail_specs = [_const_spec(a.shape) for a in tail_args]
        n_chunks = None
    return pl.pallas_call(
        functools.partial(_mixout_kernel, n_x=len(xs), first=xs[0].shape[0] // TM,
                          tail="ffn" if ffn is not None else "route", n_chunks=n_chunks),
        grid=(n_steps,),
        in_specs=x_specs + [
            half, half, half, _const_spec(sgw2.shape), _const_spec(sgb_t.shape),
            _const_spec((1, 2 * HALF)), _layer_spec(w_out, layer), _const_spec((1, D))
        ] + tail_specs,
        out_specs=[full] + ([meta, meta, pl.BlockSpec((8, LANES), lambda i: (0, 0))] if route else []),
        out_shape=[jax.ShapeDtypeStruct((T, D), F32)] + (
            [jax.ShapeDtypeStruct((T, LANES), jnp.int32), jax.ShapeDtypeStruct((T, LANES), F32),
             jax.ShapeDtypeStruct((8, LANES), F32)] if route else []),
        scratch_shapes=[pltpu.VMEM((TM, HALF), F32)] + (
            [pltpu.VMEM((1, LANES), F32), pltpu.VMEM((TM, D), F32)] if route else []),
        compiler_params=_params(("arbitrary",) if route else ("parallel",)),
        name="mixout",
    )(*xs, attn, u, svn, sgw2, sgb_t, g_out, w_out, g_ffn, *tail_args)


def _ff_chunks(d_ff):
    for width in (MXU_WIDTH, LANES):
        for n in range(1, d_ff // width + 1):
            if d_ff % n == 0 and d_ff // n <= MAX_FF_CHUNK and (d_ff // n) % width == 0:
                return n
    raise ValueError(f"d_ff={d_ff} is not a multiple of {LANES}")


def _swiglu(h, wg_ref, wu_ref, wd_ref, n_chunks):
    d_ff = wd_ref.shape[-2]
    fc = d_ff // n_chunks
    y = None
    for c in range(n_chunks):
        cols = slice(c * fc, (c + 1) * fc)
        gate = jnp.dot(h, wg_ref[:, cols], preferred_element_type=F32)
        up = jnp.dot(h, wu_ref[:, cols], preferred_element_type=F32)
        act = (gate / (1.0 + jnp.exp(-gate)) * up).astype(BF16)
        part = jnp.dot(act, wd_ref[cols, :], preferred_element_type=F32)
        y = part if y is None else y + part
    return y


def _invert_kernel(lo_ref, hi_ref, dest_ref, out_ref):
    i = pl.program_id(0)

    @pl.when(i == 0)
    def _():
        def clear(j, carry):
            out_ref[j] = 0
            return carry

        for r in range(N_EXPERTS + 1):
            lax.fori_loop(lo_ref[r], hi_ref[r], clear, 0)

    first = i * INVERT_CHUNK + 1

    def place(a, carry):
        out_ref[dest_ref[a]] = first + a
        return carry

    lax.fori_loop(0, INVERT_CHUNK, place, 0, unroll=8)


def _invert(dest, empty_lo, empty_hi, n_rows):
    n = dest.shape[0]
    assert n % INVERT_CHUNK == 0
    grid_spec = pltpu.PrefetchScalarGridSpec(
        num_scalar_prefetch=2,
        grid=(n // INVERT_CHUNK,),
        in_specs=[pl.BlockSpec((INVERT_CHUNK,), lambda i, lo, hi: (i,), memory_space=pltpu.SMEM)],
        out_specs=pl.BlockSpec((n_rows,), lambda i, lo, hi: (0,), memory_space=pltpu.SMEM),
    )
    return pl.pallas_call(
        _invert_kernel,
        grid_spec=grid_spec,
        out_shape=jax.ShapeDtypeStruct((n_rows,), jnp.int32),
        compiler_params=_params(("arbitrary",)),
        name="invert",
    )(empty_lo, empty_hi, dest)


def _experts_kernel(te_ref, src_ref, dst_ref, x_hbm, g_ref, wg_ref, wu_ref, wd_ref, ys_hbm,
                    xbuf, obuf, gsem, ssem, *, n_chunks):
    del te_ref
    i = pl.program_id(0)
    slot = i % 2

    def gather_wait(s):
        pltpu.make_async_copy(x_hbm.at[pl.ds(0, EXPERT_TILE)], xbuf.at[s], gsem.at[s]).wait()

    @pl.when(i == 0)
    def _():
        xbuf[...] = jnp.zeros_like(xbuf)
        obuf[...] = jnp.zeros_like(obuf)

    @pl.when(i >= 1)
    def _():
        gather_wait(1 - slot)

    x = xbuf[1 - slot]
    h = (x * _rms_scale(x) * g_ref[...]).astype(BF16)
    for t in range(EXPERT_TILE):
        pltpu.make_async_copy(x_hbm.at[pl.ds(src_ref[t], 1)], xbuf.at[slot, pl.ds(t, 1)],
                              gsem.at[slot]).start()
        pltpu.make_async_copy(obuf.at[slot, pl.ds(t, 1)], ys_hbm.at[pl.ds(dst_ref[t], 1)],
                              ssem).start()
    y = _swiglu(h, wg_ref.at[0], wu_ref.at[0], wd_ref.at[0], n_chunks)
    pltpu.make_async_copy(obuf.at[slot], ys_hbm.at[pl.ds(0, EXPERT_TILE)], ssem).wait()
    obuf[1 - slot] = y

    @pl.when(i == pl.num_programs(0) - 1)
    def _():
        gather_wait(slot)


def _experts(x, g, src_tok, dst_row, tile_expert, wg, wu, wd, layer):
    T, D = x.shape
    d_ff = wg.shape[-1]
    n_tiles = src_tok.shape[0] // EXPERT_TILE
    n_rows = dst_row.shape[0]

    def w_spec(w):
        return pl.BlockSpec((None, 1) + w.shape[2:],
                            lambda i, te: (layer, te[jnp.clip(i - 1, 0, n_tiles - 1)], 0, 0),
                            pipeline_mode=pl.Buffered(1))

    grid_spec = pltpu.PrefetchScalarGridSpec(
        num_scalar_prefetch=1,
        grid=(n_tiles + 2,),
        in_specs=[pl.BlockSpec((EXPERT_TILE,), lambda i, te: (jnp.minimum(i, n_tiles - 1),),
                               memory_space=pltpu.SMEM),
                  pl.BlockSpec((EXPERT_TILE,), lambda i, te: (jnp.maximum(i - 1, 0),),
                               memory_space=pltpu.SMEM),
                  pl.BlockSpec(memory_space=pl.ANY),
                  pl.BlockSpec((1, D), lambda i, te: (0, 0), pipeline_mode=pl.Buffered(1)),
                  w_spec(wg), w_spec(wu), w_spec(wd)],
        out_specs=pl.BlockSpec(memory_space=pl.ANY),
        scratch_shapes=[pltpu.VMEM((2, EXPERT_TILE, D), F32), pltpu.VMEM((2, EXPERT_TILE, D), F32),
                        pltpu.SemaphoreType.DMA((2,)), pltpu.SemaphoreType.DMA(())],
    )
    return pl.pallas_call(
        functools.partial(_experts_kernel, n_chunks=_ff_chunks(d_ff)),
        grid_spec=grid_spec,
        out_shape=jax.ShapeDtypeStruct((n_rows, D), F32),
        compiler_params=_params(("arbitrary",)),
        name="experts",
    )(tile_expert, src_tok, dst_row, x, g, wg, wu, wd)


def _combine_kernel(x_ref, gate_ref, g_ref, y0_ref, y1_ref, o_ref, *, final_norm):
    gates = gate_ref[...]
    y = x_ref[...] + (y0_ref[...] * gates[:, 0:1] + y1_ref[...] * gates[:, 1:2])
    if final_norm:
        y = y * _rms_scale(y) * g_ref[...]
    o_ref[...] = y


def _combine(x, gates, ys, g_final, final_norm, first_block, n_blocks):
    T, D = x.shape
    rows = lambda i: (i + first_block, 0)
    return pl.pallas_call(
        functools.partial(_combine_kernel, final_norm=final_norm),
        grid=(n_blocks,),
        in_specs=[pl.BlockSpec((TM, D), rows), pl.BlockSpec((TM, LANES), rows), _const_spec((1, D)),
                  pl.BlockSpec((TM, D), rows),
                  pl.BlockSpec((TM, D), lambda i: (i + first_block + T // TM, 0))],
        out_specs=pl.BlockSpec((TM, D), lambda i: (i, 0)),
        out_shape=jax.ShapeDtypeStruct((n_blocks * TM, D), F32),
        compiler_params=_params(("parallel",)),
        name="combine",
    )(x, gates, g_final, ys, ys)


def _final_norm_kernel(x_ref, g_ref, o_ref):
    x = x_ref[...]
    o_ref[...] = x * _rms_scale(x) * g_ref[...]


def _final_norm(x, g):
    T, D = x.shape
    full = pl.BlockSpec((TM, D), lambda i: (i, 0))
    return pl.pallas_call(
        _final_norm_kernel,
        grid=(T // TM,),
        in_specs=[full, _const_spec((1, D))],
        out_specs=full,
        out_shape=jax.ShapeDtypeStruct((T, D), F32),
        compiler_params=_params(("parallel",)),
        name="final_norm",
    )(x, g)


def _route_operands(router_w):
    D = router_w.shape[0]
    rw_pad = jnp.zeros((D, LANES), F32).at[:, :N_EXPERTS].set(router_w)
    rw_hi = rw_pad.astype(BF16)
    rw2 = jnp.concatenate([rw_hi, (rw_pad - rw_hi.astype(F32)).astype(BF16)], axis=1)
    tri = jnp.asarray(np.tril(np.ones((TM, TM), np.float32), -1), BF16)
    return rw2, tri


def _moe(x, routing, g_ffn, wg, wu, wd, layer, g_final, final_norm, out_blocks):
    T, D = x.shape
    idx, gates, cnt = routing

    counts = cnt[0, :N_EXPERTS].astype(jnp.int32)
    padded = (counts + EXPERT_TILE - 1) // EXPERT_TILE * EXPERT_TILE
    pad_end = jnp.cumsum(padded)
    pad_start = pad_end - padded
    dest = (pad_start[idx[:, 0:2]] + idx[:, 2:4]).reshape(2 * T)
    n_rows = 2 * T + N_EXPERTS * EXPERT_TILE
    n_tiles = n_rows // EXPERT_TILE
    tile_start = jnp.arange(n_tiles, dtype=jnp.int32) * EXPERT_TILE
    tile_expert = jnp.minimum(
        jnp.sum(pad_end[None, :] <= tile_start[:, None], axis=1), N_EXPERTS - 1).astype(jnp.int32)

    empty_lo = jnp.concatenate([pad_start + counts, pad_end[-1:]]).astype(jnp.int32)
    empty_hi = jnp.concatenate([pad_end, jnp.full((1,), n_rows)]).astype(jnp.int32)
    filled = _invert(dest, empty_lo, empty_hi, n_rows)
    is_pad = filled == 0
    pad_rank = jnp.cumsum(is_pad.astype(jnp.int32)) - 1
    assign = filled - 1
    src_tok = jnp.where(is_pad, 0, assign // 2)
    dst_row = jnp.where(is_pad, 2 * T + pad_rank, (assign % 2) * T + assign // 2)
    spare = n_rows + jnp.arange(EXPERT_TILE, dtype=jnp.int32)
    dst_row = jnp.concatenate([spare, dst_row])

    ys = _experts(x, g_ffn, src_tok, dst_row, tile_expert, wg, wu, wd, layer)
    if out_blocks is None:
        return ys, gates
    return [_combine(x, gates, ys, g_final, final_norm, b0, nb) for b0, nb in out_blocks]


def _image_blocks(image_rows):
    lo, hi, start = [], [], 0
    for rows in image_rows:
        n = rows // ROWS_PER_BLOCK
        lo += [start] * n
        hi += [start + n - 1] * n
        start += n
    return jnp.asarray(lo, jnp.int32), jnp.asarray(hi, jnp.int32)


def kernel(x_prompt, x_sample, norm_mix_g, w_in, na_rpb, sg_norm_g, sg_w, sg_b, out_norm_g, w_out,
           norm_ffn_g, dense_w_gate, dense_w_up, dense_w_down,
           router_w, moe_w_gate, moe_w_up, moe_w_down, final_norm_g):
    D = x_prompt.shape[-1]
    depth = w_in.shape[0]
    assert D == 2 * HALF
    image_rows = []
    for xs in (x_prompt, x_sample):
        assert xs.shape[1] % TM == 0 and xs.shape[1] >= WIN_BLOCKS * TM
        image_rows += [xs.shape[1] // GRID_W] * xs.shape[0]
    blk_lo, blk_hi = _image_blocks(image_rows)
    n_prompt = x_prompt.shape[0] * x_prompt.shape[1]

    xs = (x_prompt.reshape(-1, D), x_sample.reshape(-1, D))
    bd = jnp.asarray(np.kron(np.eye(N_HEADS, dtype=np.float32),
                             np.ones((HEAD_DIM, HEAD_DIM), np.float32)), BF16)
    g_final = final_norm_g.reshape(1, D)
    na_bias = _na_bias_tables(na_rpb)

    w_in, w_out, dense_w_gate, dense_w_up, dense_w_down, moe_w_gate, moe_w_up, moe_w_down = (
        w.astype(BF16) for w in (w_in, w_out, dense_w_gate, dense_w_up, dense_w_down,
                                 moe_w_gate, moe_w_up, moe_w_down))

    pending_moe = None
    for l in range(depth):
        q, k, v, u, svn, *x_new = _inproj(xs, norm_mix_g[l].reshape(1, D), w_in, l,
                                          sg_norm_g[l].reshape(1, HALF), bd, pending_moe)
        if pending_moe is not None:
            xs, pending_moe = (x_new[0],), None
        attn = _natten(q, k, v, na_bias[l], blk_lo, blk_hi)
        sgw2 = sg_w[l].astype(BF16).reshape(N_PAIRS, 2 * SG_CHUNK, SG_CHUNK)
        sgb_t = jnp.repeat(jnp.transpose(sg_b[l]), HEAD_DIM, axis=1)
        g_ffn = norm_ffn_g[l].reshape(1, D)
        i = l // 2
        last = l == depth - 1
        dense = l % 2 == 0
        x, *routing = _mixout(attn, u, svn, xs, sgw2, sgb_t, out_norm_g[l].reshape(1, D), w_out, l,
                              g_ffn,
                              ffn=(dense_w_gate, dense_w_up, dense_w_down, i) if dense else None,
                              route=None if dense else _route_operands(router_w[i]))
        if dense:
            if last:
                x = _final_norm(x, g_final)
                outs = [x[:n_prompt], x[n_prompt:]]
        elif last:
            nb, npb = x.shape[0] // TM, n_prompt // TM
            outs = _moe(x, routing, g_ffn, moe_w_gate, moe_w_up, moe_w_down, i, g_final,
                        final_norm=True, out_blocks=[(0, npb), (npb, nb - npb)])
        else:
            pending_moe = _moe(x, routing, g_ffn, moe_w_gate, moe_w_up, moe_w_down, i, g_final,
                               final_norm=False, out_blocks=None)
        xs = (x,)
    return (outs[0].reshape(x_prompt.shape), outs[1].reshape(x_sample.shape))
```
